```python
import jax
import jax.numpy as jnp
from jax import lax
import numpy as np

D_MODEL = 1024
BATCH = 32
SEQ = 256
DEPTH = 1
DEC_BATCH = 8
DEC_SEQ = 4096
PAST_LEN = 512

GRID_W = 64
D_MIX = D_MODEL
HEAD_DIM = 64
ATTN_WIDTH = D_MIX // 2
HYENA_WIDTH = D_MIX - ATTN_WIDTH
N_HEADS = ATTN_WIDTH // HEAD_DIM
N_KV_HEADS = 2
GQA_GROUP = N_HEADS // N_KV_HEADS
HY_GROUP_DIM = 64
HY_GROUPS = HYENA_WIDTH // HY_GROUP_DIM
ROPE_THETA = 10000.0
ROPE_FREQS = HEAD_DIM // 4
Q_BLOCK = 128
HY_ORDER = 2
HY_SHORT = 3
HY_DIRS = 2
HY_BANDS = 16
HY_POS_DIM = 1 + 2 * HY_BANDS
HY_FILTER_HIDDEN = 64
HY_DECAY_MIN = 3.0
HY_DECAY_MAX = 15.0
N_GROUPS = 4
EXPERTS_PER_GROUP = 8
N_EXPERTS = N_GROUPS * EXPERTS_PER_GROUP
TOP_K_INNER = 2
EXPERT_FF = D_MODEL // 4
Q_DIM = N_HEADS * HEAD_DIM
KV_DIM = N_KV_HEADS * HEAD_DIM
IN_WIDTH = Q_DIM + 2 * KV_DIM + (HY_ORDER + 1) * HYENA_WIDTH
DEEPNORM_ALPHA = (2.0 * DEPTH) ** 0.25
DEEPNORM_BETA = (8.0 * DEPTH) ** -0.25
EPS = 1e-6

kernel_name = 'hybrid_gqa_hyena_hmoe_diffusion_step'


def rms_norm(x, gain):
    xf = x.astype(jnp.float32)
    y = xf * lax.rsqrt(jnp.mean(xf * xf, axis=-1, keepdims=True) + EPS)
    return (y * gain.astype(jnp.float32)).astype(x.dtype)


def layer_norm(x, gain, bias):
    xf = x.astype(jnp.float32)
    xc = xf - jnp.mean(xf, axis=-1, keepdims=True)
    var = jnp.mean(xc * xc, axis=-1, keepdims=True)
    y = xc * lax.rsqrt(var + EPS) * gain.astype(jnp.float32) + bias.astype(jnp.float32)
    return y.astype(x.dtype)


def adaln(cond, w_mod, b_mod):
    return jnp.split(jax.nn.silu(cond) @ w_mod + b_mod, 6, axis=-1)


def rope_2d(x):
    L = x.shape[1]
    n_rows = L // GRID_W
    rows = jnp.broadcast_to(jnp.arange(n_rows, dtype=jnp.float32)[:, None], (n_rows, GRID_W)).reshape(L)
    cols = jnp.broadcast_to(jnp.arange(GRID_W, dtype=jnp.float32)[None, :], (n_rows, GRID_W)).reshape(L)
    inv_freq = ROPE_THETA ** (-jnp.arange(ROPE_FREQS, dtype=jnp.float32) / ROPE_FREQS)
    ang = jnp.stack([rows[:, None] * inv_freq, cols[:, None] * inv_freq], axis=1)
    cos = jnp.cos(ang)[None, :, None]
    sin = jnp.sin(ang)[None, :, None]
    xf = x.astype(jnp.float32).reshape(x.shape[:3] + (2, 2, ROPE_FREQS))
    x1 = xf[..., 0, :]
    x2 = xf[..., 1, :]
    out = jnp.stack([x1 * cos - x2 * sin, x2 * cos + x1 * sin], axis=-2)
    return out.reshape(x.shape).astype(x.dtype)


def attend_blocked(q, k, v):
    B, Lq = q.shape[0], q.shape[1]
    nb = Lq // Q_BLOCK
    qb = q.reshape(B, nb, Q_BLOCK, N_KV_HEADS, GQA_GROUP, HEAD_DIM).transpose(1, 0, 2, 3, 4, 5)
    scale = HEAD_DIM ** -0.5

    def block(q_blk):
        s = jnp.einsum('bqkgd,bskd->bkgqs', q_blk, k, preferred_element_type=jnp.float32) * scale
        p = jax.nn.softmax(s, axis=-1).astype(v.dtype)
        return jnp.einsum('bkgqs,bskd->bqkgd', p, v)

    o = lax.map(block, qb)
    return o.transpose(1, 0, 2, 3, 4, 5).reshape(B, Lq, Q_DIM)


def hyena_filters(L, w1, b1, w2, b2, w3, freq, decay):
    f32 = jnp.float32
    t = jnp.arange(L, dtype=f32) / L
    bands = jnp.arange(1, HY_BANDS + 1, dtype=f32)
    ang = 2.0 * np.pi * t[:, None] * bands
    z = jnp.concatenate([t[:, None], jnp.sin(ang), jnp.cos(ang)], axis=-1)
    fr = freq.astype(f32)
    h = jnp.sin(fr * (z @ w1.astype(f32) + b1.astype(f32)))
    h = jnp.sin(fr * (h @ w2.astype(f32) + b2.astype(f32)))
    h = h @ w3.astype(f32)
    h = h * jnp.exp(-t[:, None] * jnp.abs(decay.astype(f32)))
    h = h.reshape(L, HY_DIRS, HY_ORDER, HYENA_WIDTH)
    g = jnp.concatenate([h[:, 0], jnp.zeros((1, HY_ORDER, HYENA_WIDTH), f32), h[:0:-1, 1]], axis=0)
    g = g * lax.rsqrt(jnp.sum(g * g, axis=0, keepdims=True) + EPS)
    return jnp.fft.rfft(g, axis=0)


def hyena_mixer(u, short_w, short_b, w1, b1, w2, b2, w3, freq, decay, skip):
    L = u.shape[1]
    up = jnp.pad(u, ((0, 0), (1, 1), (0, 0)))
    uc = up[:, :-2] * short_w[0] + up[:, 1:-1] * short_w[1] + up[:, 2:] * short_w[2] + short_b
    parts = jnp.split(uc.astype(jnp.float32), HY_ORDER + 1, axis=-1)
    spec = hyena_filters(L, w1, b1, w2, b2, w3, freq, decay)
    z = parts[0]
    for o in range(HY_ORDER):
        zc = jnp.fft.irfft(jnp.fft.rfft(z, n=2 * L, axis=1) * spec[None, :, o], n=2 * L, axis=1)[:, :L]
        z = parts[o + 1] * (zc + z * skip[o].astype(jnp.float32))
    return z.astype(u.dtype)


def hier_moe(x, grp_w, grp_b, exp_w, exp_b, w_gate, w_up, w_down):
    shp = x.shape
    xt = x.reshape(-1, D_MODEL)
    T = xt.shape[0]
    gl = (xt @ grp_w + grp_b).astype(jnp.float32)
    pg = jax.nn.softmax(gl, axis=-1)
    g_idx = jnp.argmax(gl, axis=-1)
    pg_top = jnp.take_along_axis(pg, g_idx[:, None], axis=1)[:, 0]
    el = (xt @ exp_w + exp_b).astype(jnp.float32).reshape(T, N_GROUPS, EXPERTS_PER_GROUP)
    sel = jnp.take_along_axis(el, g_idx[:, None, None], axis=1)[:, 0]
    vals, idx = lax.top_k(sel, TOP_K_INNER)
    w = pg_top[:, None] * jax.nn.softmax(vals, axis=-1)
    expert_id = g_idx[:, None] * EXPERTS_PER_GROUP + idx
    combine = jnp.sum(jax.nn.one_hot(expert_id, N_EXPERTS, dtype=jnp.float32) * w[..., None], axis=1)
    combine = combine.astype(x.dtype)
    y = jnp.zeros_like(xt)
    for e in range(N_EXPERTS):
        h = jax.nn.silu(xt @ w_gate[e]) * (xt @ w_up[e])
        y = y + combine[:, e:e + 1] * (h @ w_down[e])
    return y.reshape(shp)


def trunk_layer(x, mod, p, l, latent, ctx_k=None, ctx_v=None):
    B, L, _ = x.shape
    sh1, sc1, g1, sh2, sc2, g2 = mod
    h = x * (1 + sc1) + sh1
    proj = h @ p['w_in'][l]
    q, k, v, hy = jnp.split(proj, [Q_DIM, Q_DIM + KV_DIM, Q_DIM + 2 * KV_DIM], axis=-1)
    q = rms_norm(q.reshape(B, L, N_HEADS, HEAD_DIM), p['q_gain'][l])
    k = rms_norm(k.reshape(B, L, N_KV_HEADS, HEAD_DIM), p['k_gain'][l])
    v = v.reshape(B, L, N_KV_HEADS, HEAD_DIM)
    if latent:
        q = rope_2d(q)
        k_all = jnp.concatenate([rope_2d(k), ctx_k.astype(k.dtype)], axis=1)
        v_all = jnp.concatenate([v, ctx_v.astype(v.dtype)], axis=1)
    else:
        k_all, v_all = k, v
    attn = attend_blocked(q, k_all, v_all)
    attn = rms_norm(attn.reshape(B, L, N_HEADS, HEAD_DIM), p['attn_out_gain'][l].reshape(N_HEADS, HEAD_DIM))
    hyo = hyena_mixer(hy, p['hy_short_w'][l], p['hy_short_b'][l], p['hy_f_w1'][l], p['hy_f_b1'][l],
                      p['hy_f_w2'][l], p['hy_f_b2'][l], p['hy_f_w3'][l], p['hy_freq'][l], p['hy_decay'][l],
                      p['hy_skip'][l])
    hyo = rms_norm(hyo.reshape(B, L, HY_GROUPS, HY_GROUP_DIM), p['hy_out_gain'][l].reshape(HY_GROUPS, HY_GROUP_DIM))
    mix = jnp.concatenate([attn.reshape(B, L, ATTN_WIDTH), hyo.reshape(B, L, HYENA_WIDTH)], axis=-1) @ p['w_out'][l]
    x = layer_norm(DEEPNORM_ALPHA * x + g1 * mix, p['ln1_g'][l], p['ln1_b'][l])
    h2 = x * (1 + sc2) + sh2
    moe = hier_moe(h2, p['router_grp_w'][l], p['router_grp_b'][l], p['router_exp_w'][l], p['router_exp_b'][l],
                   p['exp_w_gate'][l], p['exp_w_up'][l], p['exp_w_down'][l])
    x = layer_norm(DEEPNORM_ALPHA * x + g2 * moe, p['ln2_g'][l], p['ln2_b'][l])
    return x, k, v


def setup_inputs(seed: int = 0) -> dict:
    key = jax.random.key(seed)
    ks = jax.random.split(key, 35)

    def nrm(i, shape, scale=1.0):
        return jax.random.normal(ks[i], shape, jnp.float32) * scale

    HY_IN = (HY_ORDER + 1) * HYENA_WIDTH
    HY_FILT_OUT = HY_DIRS * HY_ORDER * HYENA_WIDTH
    return {
        'x_prompt': nrm(0, (BATCH, SEQ, D_MODEL)),
        'x_sample': nrm(1, (DEC_BATCH, DEC_SEQ, D_MODEL)),
        'c': nrm(2, (DEC_BATCH, D_MODEL)),
        'cache_k': nrm(3, (DEC_BATCH, DEPTH, PAST_LEN, N_KV_HEADS, HEAD_DIM)),
        'cache_v': nrm(4, (DEC_BATCH, DEPTH, PAST_LEN, N_KV_HEADS, HEAD_DIM)),
        'c_ctx': nrm(5, (D_MODEL,)),
        'w_mod': nrm(6, (DEPTH, D_MODEL, 6 * D_MODEL), 0.5 * D_MODEL ** -0.5),
        'b_mod': nrm(7, (DEPTH, 6 * D_MODEL), 0.02),
        'w_in': nrm(8, (DEPTH, D_MODEL, IN_WIDTH), D_MODEL ** -0.5),
        'q_gain': 1.0 + nrm(9, (DEPTH, HEAD_DIM), 0.02),
        'k_gain': 1.0 + nrm(10, (DEPTH, HEAD_DIM), 0.02),
        'attn_out_gain': 1.0 + nrm(11, (DEPTH, ATTN_WIDTH), 0.02),
        'hy_short_w': nrm(12, (DEPTH, HY_SHORT, HY_IN), HY_SHORT ** -0.5),
        'hy_short_b': nrm(13, (DEPTH, HY_IN), 0.02),
        'hy_f_w1': nrm(14, (DEPTH, HY_POS_DIM, HY_FILTER_HIDDEN), HY_POS_DIM ** -0.5),
        'hy_f_b1': nrm(15, (DEPTH, HY_FILTER_HIDDEN), 0.1),
        'hy_f_w2': nrm(16, (DEPTH, HY_FILTER_HIDDEN, HY_FILTER_HIDDEN), HY_FILTER_HIDDEN ** -0.5),
        'hy_f_b2': nrm(17, (DEPTH, HY_FILTER_HIDDEN), 0.1),
        'hy_f_w3': nrm(18, (DEPTH, HY_FILTER_HIDDEN, HY_FILT_OUT), HY_FILTER_HIDDEN ** -0.5),
        'hy_freq': 1.0 + nrm(19, (DEPTH, HY_FILTER_HIDDEN), 0.1),
        'hy_decay': jax.random.uniform(ks[20], (DEPTH, HY_FILT_OUT), jnp.float32, HY_DECAY_MIN, HY_DECAY_MAX),
        'hy_skip': nrm(21, (DEPTH, HY_ORDER, HYENA_WIDTH), 0.5),
        'hy_out_gain': 1.0 + nrm(22, (DEPTH, HYENA_WIDTH), 0.02),
        'w_out': nrm(23, (DEPTH, D_MIX, D_MODEL), DEEPNORM_BETA * D_MIX ** -0.5),
        'ln1_g': 1.0 + nrm(24, (DEPTH, D_MODEL), 0.02),
        'ln1_b': nrm(25, (DEPTH, D_MODEL), 0.02),
        'router_grp_w': nrm(26, (DEPTH, D_MODEL, N_GROUPS), D_MODEL ** -0.5),
        'router_grp_b': nrm(27, (DEPTH, N_GROUPS), 0.01),
        'router_exp_w': nrm(28, (DEPTH, D_MODEL, N_EXPERTS), D_MODEL ** -0.5),
        'router_exp_b': nrm(29, (DEPTH, N_EXPERTS), 0.01),
        'exp_w_gate': nrm(30, (DEPTH, N_EXPERTS, D_MODEL, EXPERT_FF), D_MODEL ** -0.5),
        'exp_w_up': nrm(31, (DEPTH, N_EXPERTS, D_MODEL, EXPERT_FF), D_MODEL ** -0.5),
        'exp_w_down': nrm(32, (DEPTH, N_EXPERTS, EXPERT_FF, D_MODEL), DEEPNORM_BETA * EXPERT_FF ** -0.5),
        'ln2_g': 1.0 + nrm(33, (DEPTH, D_MODEL), 0.02),
        'ln2_b': nrm(34, (DEPTH, D_MODEL), 0.02),
    }


def reference(x_prompt, x_sample, c, cache_k, cache_v, c_ctx, w_mod, b_mod, w_in, q_gain, k_gain,
              attn_out_gain, hy_short_w, hy_short_b, hy_f_w1, hy_f_b1, hy_f_w2, hy_f_b2, hy_f_w3, hy_freq,
              hy_decay, hy_skip, hy_out_gain, w_out, ln1_g, ln1_b, router_grp_w, router_grp_b, router_exp_w,
              router_exp_b, exp_w_gate, exp_w_up, exp_w_down, ln2_g, ln2_b):
    p = {
        'w_in': w_in, 'q_gain': q_gain, 'k_gain': k_gain, 'attn_out_gain': attn_out_gain,
        'hy_short_w': hy_short_w, 'hy_short_b': hy_short_b, 'hy_f_w1': hy_f_w1, 'hy_f_b1': hy_f_b1,
        'hy_f_w2': hy_f_w2, 'hy_f_b2': hy_f_b2, 'hy_f_w3': hy_f_w3, 'hy_freq': hy_freq, 'hy_decay': hy_decay,
        'hy_skip': hy_skip, 'hy_out_gain': hy_out_gain, 'w_out': w_out, 'ln1_g': ln1_g, 'ln1_b': ln1_b,
        'router_grp_w': router_grp_w, 'router_grp_b': router_grp_b, 'router_exp_w': router_exp_w,
        'router_exp_b': router_exp_b, 'exp_w_gate': exp_w_gate, 'exp_w_up': exp_w_up,
        'exp_w_down': exp_w_down, 'ln2_g': ln2_g, 'ln2_b': ln2_b,
    }
    y_prompt = x_prompt
    ks_new = []
    vs_new = []
    for l in range(DEPTH):
        mod_ctx = adaln(c_ctx[None, None, :], w_mod[l], b_mod[l])
        y_prompt, k_l, v_l = trunk_layer(y_prompt, mod_ctx, p, l, False)
        ks_new.append(k_l)
        vs_new.append(v_l)
    new_cache_k = jnp.stack(ks_new, axis=1)
    new_cache_v = jnp.stack(vs_new, axis=1)
    y_sample = x_sample
    for l in range(DEPTH):
        mod_lat = adaln(c[:, None, :], w_mod[l], b_mod[l])
        y_sample, _, _ = trunk_layer(y_sample, mod_lat, p, l, True, cache_k[:, l], cache_v[:, l])
    return (y_prompt, y_sample, new_cache_k, new_cache_v)
```

```python
import functools

import numpy as np
import jax
import jax.numpy as jnp
from jax import lax
from jax.experimental import pallas as pl
from jax.experimental.pallas import tpu as pltpu

F32 = jnp.float32
BF16 = jnp.bfloat16

D_MODEL = 1024
GRID_W = 64
HEAD_DIM = 64
N_HEADS = 8
N_KV_HEADS = 2
GQA_GROUP = N_HEADS // N_KV_HEADS
Q_DIM = N_HEADS * HEAD_DIM
KV_DIM = N_KV_HEADS * HEAD_DIM
QK_DIM = Q_DIM + KV_DIM
HYENA_WIDTH = 512
HY_ORDER = 2
HY_IN = (HY_ORDER + 1) * HYENA_WIDTH
HY_GROUP_DIM = 64
HY_BANDS = 16
HY_POS_DIM = 1 + 2 * HY_BANDS
HY_FILTER_HIDDEN = 64
HY_OC = HY_ORDER * HYENA_WIDTH
IN_WIDTH = Q_DIM + 2 * KV_DIM + HY_IN
ROPE_THETA = 10000.0
ROPE_FREQS = HEAD_DIM // 4
N_GROUPS = 4
EXPERTS_PER_GROUP = 8
N_EXPERTS = N_GROUPS * EXPERTS_PER_GROUP
EXPERT_FF = D_MODEL // 4
DEPTH = 1
DEEPNORM_ALPHA = (2.0 * DEPTH) ** 0.25
EPS = 1e-6

LANES = 128
SUBLANES = 8
VMEM_LIMIT = 56 * 1024 * 1024
EXPERT_BLOCK = 4
ATTN_SUBTILES = 2
NEG_BIG = -1e30


def _cparams(sem):
    return pltpu.CompilerParams(dimension_semantics=sem, vmem_limit_bytes=VMEM_LIMIT)


def _split_bf16(a):
    hi = a.astype(BF16)
    lo = (a - hi.astype(F32)).astype(BF16)
    return hi, lo


def _dot(a, b):
    return jnp.dot(a, b, preferred_element_type=F32)


def _dot3(a, b):
    ah, al = _split_bf16(a)
    bh, bl = _split_bf16(b)
    return _dot(ah, bh) + _dot(al, bh) + _dot(ah, bl)


def _dot3_pre(ah, al, b):
    bh, bl = _split_bf16(b)
    return _dot(ah, bh) + _dot(al, bh) + _dot(ah, bl)


def _silu(x):
    return x / (1.0 + jnp.exp(-x))


def _layer_norm(y, g, b):
    mu = jnp.mean(y, axis=-1, keepdims=True)
    yc = y - mu
    var = jnp.mean(yc * yc, axis=-1, keepdims=True)
    return yc * lax.rsqrt(var + EPS) * g + b


def _adaln_kernel(c_ref, w_ref, b_ref, o_ref):
    o_ref[...] = _dot3(_silu(c_ref[...]), w_ref[...]) + b_ref[...]


def _adaln(cond, w_mod, b_mod):
    rows = cond.shape[0]
    n = w_mod.shape[1]
    tn = 1536
    return pl.pallas_call(
        _adaln_kernel,
        out_shape=jax.ShapeDtypeStruct((rows, n), F32),
        grid=(n // tn,),
        in_specs=[
            pl.BlockSpec((rows, D_MODEL), lambda j: (0, 0)),
            pl.BlockSpec((D_MODEL, tn), lambda j: (0, j)),
            pl.BlockSpec((1, tn), lambda j: (0, j)),
        ],
        out_specs=pl.BlockSpec((rows, tn), lambda j: (0, j)),
        compiler_params=_cparams(("arbitrary",)),
    )(cond, w_mod, b_mod.reshape(1, n))


def _rope_tables(seq):
    t = np.arange(seq)
    rows = (t // GRID_W).astype(np.float64)
    cols = (t % GRID_W).astype(np.float64)
    inv_freq = ROPE_THETA ** (-np.arange(ROPE_FREQS, dtype=np.float64) / ROPE_FREQS)
    d = np.arange(LANES) % HEAD_DIM
    axis = d // (2 * ROPE_FREQS)
    f = d % ROPE_FREQS
    pos = np.where(axis[None, :] == 0, rows[:, None], cols[:, None])
    ang = pos * inv_freq[f][None, :]
    first = (d % (2 * ROPE_FREQS)) < ROPE_FREQS
    cos = np.cos(ang)
    sin = np.where(first[None, :], -np.sin(ang), np.sin(ang))
    return jnp.asarray(cos, F32), jnp.asarray(sin, F32)


def _inproj_kernel(latent, x_ref, mod_ref, w_ref, gain_ref, bd_ref, *rest):
    if latent:
        cos_ref, sin_ref, q_ref, kt_ref, v_ref, hy_ref = rest
    else:
        q_ref, kt_ref, v_ref, hy_ref, knat_ref = rest
    m = mod_ref[...]
    h = x_ref[...] * (1.0 + m[1:2]) + m[0:1]
    proj = _dot(h.astype(BF16), w_ref[...])
    qk = proj[:, :QK_DIM]
    ms = _dot((qk * qk).astype(BF16), bd_ref[...])
    qk = qk * lax.rsqrt(ms + EPS) * gain_ref[...]
    if not latent:
        knat_ref[...] = qk[:, Q_DIM:]
    else:
        cos = cos_ref[...]
        sin = sin_ref[...]
        lane = lax.broadcasted_iota(jnp.int32, cos.shape, 1)
        first = (lane % (2 * ROPE_FREQS)) < ROPE_FREQS
        chunks = []
        for c in range(QK_DIM // LANES):
            xc = qk[:, c * LANES:(c + 1) * LANES]
            below = pltpu.roll(xc, ROPE_FREQS, axis=1)
            above = pltpu.roll(xc, LANES - ROPE_FREQS, axis=1)
            chunks.append(xc * cos + jnp.where(first, above, below) * sin)
        qk = jnp.concatenate(chunks, axis=1)
    qs = (qk[:, :Q_DIM] * (HEAD_DIM ** -0.5)).astype(BF16)
    for hd in range(N_HEADS):
        q_ref[hd] = qs[:, hd * HEAD_DIM:(hd + 1) * HEAD_DIM]
    kt = qk[:, Q_DIM:].T
    for kh in range(N_KV_HEADS):
        kt_ref[kh] = kt[kh * HEAD_DIM:(kh + 1) * HEAD_DIM].astype(BF16)
    v_ref[...] = proj[:, QK_DIM:QK_DIM + KV_DIM]
    hy_ref[...] = proj[:, QK_DIM + KV_DIM:]


def _inproj(x, mod, w_in, qk_gain, bd_qk, latent, tm):
    nb, seq, _ = x.shape
    grid = (nb, seq // tm)
    mod_map = (lambda b, i: (b, 0, 0)) if latent else (lambda b, i: (0, 0, 0))
    in_specs = [
        pl.BlockSpec((None, tm, D_MODEL), lambda b, i: (b, i, 0)),
        pl.BlockSpec((None, 6, D_MODEL), mod_map),
        pl.BlockSpec((D_MODEL, IN_WIDTH), lambda b, i: (0, 0)),
        pl.BlockSpec((1, QK_DIM), lambda b, i: (0, 0)),
        pl.BlockSpec((QK_DIM, QK_DIM), lambda b, i: (0, 0)),
    ]
    args = [x, mod, w_in, qk_gain, bd_qk]
    out_shape = [
        jax.ShapeDtypeStruct((nb, N_HEADS, seq, HEAD_DIM), BF16),
        jax.ShapeDtypeStruct((nb, N_KV_HEADS, HEAD_DIM, seq), BF16),
        jax.ShapeDtypeStruct((nb, seq, KV_DIM), F32),
        jax.ShapeDtypeStruct((nb, seq, HY_IN), F32),
    ]
    out_specs = [
        pl.BlockSpec((None, N_HEADS, tm, HEAD_DIM), lambda b, i: (b, 0, i, 0)),
        pl.BlockSpec((None, N_KV_HEADS, HEAD_DIM, tm), lambda b, i: (b, 0, 0, i)),
        pl.BlockSpec((None, tm, KV_DIM), lambda b, i: (b, i, 0)),
        pl.BlockSpec((None, tm, HY_IN), lambda b, i: (b, i, 0)),
    ]
    if latent:
        cos, sin = _rope_tables(seq)
        in_specs += [pl.BlockSpec((tm, LANES), lambda b, i: (i, 0))] * 2
        args += [cos, sin]
    else:
        out_shape.append(jax.ShapeDtypeStruct((nb, seq, KV_DIM), F32))
        out_specs.append(pl.BlockSpec((None, tm, KV_DIM), lambda b, i: (b, i, 0)))
    return pl.pallas_call(
        functools.partial(_inproj_kernel, latent),
        out_shape=out_shape,
        grid=grid,
        in_specs=in_specs,
        out_specs=out_specs,
        compiler_params=_cparams(("parallel", "parallel")),
    )(*args)


def _attn_kernel(n_ctx, chunk, q_ref, kt_ref, v_ref, gain_ref, *rest):
    if n_ctx:
        ckt_ref, cv_ref, o_ref = rest
    else:
        (o_ref,) = rest
    kh = pl.program_id(1)
    g, tq, _ = q_ref.shape
    seq = kt_ref.shape[1]
    pieces = [(kt_ref[:, c * chunk:(c + 1) * chunk], v_ref[c * chunk:(c + 1) * chunk, :])
              for c in range(seq // chunk)]
    if n_ctx:
        pieces.append((ckt_ref[...], cv_ref[...]))
    ts = tq // ATTN_SUBTILES
    for t in range(ATTN_SUBTILES):
        rows = slice(t * ts, (t + 1) * ts)
        qs = q_ref[:, rows, :].reshape(g * ts, HEAD_DIM)
        m = den = acc = None
        for kt_c, v_c in pieces:
            s = _dot(qs, kt_c)
            row_max = jnp.max(s, axis=1, keepdims=True)
            m_new = row_max if m is None else jnp.maximum(m, row_max)
            p = jnp.exp(s - m_new)
            p_sum = jnp.sum(p, axis=1, keepdims=True)
            pv = _dot(p.astype(BF16), v_c.astype(BF16))
            if m is None:
                den, acc = p_sum, pv
            else:
                alpha = jnp.exp(m - m_new)
                den = alpha * den + p_sum
                acc = alpha * acc + pv
            m = m_new
        o = jnp.where(kh == 0, acc[:, :HEAD_DIM], acc[:, HEAD_DIM:]) / den
        o = o * lax.rsqrt(jnp.mean(o * o, axis=1, keepdims=True) + EPS)
        for i in range(g):
            cols = slice(i * HEAD_DIM, (i + 1) * HEAD_DIM)
            o_ref[rows, cols] = (o[i * ts:(i + 1) * ts] * gain_ref[:, cols]).astype(o_ref.dtype)


def _attention(q, kt, v, gain, ckt, cv, tq, chunk):
    nb, _, seq, _ = q.shape
    n_ctx = 0 if ckt is None else ckt.shape[-1]
    width = GQA_GROUP * HEAD_DIM
    in_specs = [
        pl.BlockSpec((None, GQA_GROUP, tq, HEAD_DIM), lambda b, k, i: (b, k, i, 0)),
        pl.BlockSpec((None, None, HEAD_DIM, seq), lambda b, k, i: (b, k, 0, 0)),
        pl.BlockSpec((None, seq, KV_DIM), lambda b, k, i: (b, 0, 0)),
        pl.BlockSpec((1, width), lambda b, k, i: (0, k)),
    ]
    args = [q, kt, v, gain]
    if n_ctx:
        in_specs += [
            pl.BlockSpec((None, None, HEAD_DIM, n_ctx), lambda b, k, i: (b, k, 0, 0)),
            pl.BlockSpec((None, n_ctx, KV_DIM), lambda b, k, i: (b, 0, 0)),
        ]
        args += [ckt, cv]
    return pl.pallas_call(
        functools.partial(_attn_kernel, n_ctx, chunk),
        out_shape=jax.ShapeDtypeStruct((nb, seq, Q_DIM), BF16),
        grid=(nb, N_KV_HEADS, seq // tq),
        in_specs=in_specs,
        out_specs=pl.BlockSpec((None, tq, width), lambda b, k, i: (b, i, k)),
        compiler_params=_cparams(("parallel", "parallel", "parallel")),
    )(*args)


def _hyena_decimation(seq):
    return 8 if seq >= 2048 else 1


def _dft_tables(seq, r):
    n_sub = 2 * seq // r
    half = n_sub // 2
    k = np.arange(half, dtype=np.float64)[:, None]
    m = np.arange(n_sub, dtype=np.float64)[None, :]
    ang = 2.0 * np.pi * k * m / n_sub
    fwd = np.concatenate([np.cos(ang), -np.sin(ang)], axis=0)
    fwd[half] = np.cos(np.pi * m[0])
    inv = fwd.T.copy() * (2.0 / n_sub)
    inv[:, 0] *= 0.5
    inv[:, half] *= 0.5
    inv = inv[:seq // r]
    kk = np.arange(half, dtype=np.float64)[:, None] * np.ones((1, LANES))
    tw_r = np.cos(2.0 * np.pi * kk / n_sub)
    tw_i = -np.sin(2.0 * np.pi * kk / n_sub)
    return fwd, inv, tw_r, tw_i


def _filter_positions(seq, r):
    n_tot = 2 * seq
    n = (np.arange(n_tot // r)[None, :] * r + np.arange(r)[:, None]).reshape(-1)
    j = np.where(n < seq, n, n_tot - n)
    t = j.astype(np.float64) / seq
    bands = np.arange(1, HY_BANDS + 1, dtype=np.float64)
    ang = 2.0 * np.pi * t[:, None] * bands
    z = np.concatenate([t[:, None], np.sin(ang), np.cos(ang)], axis=-1)
    ones = np.ones((1, HY_FILTER_HIDDEN))
    sel_f = (n < seq).astype(np.float64)[:, None] * ones
    sel_b = (n > seq).astype(np.float64)[:, None] * ones
    return z, t[:, None] * np.ones((1, LANES)), sel_f, sel_b


def _filter_ffn_kernel(z_ref, self_ref, selb_ref, w1_ref, b1_ref, w2_ref, b2_ref, fr_ref, hf_ref, hb_ref):
    fr = fr_ref[...]
    h = jnp.sin(fr * (_dot3(z_ref[...], w1_ref[...]) + b1_ref[...]))
    h = jnp.sin(fr * (_dot3(h, w2_ref[...]) + b2_ref[...]))
    hf_ref[...] = h * self_ref[...]
    hb_ref[...] = h * selb_ref[...]


def _filter_spec_kernel(r, hf_ref, hb_ref, t_ref, w3f_ref, w3b_ref, dcf_ref, dcb_ref, fh_ref, fl_ref,
                        gre_ref, gim_ref):
    t = t_ref[...]
    g = (_dot3(hf_ref[...], w3f_ref[...]) * jnp.exp(-t * jnp.abs(dcf_ref[...]))
         + _dot3(hb_ref[...], w3b_ref[...]) * jnp.exp(-t * jnp.abs(dcb_ref[...])))
    g = g * lax.rsqrt(jnp.sum(g * g, axis=0, keepdims=True) + EPS)
    n_sub = g.shape[0] // r
    half = n_sub // 2
    fh = fh_ref[...]
    fl = fl_ref[...]
    for p in range(r):
        spec = _dot3_pre(fh, fl, g[p * n_sub:(p + 1) * n_sub])
        gre_ref[p] = spec[:half]
        gim_ref[p] = spec[half:]


def _hyena_filters(seq, r, w1, b1, w2, b2, w3, freq, decay, fwd_hi, fwd_lo):
    n_tot = 2 * seq
    n_sub = n_tot // r
    half = n_sub // 2
    z, t, sel_f, sel_b = _filter_positions(seq, r)
    pad = (-HY_POS_DIM) % SUBLANES
    z = jnp.asarray(np.pad(z, ((0, 0), (0, pad))), F32)
    w1p = jnp.pad(w1, ((0, pad), (0, 0)))
    kin = HY_POS_DIM + pad
    hid = HY_FILTER_HIDDEN
    tr = min(n_tot, 512)
    rows = lambda width: pl.BlockSpec((tr, width), lambda i: (i, 0))
    full = lambda shape: pl.BlockSpec(shape, lambda j: (0,) * len(shape))
    hf, hb = pl.pallas_call(
        _filter_ffn_kernel,
        out_shape=[jax.ShapeDtypeStruct((n_tot, hid), F32)] * 2,
        grid=(n_tot // tr,),
        in_specs=[rows(kin), rows(hid), rows(hid), full((kin, hid)), full((1, hid)), full((hid, hid)),
                  full((1, hid)), full((1, hid))],
        out_specs=[rows(hid)] * 2,
        compiler_params=_cparams(("parallel",)),
    )(z, jnp.asarray(sel_f, F32), jnp.asarray(sel_b, F32), w1p, b1.reshape(1, hid), w2, b2.reshape(1, hid),
      freq.reshape(1, hid))
    ncb = HY_OC // LANES
    return pl.pallas_call(
        functools.partial(_filter_spec_kernel, r),
        out_shape=[jax.ShapeDtypeStruct((r, half, HY_OC), F32)] * 2,
        grid=(ncb,),
        in_specs=[
            full((n_tot, hid)), full((n_tot, hid)), full((n_tot, LANES)),
            pl.BlockSpec((hid, LANES), lambda j: (0, j)),
            pl.BlockSpec((hid, LANES), lambda j: (0, j + ncb)),
            pl.BlockSpec((1, LANES), lambda j: (0, j)),
            pl.BlockSpec((1, LANES), lambda j: (0, j + ncb)),
            full((2 * half, n_sub)), full((2 * half, n_sub)),
        ],
        out_specs=[pl.BlockSpec((r, half, LANES), lambda j: (0, 0, j))] * 2,
        compiler_params=_cparams(("parallel",)),
    )(hf, hb, jnp.asarray(t, F32), w3, w3, decay.reshape(1, -1), decay.reshape(1, -1), fwd_hi, fwd_lo)


def _hyena_kernel(r, hy0_ref, hy1_ref, hy2_ref, sw_ref, sb_ref, skip_ref,
                  gre0_ref, gim0_ref, gre1_ref, gim1_ref, fwd_ref, inv_ref, twr_ref, twi_ref,
                  o_ref, z_ref, rhs_ref, x_ref):
    seq = hy0_ref.shape[0]
    m_len = seq // r
    half = fwd_ref.shape[0] // 2
    row = lax.broadcasted_iota(jnp.int32, (m_len, LANES), 0)
    hy_refs = (hy0_ref, hy1_ref, hy2_ref)

    def phase(ref, j):
        if r == 1:
            return ref[...]
        return ref[pl.ds(j, m_len, stride=r), :]

    def short_conv(part, j):
        ref = hy_refs[part]
        w = sw_ref[:, part * LANES:(part + 1) * LANES]
        b = sb_ref[:, part * LANES:(part + 1) * LANES]
        if j > 0:
            prev = phase(ref, j - 1)
        else:
            prev = jnp.where(row == 0, 0.0, pltpu.roll(phase(ref, r - 1), 1, axis=0))
        if j < r - 1:
            nxt = phase(ref, j + 1)
        else:
            nxt = jnp.where(row == m_len - 1, 0.0, pltpu.roll(phase(ref, 0), m_len - 1, axis=0))
        return prev * w[0:1] + phase(ref, j) * w[1:2] + nxt * w[2:3] + b

    for j in range(r):
        z_ref[j] = short_conv(0, j)

    for o, (gre_ref, gim_ref) in enumerate(((gre0_ref, gim0_ref), (gre1_ref, gim1_ref))):
        for j in range(r):
            rhs_ref[:, j * LANES:(j + 1) * LANES] = z_ref[j].astype(BF16)
        x_ref[...] = _dot(fwd_ref[...], rhs_ref[...])
        dc = [x_ref[0:1, j * LANES:(j + 1) * LANES] for j in range(r)]
        ny = [x_ref[half:half + 1, j * LANES:(j + 1) * LANES] for j in range(r)]
        y_dc, y_ny = [], []
        for j in range(r):
            a = jnp.zeros((1, LANES), F32)
            c = jnp.zeros((1, LANES), F32)
            for jp in range(r):
                p = (j - jp) % r
                a = a + gre_ref[p, 0:1, :] * dc[jp]
                t = gim_ref[p, 0:1, :] * ny[jp]
                c = c + t if jp <= j else c - t
            y_dc.append(a)
            y_ny.append(c)

        def mix(i, carry):
            r0 = pl.multiple_of(i * SUBLANES, SUBLANES)
            rows_re = pl.ds(r0, SUBLANES)
            rows_im = pl.ds(half + r0, SUBLANES)
            xr = [x_ref[rows_re, j * LANES:(j + 1) * LANES] for j in range(r)]
            xi = [x_ref[rows_im, j * LANES:(j + 1) * LANES] for j in range(r)]
            wr = twr_ref[rows_re, :]
            wi = twi_ref[rows_re, :]
            for j in range(r):
                pr = pi = qr = qi = None
                for jp in range(r):
                    p = (j - jp) % r
                    gr = gre_ref[p, rows_re, :]
                    gi = gim_ref[p, rows_re, :]
                    tr = gr * xr[jp] - gi * xi[jp]
                    ti = gr * xi[jp] + gi * xr[jp]
                    if jp <= j:
                        pr = tr if pr is None else pr + tr
                        pi = ti if pi is None else pi + ti
                    else:
                        qr = tr if qr is None else qr + tr
                        qi = ti if qi is None else qi + ti
                if qr is not None:
                    pr = pr + wr * qr - wi * qi
                    pi = pi + wr * qi + wi * qr
                x_ref[rows_re, j * LANES:(j + 1) * LANES] = pr
                x_ref[rows_im, j * LANES:(j + 1) * LANES] = pi
            return carry

        lax.fori_loop(0, half // SUBLANES, mix, 0)
        for j in range(r):
            x_ref[0:1, j * LANES:(j + 1) * LANES] = y_dc[j]
            x_ref[half:half + 1, j * LANES:(j + 1) * LANES] = y_ny[j]
        y = _dot(inv_ref[...], x_ref[...].astype(BF16))
        sk = skip_ref[o:o + 1, :]
        for j in range(r):
            z_ref[j] = short_conv(o + 1, j) * (y[:, j * LANES:(j + 1) * LANES] + z_ref[j] * sk)
    for j in range(r):
        if r == 1:
            o_ref[...] = z_ref[j]
        else:
            o_ref[pl.ds(j, m_len, stride=r), :] = z_ref[j]


def _hyena(hy, short_w, short_b, skip, gre, gim, fwd, inv, tw_r, tw_i, r):
    nb, seq, _ = hy.shape
    m_len = seq // r
    n_half2 = fwd.shape[0]
    half = n_half2 // 2
    ncb = HYENA_WIDTH // LANES
    parts = HY_ORDER + 1
    once = pl.Buffered(1)
    hy_spec = lambda part: pl.BlockSpec((None, seq, LANES), lambda c, b: (b, 0, part * ncb + c))
    g_spec = lambda o: pl.BlockSpec((r, half, LANES), lambda c, b: (0, 0, o * ncb + c), pipeline_mode=once)
    const = lambda shape: pl.BlockSpec(shape, lambda c, b: (0,) * len(shape), pipeline_mode=once)
    sw = short_w.reshape(3, parts, ncb, LANES).transpose(2, 0, 1, 3).reshape(ncb, 3, parts * LANES)
    sb = short_b.reshape(1, parts, ncb, LANES).transpose(2, 0, 1, 3).reshape(ncb, 1, parts * LANES)
    return pl.pallas_call(
        functools.partial(_hyena_kernel, r),
        out_shape=jax.ShapeDtypeStruct((nb, seq, HYENA_WIDTH), F32),
        grid=(ncb, nb),
        in_specs=[
            hy_spec(0), hy_spec(1), hy_spec(2),
            pl.BlockSpec((None, 3, parts * LANES), lambda c, b: (c, 0, 0)),
            pl.BlockSpec((None, 1, parts * LANES), lambda c, b: (c, 0, 0)),
            pl.BlockSpec((HY_ORDER, LANES), lambda c, b: (0, c)),
            g_spec(0), g_spec(0), g_spec(1), g_spec(1),
            const((n_half2, m_len)), const((m_len, n_half2)),
            const((half, LANES)), const((half, LANES)),
        ],
        out_specs=pl.BlockSpec((None, seq, LANES), lambda c, b: (b, 0, c)),
        scratch_shapes=[
            pltpu.VMEM((r, m_len, LANES), F32),
            pltpu.VMEM((m_len, r * LANES), BF16),
            pltpu.VMEM((n_half2, r * LANES), F32),
        ],
        compiler_params=_cparams(("parallel", "parallel")),
    )(hy, hy, hy, sw, sb, skip, gre, gim, gre, gim, fwd, inv, tw_r, tw_i)


def _route(logits):
    lane = lax.broadcasted_iota(jnp.int32, logits.shape, 1).astype(F32)
    big = jnp.float32(1e9)
    is_grp = (lane >= N_EXPERTS) & (lane < N_EXPERTS + N_GROUPS)
    gl = jnp.where(is_grp, logits, NEG_BIG)
    gmax = jnp.max(gl, axis=1, keepdims=True)
    gidx = jnp.min(jnp.where(gl == gmax, lane, big), axis=1, keepdims=True) - N_EXPERTS
    den = jnp.sum(jnp.where(is_grp, jnp.exp(gl - gmax), 0.0), axis=1, keepdims=True)
    pg_top = 1.0 / den
    lo = gidx * EXPERTS_PER_GROUP
    sel = jnp.where((lane >= lo) & (lane < lo + EXPERTS_PER_GROUP), logits, NEG_BIG)
    m1 = jnp.max(sel, axis=1, keepdims=True)
    i1 = jnp.min(jnp.where(sel == m1, lane, big), axis=1, keepdims=True)
    sel2 = jnp.where(lane == i1, NEG_BIG, sel)
    m2 = jnp.max(sel2, axis=1, keepdims=True)
    i2 = jnp.min(jnp.where(sel2 == m2, lane, big), axis=1, keepdims=True)
    e2 = jnp.exp(m2 - m1)
    w1 = pg_top / (1.0 + e2)
    w2 = pg_top * e2 / (1.0 + e2)
    return jnp.where(lane == i1, w1, 0.0) + jnp.where(lane == i2, w2, 0.0)


def _outproj_kernel(attn_ref, hyz_ref, x_ref, mod_ref, wo_ref, hg_ref, bd_ref, lng_ref, lnb_ref,
                    rwh_ref, rwl_ref, rb_ref, x1_ref, h2_ref, comb_ref):
    m = mod_ref[...]
    z = hyz_ref[...]
    ms = _dot((z * z).astype(BF16), bd_ref[...])
    zn = (z * lax.rsqrt(ms + EPS) * hg_ref[...]).astype(BF16)
    half = wo_ref.shape[0] // 2
    mix = _dot(attn_ref[...], wo_ref[:half, :]) + _dot(zn, wo_ref[half:, :])
    x1 = _layer_norm(DEEPNORM_ALPHA * x_ref[...] + m[2:3] * mix, lng_ref[...], lnb_ref[...])
    h2 = x1 * (1.0 + m[4:5]) + m[3:4]
    x1_ref[...] = x1
    h2_ref[...] = h2.astype(BF16)
    hh, hl = _split_bf16(h2)
    logits = _dot(hh, rwh_ref[...]) + _dot(hl, rwh_ref[...]) + _dot(hh, rwl_ref[...]) + rb_ref[...]
    comb_ref[...] = _route(logits)


def _outproj(attn, hyz, x, mod, w_out, hy_gain, bd_hy, ln_g, ln_b, rw_hi, rw_lo, rb, latent, tm):
    nb, seq, _ = x.shape
    mod_map = (lambda b, i: (b, 0, 0)) if latent else (lambda b, i: (0, 0, 0))
    const = lambda shape: pl.BlockSpec(shape, lambda b, i: (0,) * len(shape))
    tok = lambda width: pl.BlockSpec((None, tm, width), lambda b, i: (b, i, 0))
    return pl.pallas_call(
        _outproj_kernel,
        out_shape=[
            jax.ShapeDtypeStruct((nb, seq, D_MODEL), F32),
            jax.ShapeDtypeStruct((nb, seq, D_MODEL), BF16),
            jax.ShapeDtypeStruct((nb, seq, LANES), F32),
        ],
        grid=(nb, seq // tm),
        in_specs=[
            tok(Q_DIM), tok(HYENA_WIDTH), tok(D_MODEL),
            pl.BlockSpec((None, 6, D_MODEL), mod_map),
            const((D_MODEL, D_MODEL)), const((1, HYENA_WIDTH)), const((HYENA_WIDTH, HYENA_WIDTH)),
            const((1, D_MODEL)), const((1, D_MODEL)),
            const((D_MODEL, LANES)), const((D_MODEL, LANES)), const((1, LANES)),
        ],
        out_specs=[tok(D_MODEL), tok(D_MODEL), tok(LANES)],
        compiler_params=_cparams(("parallel", "parallel")),
    )(attn, hyz, x, mod, w_out, hy_gain, bd_hy, ln_g, ln_b, rw_hi, rw_lo, rb)


def _moe_kernel(h2_ref, comb_ref, wg_ref, wu_ref, wd_ref, x1_ref, mod_ref, lng_ref, lnb_ref, o_ref, acc_ref):
    e = pl.program_id(1)

    @pl.when(e == 0)
    def _():
        acc_ref[...] = jnp.zeros_like(acc_ref)

    x = h2_ref[...]
    comb = comb_ref[...]
    lane = lax.broadcasted_iota(jnp.int32, comb.shape, 1)
    hs = []
    for k in range(EXPERT_BLOCK):
        c = jnp.sum(jnp.where(lane == e * EXPERT_BLOCK + k, comb, 0.0), axis=1, keepdims=True)
        hk = _silu(_dot(x, wg_ref[k])) * _dot(x, wu_ref[k]) * c
        hs.append(hk.astype(BF16))
    hcat = jnp.concatenate(hs, axis=1)
    wd = wd_ref[...].reshape(EXPERT_BLOCK * EXPERT_FF, D_MODEL)
    acc_ref[...] += _dot(hcat, wd)

    @pl.when(e == pl.num_programs(1) - 1)
    def _():
        m = mod_ref[...]
        y = DEEPNORM_ALPHA * x1_ref[...] + m[5:6] * acc_ref[...]
        o_ref[...] = _layer_norm(y, lng_ref[...], lnb_ref[...])


def _moe(h2, comb, wg, wu, wd, x1, mod, ln_g, ln_b, tiles_per_mod, tm):
    tokens = h2.shape[0]
    const = lambda shape: pl.BlockSpec(shape, lambda i, e: (0,) * len(shape))
    tok = lambda width: pl.BlockSpec((tm, width), lambda i, e: (i, 0))
    return pl.pallas_call(
        _moe_kernel,
        out_shape=jax.ShapeDtypeStruct((tokens, D_MODEL), F32),
        grid=(tokens // tm, N_EXPERTS // EXPERT_BLOCK),
        in_specs=[
            tok(D_MODEL), tok(LANES),
            pl.BlockSpec((EXPERT_BLOCK, D_MODEL, EXPERT_FF), lambda i, e: (e, 0, 0)),
            pl.BlockSpec((EXPERT_BLOCK, D_MODEL, EXPERT_FF), lambda i, e: (e, 0, 0)),
            pl.BlockSpec((EXPERT_BLOCK, EXPERT_FF, D_MODEL), lambda i, e: (e, 0, 0)),
            tok(D_MODEL),
            pl.BlockSpec((None, 6, D_MODEL), lambda i, e: (i // tiles_per_mod, 0, 0)),
            const((1, D_MODEL)), const((1, D_MODEL)),
        ],
        out_specs=tok(D_MODEL),
        scratch_shapes=[pltpu.VMEM((tm, D_MODEL), F32)],
        compiler_params=_cparams(("parallel", "arbitrary")),
    )(h2, comb, wg, wu, wd, x1, mod, ln_g, ln_b)


def _block_diag_mean(width, group):
    idx = np.arange(width) // group
    return jnp.asarray((idx[:, None] == idx[None, :]).astype(np.float32) / group, BF16)


def _pick_tile(n, target):
    t = min(n, target)
    while n % t:
        t //= 2
    return t


def _trunk_layer(x, mod, p, latent, ctx_k=None, ctx_v=None):
    nb, seq, _ = x.shape
    tm = _pick_tile(seq, 512)
    q, kt, v, hy, *knat = _inproj(x, mod, p['w_in'], p['qk_gain'], p['bd_qk'], latent, tm)
    if latent:
        ckt = jnp.transpose(ctx_k, (0, 2, 3, 1)).astype(BF16)
        cv = ctx_v.reshape(nb, -1, KV_DIM)
    else:
        ckt = cv = None
    attn = _attention(q, kt, v, p['attn_gain'], ckt, cv, _pick_tile(seq, 256), _pick_tile(seq, 1024))

    r = _hyena_decimation(seq)
    fwd, inv, tw_r, tw_i = _dft_tables(seq, r)
    fwd_hi, fwd_lo = _split_bf16(jnp.asarray(fwd, F32))
    gre, gim = _hyena_filters(seq, r, p['hy_f_w1'], p['hy_f_b1'], p['hy_f_w2'], p['hy_f_b2'], p['hy_f_w3'],
                              p['hy_freq'], p['hy_decay'], fwd_hi, fwd_lo)
    hyz = _hyena(hy, p['hy_short_w'], p['hy_short_b'], p['hy_skip'], gre, gim,
                 fwd_hi[:, :seq // r], jnp.asarray(inv, F32).astype(BF16), jnp.asarray(tw_r, F32),
                 jnp.asarray(tw_i, F32), r)

    x1, h2, comb = _outproj(attn, hyz, x, mod, p['w_out'], p['hy_gain'], p['bd_hy'], p['ln1_g'], p['ln1_b'],
                            p['rw_hi'], p['rw_lo'], p['rb'], latent, tm)
    tokens = nb * seq
    tmo = _pick_tile(seq, 1024) if latent else _pick_tile(tokens, 1024)
    tiles_per_mod = seq // tmo if latent else tokens // tmo
    y = _moe(h2.reshape(tokens, D_MODEL), comb.reshape(tokens, LANES), p['wg'], p['wu'], p['wd'],
             x1.reshape(tokens, D_MODEL), mod, p['ln2_g'], p['ln2_b'], tiles_per_mod, tmo)
    return y.reshape(nb, seq, D_MODEL), knat, v


def _prepare(w_in, q_gain, k_gain, attn_out_gain, hy_short_w, hy_short_b, hy_f_w1, hy_f_b1, hy_f_w2, hy_f_b2,
             hy_f_w3, hy_freq, hy_decay, hy_skip, hy_out_gain, w_out, ln1_g, ln1_b, router_grp_w, router_grp_b,
             router_exp_w, router_exp_b, exp_w_gate, exp_w_up, exp_w_down, ln2_g, ln2_b, l):
    row = lambda a: a.reshape(1, -1)
    rw = jnp.concatenate([router_exp_w[l], router_grp_w[l]], axis=1)
    rw = jnp.pad(rw, ((0, 0), (0, LANES - rw.shape[1])))
    rb = jnp.concatenate([router_exp_b[l], router_grp_b[l]])
    rb = jnp.pad(rb, (0, LANES - rb.shape[0]))
    rw_hi, rw_lo = _split_bf16(rw)
    return {
        'w_in': w_in[l].astype(BF16),
        'qk_gain': row(jnp.concatenate([jnp.tile(q_gain[l], N_HEADS), jnp.tile(k_gain[l], N_KV_HEADS)])),
        'bd_qk': _block_diag_mean(QK_DIM, HEAD_DIM),
        'attn_gain': row(attn_out_gain[l]),
        'hy_short_w': hy_short_w[l], 'hy_short_b': hy_short_b[l],
        'hy_f_w1': hy_f_w1[l], 'hy_f_b1': hy_f_b1[l], 'hy_f_w2': hy_f_w2[l], 'hy_f_b2': hy_f_b2[l],
        'hy_f_w3': hy_f_w3[l], 'hy_freq': hy_freq[l], 'hy_decay': hy_decay[l], 'hy_skip': hy_skip[l],
        'hy_gain': row(hy_out_gain[l]),
        'bd_hy': _block_diag_mean(HYENA_WIDTH, HY_GROUP_DIM),
        'w_out': w_out[l].astype(BF16),
        'ln1_g': row(ln1_g[l]), 'ln1_b': row(ln1_b[l]),
        'rw_hi': rw_hi, 'rw_lo': rw_lo, 'rb': row(rb),
        'wg': exp_w_gate[l].astype(BF16), 'wu': exp_w_up[l].astype(BF16), 'wd': exp_w_down[l].astype(BF16),
        'ln2_g': row(ln2_g[l]), 'ln2_b': row(ln2_b[l]),
    }


def kernel(x_prompt, x_sample, c, cache_k, cache_v, c_ctx, w_mod, b_mod, w_in, q_gain, k_gain, attn_out_gain, hy_short_w, hy_short_b, hy_f_w1, hy_f_b1, hy_f_w2, hy_f_b2, hy_f_w3, hy_freq, hy_decay, hy_skip, hy_out_gain, w_out, ln1_g, ln1_b, router_grp_w, router_grp_b, router_exp_w, router_exp_b, exp_w_gate, exp_w_up, exp_w_down, ln2_g, ln2_b):
    depth = w_mod.shape[0]
    n_lat = c.shape[0]
    cond = jnp.concatenate([c, c_ctx[None, :]], axis=0)
    rows = -(-cond.shape[0] // SUBLANES) * SUBLANES
    cond = jnp.pad(cond, ((0, rows - cond.shape[0]), (0, 0)))
    y_prompt, y_sample = x_prompt, x_sample
    ks_new, vs_new = [], []
    for l in range(depth):
        p = _prepare(w_in, q_gain, k_gain, attn_out_gain, hy_short_w, hy_short_b, hy_f_w1, hy_f_b1, hy_f_w2,
                     hy_f_b2, hy_f_w3, hy_freq, hy_decay, hy_skip, hy_out_gain, w_out, ln1_g, ln1_b,
                     router_grp_w, router_grp_b, router_exp_w, router_exp_b, exp_w_gate, exp_w_up, exp_w_down,
                     ln2_g, ln2_b, l)
        mod = _adaln(cond, w_mod[l], b_mod[l])
        mod_lat = mod[:n_lat].reshape(n_lat, 6, D_MODEL)
        mod_ctx = mod[n_lat:n_lat + 1].reshape(1, 6, D_MODEL)
        y_prompt, knat, v = _trunk_layer(y_prompt, mod_ctx, p, False)
        ks_new.append(knat[0].reshape(knat[0].shape[:2] + (N_KV_HEADS, HEAD_DIM)))
        vs_new.append(v.reshape(v.shape[:2] + (N_KV_HEADS, HEAD_DIM)))
        y_sample, _, _ = _trunk_layer(y_sample, mod_lat, p, True, cache_k[:, l], cache_v[:, l])
    return (y_prompt, y_sample, jnp.stack(ks_new, axis=1), jnp.stack(vs_new, axis=1))
```

```python
import functools

import numpy as np
import jax
import jax.numpy as jnp
from jax import lax
from jax.experimental import pallas as pl
from jax.experimental.pallas import tpu as pltpu

F32 = jnp.float32
BF16 = jnp.bfloat16

D_MODEL = 1024
GRID_W = 64
HEAD_DIM = 64
N_HEADS = 8
N_KV_HEADS = 2
GQA_GROUP = N_HEADS // N_KV_HEADS
Q_DIM = N_HEADS * HEAD_DIM
KV_DIM = N_KV_HEADS * HEAD_DIM
QK_DIM = Q_DIM + KV_DIM
HYENA_WIDTH = 512
HY_ORDER = 2
HY_IN = (HY_ORDER + 1) * HYENA_WIDTH
HY_GROUP_DIM = 64
HY_BANDS = 16
HY_POS_DIM = 1 + 2 * HY_BANDS
HY_FILTER_HIDDEN = 64
HY_OC = HY_ORDER * HYENA_WIDTH
IN_WIDTH = Q_DIM + 2 * KV_DIM + HY_IN
ROPE_THETA = 10000.0
ROPE_FREQS = HEAD_DIM // 4
N_GROUPS = 4
EXPERTS_PER_GROUP = 8
N_EXPERTS = N_GROUPS * EXPERTS_PER_GROUP
EXPERT_FF = D_MODEL // 4
DEPTH = 1
DEEPNORM_ALPHA = (2.0 * DEPTH) ** 0.25
EPS = 1e-6

LANES = 128
SUBLANES = 8
VMEM_LIMIT = 56 * 1024 * 1024
MOE_TILE = 896
MOE_CHUNK = 256
GROUP_ID_LANE = N_EXPERTS
ATTN_SUBTILES = 2
NEG_BIG = -1e30


def _cparams(sem):
    return pltpu.CompilerParams(dimension_semantics=sem, vmem_limit_bytes=VMEM_LIMIT)


def _split_bf16(a):
    hi = a.astype(BF16)
    lo = (a - hi.astype(F32)).astype(BF16)
    return hi, lo


def _dot(a, b):
    return jnp.dot(a, b, preferred_element_type=F32)


def _dot3(a, b):
    ah, al = _split_bf16(a)
    bh, bl = _split_bf16(b)
    return _dot(ah, bh) + _dot(al, bh) + _dot(ah, bl)


def _dot3_pre(ah, al, b):
    bh, bl = _split_bf16(b)
    return _dot(ah, bh) + _dot(al, bh) + _dot(ah, bl)


def _silu(x):
    return x / (1.0 + jnp.exp(-x))


def _layer_norm(y, g, b):
    mu = jnp.mean(y, axis=-1, keepdims=True)
    yc = y - mu
    var = jnp.mean(yc * yc, axis=-1, keepdims=True)
    return yc * lax.rsqrt(var + EPS) * g + b


def _adaln_kernel(c_ref, w_ref, b_ref, o_ref):
    o_ref[...] = _dot3(_silu(c_ref[...]), w_ref[...]) + b_ref[...]


def _adaln(cond, w_mod, b_mod):
    rows = cond.shape[0]
    n = w_mod.shape[1]
    tn = 1536
    return pl.pallas_call(
        _adaln_kernel,
        out_shape=jax.ShapeDtypeStruct((rows, n), F32),
        grid=(n // tn,),
        in_specs=[
            pl.BlockSpec((rows, D_MODEL), lambda j: (0, 0)),
            pl.BlockSpec((D_MODEL, tn), lambda j: (0, j)),
            pl.BlockSpec((1, tn), lambda j: (0, j)),
        ],
        out_specs=pl.BlockSpec((rows, tn), lambda j: (0, j)),
        compiler_params=_cparams(("arbitrary",)),
    )(cond, w_mod, b_mod.reshape(1, n))


def _rope_tables(seq):
    t = np.arange(seq)
    rows = (t // GRID_W).astype(np.float64)
    cols = (t % GRID_W).astype(np.float64)
    inv_freq = ROPE_THETA ** (-np.arange(ROPE_FREQS, dtype=np.float64) / ROPE_FREQS)
    d = np.arange(LANES) % HEAD_DIM
    axis = d // (2 * ROPE_FREQS)
    f = d % ROPE_FREQS
    pos = np.where(axis[None, :] == 0, rows[:, None], cols[:, None])
    ang = pos * inv_freq[f][None, :]
    first = (d % (2 * ROPE_FREQS)) < ROPE_FREQS
    cos = np.cos(ang)
    sin = np.where(first[None, :], -np.sin(ang), np.sin(ang))
    return jnp.asarray(cos, F32), jnp.asarray(sin, F32)


def _inproj_kernel(latent, x_ref, mod_ref, w_ref, gain_ref, bd_ref, *rest):
    if latent:
        cos_ref, sin_ref, q_ref, kt_ref, v_ref, hy_ref = rest
    else:
        q_ref, kt_ref, v_ref, hy_ref, knat_ref = rest
    m = mod_ref[...]
    h = x_ref[...] * (1.0 + m[1:2]) + m[0:1]
    proj = _dot(h.astype(BF16), w_ref[...])
    qk = proj[:, :QK_DIM]
    ms = _dot((qk * qk).astype(BF16), bd_ref[...])
    qk = qk * lax.rsqrt(ms + EPS) * gain_ref[...]
    if not latent:
        knat_ref[...] = qk[:, Q_DIM:]
    else:
        cos = cos_ref[...]
        sin = sin_ref[...]
        lane = lax.broadcasted_iota(jnp.int32, cos.shape, 1)
        first = (lane % (2 * ROPE_FREQS)) < ROPE_FREQS
        chunks = []
        for c in range(QK_DIM // LANES):
            xc = qk[:, c * LANES:(c + 1) * LANES]
            below = pltpu.roll(xc, ROPE_FREQS, axis=1)
            above = pltpu.roll(xc, LANES - ROPE_FREQS, axis=1)
            chunks.append(xc * cos + jnp.where(first, above, below) * sin)
        qk = jnp.concatenate(chunks, axis=1)
    qs = (qk[:, :Q_DIM] * (HEAD_DIM ** -0.5)).astype(BF16)
    for hd in range(N_HEADS):
        q_ref[hd] = qs[:, hd * HEAD_DIM:(hd + 1) * HEAD_DIM]
    kt = qk[:, Q_DIM:].T
    for kh in range(N_KV_HEADS):
        kt_ref[kh] = kt[kh * HEAD_DIM:(kh + 1) * HEAD_DIM].astype(BF16)
    v_ref[...] = proj[:, QK_DIM:QK_DIM + KV_DIM]
    hy_ref[...] = proj[:, QK_DIM + KV_DIM:]


def _inproj(x, mod, w_in, qk_gain, bd_qk, latent, tm):
    nb, seq, _ = x.shape
    grid = (nb, seq // tm)
    mod_map = (lambda b, i: (b, 0, 0)) if latent else (lambda b, i: (0, 0, 0))
    in_specs = [
        pl.BlockSpec((None, tm, D_MODEL), lambda b, i: (b, i, 0)),
        pl.BlockSpec((None, 6, D_MODEL), mod_map),
        pl.BlockSpec((D_MODEL, IN_WIDTH), lambda b, i: (0, 0)),
        pl.BlockSpec((1, QK_DIM), lambda b, i: (0, 0)),
        pl.BlockSpec((QK_DIM, QK_DIM), lambda b, i: (0, 0)),
    ]
    args = [x, mod, w_in, qk_gain, bd_qk]
    out_shape = [
        jax.ShapeDtypeStruct((nb, N_HEADS, seq, HEAD_DIM), BF16),
        jax.ShapeDtypeStruct((nb, N_KV_HEADS, HEAD_DIM, seq), BF16),
        jax.ShapeDtypeStruct((nb, seq, KV_DIM), F32),
        jax.ShapeDtypeStruct((nb, seq, HY_IN), F32),
    ]
    out_specs = [
        pl.BlockSpec((None, N_HEADS, tm, HEAD_DIM), lambda b, i: (b, 0, i, 0)),
        pl.BlockSpec((None, N_KV_HEADS, HEAD_DIM, tm), lambda b, i: (b, 0, 0, i)),
        pl.BlockSpec((None, tm, KV_DIM), lambda b, i: (b, i, 0)),
        pl.BlockSpec((None, tm, HY_IN), lambda b, i: (b, i, 0)),
    ]
    if latent:
        cos, sin = _rope_tables(seq)
        in_specs += [pl.BlockSpec((tm, LANES), lambda b, i: (i, 0))] * 2
        args += [cos, sin]
    else:
        out_shape.append(jax.ShapeDtypeStruct((nb, seq, KV_DIM), F32))
        out_specs.append(pl.BlockSpec((None, tm, KV_DIM), lambda b, i: (b, i, 0)))
    return pl.pallas_call(
        functools.partial(_inproj_kernel, latent),
        out_shape=out_shape,
        grid=grid,
        in_specs=in_specs,
        out_specs=out_specs,
        compiler_params=_cparams(("parallel", "parallel")),
    )(*args)


def _attn_kernel(n_ctx, chunk, q_ref, kt_ref, v_ref, gain_ref, *rest):
    if n_ctx:
        ckt_ref, cv_ref, o_ref = rest
    else:
        (o_ref,) = rest
    kh = pl.program_id(1)
    g, tq, _ = q_ref.shape
    seq = kt_ref.shape[1]
    pieces = [(kt_ref[:, c * chunk:(c + 1) * chunk], v_ref[c * chunk:(c + 1) * chunk, :])
              for c in range(seq // chunk)]
    if n_ctx:
        pieces.append((ckt_ref[...], cv_ref[...]))
    ts = tq // ATTN_SUBTILES
    for t in range(ATTN_SUBTILES):
        rows = slice(t * ts, (t + 1) * ts)
        qs = q_ref[:, rows, :].reshape(g * ts, HEAD_DIM)
        m = den = acc = None
        for kt_c, v_c in pieces:
            s = _dot(qs, kt_c)
            row_max = jnp.max(s, axis=1, keepdims=True)
            m_new = row_max if m is None else jnp.maximum(m, row_max)
            p = jnp.exp(s - m_new)
            p_sum = jnp.sum(p, axis=1, keepdims=True)
            pv = _dot(p.astype(BF16), v_c.astype(BF16))
            if m is None:
                den, acc = p_sum, pv
            else:
                alpha = jnp.exp(m - m_new)
                den = alpha * den + p_sum
                acc = alpha * acc + pv
            m = m_new
        o = jnp.where(kh == 0, acc[:, :HEAD_DIM], acc[:, HEAD_DIM:]) / den
        o = o * lax.rsqrt(jnp.mean(o * o, axis=1, keepdims=True) + EPS)
        for i in range(g):
            cols = slice(i * HEAD_DIM, (i + 1) * HEAD_DIM)
            o_ref[rows, cols] = (o[i * ts:(i + 1) * ts] * gain_ref[:, cols]).astype(o_ref.dtype)


def _attention(q, kt, v, gain, ckt, cv, tq, chunk):
    nb, _, seq, _ = q.shape
    n_ctx = 0 if ckt is None else ckt.shape[-1]
    width = GQA_GROUP * HEAD_DIM
    in_specs = [
        pl.BlockSpec((None, GQA_GROUP, tq, HEAD_DIM), lambda b, k, i: (b, k, i, 0)),
        pl.BlockSpec((None, None, HEAD_DIM, seq), lambda b, k, i: (b, k, 0, 0)),
        pl.BlockSpec((None, seq, KV_DIM), lambda b, k, i: (b, 0, 0)),
        pl.BlockSpec((1, width), lambda b, k, i: (0, k)),
    ]
    args = [q, kt, v, gain]
    if n_ctx:
        in_specs += [
            pl.BlockSpec((None, None, HEAD_DIM, n_ctx), lambda b, k, i: (b, k, 0, 0)),
            pl.BlockSpec((None, n_ctx, KV_DIM), lambda b, k, i: (b, 0, 0)),
        ]
        args += [ckt, cv]
    return pl.pallas_call(
        functools.partial(_attn_kernel, n_ctx, chunk),
        out_shape=jax.ShapeDtypeStruct((nb, seq, Q_DIM), BF16),
        grid=(nb, N_KV_HEADS, seq // tq),
        in_specs=in_specs,
        out_specs=pl.BlockSpec((None, tq, width), lambda b, k, i: (b, i, k)),
        compiler_params=_cparams(("parallel", "parallel", "parallel")),
    )(*args)


def _hyena_decimation(seq):
    return 8 if seq >= 2048 else 1


def _dft_tables(seq, r):
    n_sub = 2 * seq // r
    half = n_sub // 2
    k = np.arange(half, dtype=np.float64)[:, None]
    m = np.arange(n_sub, dtype=np.float64)[None, :]
    ang = 2.0 * np.pi * k * m / n_sub
    fwd = np.concatenate([np.cos(ang), -np.sin(ang)], axis=0)
    fwd[half] = np.cos(np.pi * m[0])
    inv = fwd.T.copy() * (2.0 / n_sub)
    inv[:, 0] *= 0.5
    inv[:, half] *= 0.5
    inv = inv[:seq // r]
    kk = np.arange(half, dtype=np.float64)[:, None] * np.ones((1, LANES))
    tw_r = np.cos(2.0 * np.pi * kk / n_sub)
    tw_i = -np.sin(2.0 * np.pi * kk / n_sub)
    return fwd, inv, tw_r, tw_i


def _filter_positions(seq, r):
    n_tot = 2 * seq
    n = (np.arange(n_tot // r)[None, :] * r + np.arange(r)[:, None]).reshape(-1)
    j = np.where(n < seq, n, n_tot - n)
    t = j.astype(np.float64) / seq
    bands = np.arange(1, HY_BANDS + 1, dtype=np.float64)
    ang = 2.0 * np.pi * t[:, None] * bands
    z = np.concatenate([t[:, None], np.sin(ang), np.cos(ang)], axis=-1)
    ones = np.ones((1, HY_FILTER_HIDDEN))
    sel_f = (n < seq).astype(np.float64)[:, None] * ones
    sel_b = (n > seq).astype(np.float64)[:, None] * ones
    return z, t[:, None] * np.ones((1, LANES)), sel_f, sel_b


def _filter_ffn_kernel(z_ref, self_ref, selb_ref, w1_ref, b1_ref, w2_ref, b2_ref, fr_ref, hf_ref, hb_ref):
    fr = fr_ref[...]
    h = jnp.sin(fr * (_dot3(z_ref[...], w1_ref[...]) + b1_ref[...]))
    h = jnp.sin(fr * (_dot3(h, w2_ref[...]) + b2_ref[...]))
    hf_ref[...] = h * self_ref[...]
    hb_ref[...] = h * selb_ref[...]


def _filter_spec_kernel(r, hf_ref, hb_ref, t_ref, w3f_ref, w3b_ref, dcf_ref, dcb_ref, fh_ref, fl_ref,
                        gre_ref, gim_ref):
    t = t_ref[...]
    g = (_dot3(hf_ref[...], w3f_ref[...]) * jnp.exp(-t * jnp.abs(dcf_ref[...]))
         + _dot3(hb_ref[...], w3b_ref[...]) * jnp.exp(-t * jnp.abs(dcb_ref[...])))
    g = g * lax.rsqrt(jnp.sum(g * g, axis=0, keepdims=True) + EPS)
    n_sub = g.shape[0] // r
    half = n_sub // 2
    fh = fh_ref[...]
    fl = fl_ref[...]
    for p in range(r):
        spec = _dot3_pre(fh, fl, g[p * n_sub:(p + 1) * n_sub])
        gre_ref[p] = spec[:half]
        gim_ref[p] = spec[half:]


def _hyena_filters(seq, r, w1, b1, w2, b2, w3, freq, decay, fwd_hi, fwd_lo):
    n_tot = 2 * seq
    n_sub = n_tot // r
    half = n_sub // 2
    z, t, sel_f, sel_b = _filter_positions(seq, r)
    pad = (-HY_POS_DIM) % SUBLANES
    z = jnp.asarray(np.pad(z, ((0, 0), (0, pad))), F32)
    w1p = jnp.pad(w1, ((0, pad), (0, 0)))
    kin = HY_POS_DIM + pad
    hid = HY_FILTER_HIDDEN
    tr = min(n_tot, 512)
    rows = lambda width: pl.BlockSpec((tr, width), lambda i: (i, 0))
    full = lambda shape: pl.BlockSpec(shape, lambda j: (0,) * len(shape))
    hf, hb = pl.pallas_call(
        _filter_ffn_kernel,
        out_shape=[jax.ShapeDtypeStruct((n_tot, hid), F32)] * 2,
        grid=(n_tot // tr,),
        in_specs=[rows(kin), rows(hid), rows(hid), full((kin, hid)), full((1, hid)), full((hid, hid)),
                  full((1, hid)), full((1, hid))],
        out_specs=[rows(hid)] * 2,
        compiler_params=_cparams(("parallel",)),
    )(z, jnp.asarray(sel_f, F32), jnp.asarray(sel_b, F32), w1p, b1.reshape(1, hid), w2, b2.reshape(1, hid),
      freq.reshape(1, hid))
    ncb = HY_OC // LANES
    return pl.pallas_call(
        functools.partial(_filter_spec_kernel, r),
        out_shape=[jax.ShapeDtypeStruct((r, half, HY_OC), F32)] * 2,
        grid=(ncb,),
        in_specs=[
            full((n_tot, hid)), full((n_tot, hid)), full((n_tot, LANES)),
            pl.BlockSpec((hid, LANES), lambda j: (0, j)),
            pl.BlockSpec((hid, LANES), lambda j: (0, j + ncb)),
            pl.BlockSpec((1, LANES), lambda j: (0, j)),
            pl.BlockSpec((1, LANES), lambda j: (0, j + ncb)),
            full((2 * half, n_sub)), full((2 * half, n_sub)),
        ],
        out_specs=[pl.BlockSpec((r, half, LANES), lambda j: (0, 0, j))] * 2,
        compiler_params=_cparams(("parallel",)),
    )(hf, hb, jnp.asarray(t, F32), w3, w3, decay.reshape(1, -1), decay.reshape(1, -1), fwd_hi, fwd_lo)


def _hyena_kernel(r, hy0_ref, hy1_ref, hy2_ref, sw_ref, sb_ref, skip_ref,
                  gre0_ref, gim0_ref, gre1_ref, gim1_ref, fwd_ref, inv_ref, twr_ref, twi_ref,
                  o_ref, z_ref, rhs_ref, x_ref):
    seq = hy0_ref.shape[0]
    m_len = seq // r
    half = fwd_ref.shape[0] // 2
    row = lax.broadcasted_iota(jnp.int32, (m_len, LANES), 0)
    hy_refs = (hy0_ref, hy1_ref, hy2_ref)

    def phase(ref, j):
        if r == 1:
            return ref[...]
        return ref[pl.ds(j, m_len, stride=r), :]

    def short_conv(part, j):
        ref = hy_refs[part]
        w = sw_ref[:, part * LANES:(part + 1) * LANES]
        b = sb_ref[:, part * LANES:(part + 1) * LANES]
        if j > 0:
            prev = phase(ref, j - 1)
        else:
            prev = jnp.where(row == 0, 0.0, pltpu.roll(phase(ref, r - 1), 1, axis=0))
        if j < r - 1:
            nxt = phase(ref, j + 1)
        else:
            nxt = jnp.where(row == m_len - 1, 0.0, pltpu.roll(phase(ref, 0), m_len - 1, axis=0))
        return prev * w[0:1] + phase(ref, j) * w[1:2] + nxt * w[2:3] + b

    for j in range(r):
        z_ref[j] = short_conv(0, j)

    for o, (gre_ref, gim_ref) in enumerate(((gre0_ref, gim0_ref), (gre1_ref, gim1_ref))):
        for j in range(r):
            rhs_ref[:, j * LANES:(j + 1) * LANES] = z_ref[j].astype(BF16)
        x_ref[...] = _dot(fwd_ref[...], rhs_ref[...])
        dc = [x_ref[0:1, j * LANES:(j + 1) * LANES] for j in range(r)]
        ny = [x_ref[half:half + 1, j * LANES:(j + 1) * LANES] for j in range(r)]
        y_dc, y_ny = [], []
        for j in range(r):
            a = jnp.zeros((1, LANES), F32)
            c = jnp.zeros((1, LANES), F32)
            for jp in range(r):
                p = (j - jp) % r
                a = a + gre_ref[p, 0:1, :] * dc[jp]
                t = gim_ref[p, 0:1, :] * ny[jp]
                c = c + t if jp <= j else c - t
            y_dc.append(a)
            y_ny.append(c)

        def mix(i, carry):
            r0 = pl.multiple_of(i * SUBLANES, SUBLANES)
            rows_re = pl.ds(r0, SUBLANES)
            rows_im = pl.ds(half + r0, SUBLANES)
            xr = [x_ref[rows_re, j * LANES:(j + 1) * LANES] for j in range(r)]
            xi = [x_ref[rows_im, j * LANES:(j + 1) * LANES] for j in range(r)]
            wr = twr_ref[rows_re, :]
            wi = twi_ref[rows_re, :]
            for j in range(r):
                pr = pi = qr = qi = None
                for jp in range(r):
                    p = (j - jp) % r
                    gr = gre_ref[p, rows_re, :]
                    gi = gim_ref[p, rows_re, :]
                    tr = gr * xr[jp] - gi * xi[jp]
                    ti = gr * xi[jp] + gi * xr[jp]
                    if jp <= j:
                        pr = tr if pr is None else pr + tr
                        pi = ti if pi is None else pi + ti
                    else:
                        qr = tr if qr is None else qr + tr
                        qi = ti if qi is None else qi + ti
                if qr is not None:
                    pr = pr + wr * qr - wi * qi
                    pi = pi + wr * qi + wi * qr
                x_ref[rows_re, j * LANES:(j + 1) * LANES] = pr
                x_ref[rows_im, j * LANES:(j + 1) * LANES] = pi
            return carry

        lax.fori_loop(0, half // SUBLANES, mix, 0)
        for j in range(r):
            x_ref[0:1, j * LANES:(j + 1) * LANES] = y_dc[j]
            x_ref[half:half + 1, j * LANES:(j + 1) * LANES] = y_ny[j]
        y = _dot(inv_ref[...], x_ref[...].astype(BF16))
        sk = skip_ref[o:o + 1, :]
        for j in range(r):
            z_ref[j] = short_conv(o + 1, j) * (y[:, j * LANES:(j + 1) * LANES] + z_ref[j] * sk)
    for j in range(r):
        if r == 1:
            o_ref[...] = z_ref[j]
        else:
            o_ref[pl.ds(j, m_len, stride=r), :] = z_ref[j]


def _hyena_direct_kernel(hy_ref, sw_ref, sb_ref, skip_ref, gre_ref, gim_ref, fwd_ref, inv_ref, o_ref):
    bb, seq, _ = hy_ref.shape
    half = fwd_ref.shape[0] // 2
    w = HYENA_WIDTH
    row = lax.broadcasted_iota(jnp.int32, (seq, w), 0)
    is_dc = lax.broadcasted_iota(jnp.int32, (half, w), 0) == 0

    def short_conv(b, part):
        cols = slice(part * w, (part + 1) * w)
        x = hy_ref[b, :, cols]
        prev = jnp.where(row == 0, 0.0, pltpu.roll(x, 1, axis=0))
        nxt = jnp.where(row == seq - 1, 0.0, pltpu.roll(x, seq - 1, axis=0))
        return prev * sw_ref[0:1, cols] + x * sw_ref[1:2, cols] + nxt * sw_ref[2:3, cols] + sb_ref[:, cols]

    for b in range(bb):
        z = short_conv(b, 0)
        for o in range(HY_ORDER):
            cols = slice(o * w, (o + 1) * w)
            x = _dot(fwd_ref[...], z.astype(BF16))
            xr, xi = x[:half], x[half:]
            gr, gi = gre_ref[0, :, cols], gim_ref[0, :, cols]
            vr = gr * xr - jnp.where(is_dc, 0.0, gi * xi)
            vi = jnp.where(is_dc, gi * xi, gr * xi + gi * xr)
            y = _dot(inv_ref[...], jnp.concatenate([vr, vi], axis=0).astype(BF16))
            z = short_conv(b, o + 1) * (y + z * skip_ref[o:o + 1, :])
        o_ref[b] = z


def _hyena_direct(hy, short_w, short_b, skip, gre, gim, fwd, inv):
    nb, seq, _ = hy.shape
    bb = _pick_tile(nb, 4)
    const = lambda a: pl.BlockSpec(a.shape, lambda i: (0,) * a.ndim)
    short_b = short_b.reshape(1, -1)
    return pl.pallas_call(
        _hyena_direct_kernel,
        out_shape=jax.ShapeDtypeStruct((nb, seq, HYENA_WIDTH), F32),
        grid=(nb // bb,),
        in_specs=[pl.BlockSpec((bb, seq, HY_IN), lambda i: (i, 0, 0)), const(short_w), const(short_b), const(skip),
                  const(gre), const(gim), const(fwd), const(inv)],
        out_specs=pl.BlockSpec((bb, seq, HYENA_WIDTH), lambda i: (i, 0, 0)),
        compiler_params=_cparams(("parallel",)),
    )(hy, short_w, short_b, skip, gre, gim, fwd, inv)


def _hyena(hy, short_w, short_b, skip, gre, gim, fwd, inv, tw_r, tw_i, r):
    if r == 1:
        return _hyena_direct(hy, short_w, short_b, skip, gre, gim, fwd, inv)
    nb, seq, _ = hy.shape
    m_len = seq // r
    n_half2 = fwd.shape[0]
    half = n_half2 // 2
    ncb = HYENA_WIDTH // LANES
    parts = HY_ORDER + 1
    once = pl.Buffered(1)
    hy_spec = lambda part: pl.BlockSpec((None, seq, LANES), lambda c, b: (b, 0, part * ncb + c))
    g_spec = lambda o: pl.BlockSpec((r, half, LANES), lambda c, b: (0, 0, o * ncb + c), pipeline_mode=once)
    const = lambda shape: pl.BlockSpec(shape, lambda c, b: (0,) * len(shape), pipeline_mode=once)
    sw = short_w.reshape(3, parts, ncb, LANES).transpose(2, 0, 1, 3).reshape(ncb, 3, parts * LANES)
    sb = short_b.reshape(1, parts, ncb, LANES).transpose(2, 0, 1, 3).reshape(ncb, 1, parts * LANES)
    return pl.pallas_call(
        functools.partial(_hyena_kernel, r),
        out_shape=jax.ShapeDtypeStruct((nb, seq, HYENA_WIDTH), F32),
        grid=(ncb, nb),
        in_specs=[
            hy_spec(0), hy_spec(1), hy_spec(2),
            pl.BlockSpec((None, 3, parts * LANES), lambda c, b: (c, 0, 0)),
            pl.BlockSpec((None, 1, parts * LANES), lambda c, b: (c, 0, 0)),
            pl.BlockSpec((HY_ORDER, LANES), lambda c, b: (0, c)),
            g_spec(0), g_spec(0), g_spec(1), g_spec(1),
            const((n_half2, m_len)), const((m_len, n_half2)),
            const((half, LANES)), const((half, LANES)),
        ],
        out_specs=pl.BlockSpec((None, seq, LANES), lambda c, b: (b, 0, c)),
        scratch_shapes=[
            pltpu.VMEM((r, m_len, LANES), F32),
            pltpu.VMEM((m_len, r * LANES), BF16),
            pltpu.VMEM((n_half2, r * LANES), F32),
        ],
        compiler_params=_cparams(("parallel", "parallel")),
    )(hy, hy, hy, sw, sb, skip, gre, gim, gre, gim, fwd, inv, tw_r, tw_i)


def _route(logits):
    lane = lax.broadcasted_iota(jnp.int32, logits.shape, 1).astype(F32)
    big = jnp.float32(1e9)
    is_grp = (lane >= N_EXPERTS) & (lane < N_EXPERTS + N_GROUPS)
    gl = jnp.where(is_grp, logits, NEG_BIG)
    gmax = jnp.max(gl, axis=1, keepdims=True)
    gidx = jnp.min(jnp.where(gl == gmax, lane, big), axis=1, keepdims=True) - N_EXPERTS
    den = jnp.sum(jnp.where(is_grp, jnp.exp(gl - gmax), 0.0), axis=1, keepdims=True)
    pg_top = 1.0 / den
    lo = gidx * EXPERTS_PER_GROUP
    sel = jnp.where((lane >= lo) & (lane < lo + EXPERTS_PER_GROUP), logits, NEG_BIG)
    m1 = jnp.max(sel, axis=1, keepdims=True)
    i1 = jnp.min(jnp.where(sel == m1, lane, big), axis=1, keepdims=True)
    sel2 = jnp.where(lane == i1, NEG_BIG, sel)
    m2 = jnp.max(sel2, axis=1, keepdims=True)
    i2 = jnp.min(jnp.where(sel2 == m2, lane, big), axis=1, keepdims=True)
    e2 = jnp.exp(m2 - m1)
    w1 = pg_top / (1.0 + e2)
    w2 = pg_top * e2 / (1.0 + e2)
    comb = jnp.where(lane == i1, w1, 0.0) + jnp.where(lane == i2, w2, 0.0)
    return comb + jnp.where(lane == GROUP_ID_LANE, gidx, 0.0)


def _outproj_kernel(attn_ref, hyz_ref, x_ref, mod_ref, wo_ref, hg_ref, bd_ref, lng_ref, lnb_ref,
                    rwh_ref, rwl_ref, rb_ref, x1_ref, h2t_ref, comb_ref):
    m = mod_ref[...]
    z = hyz_ref[...]
    ms = _dot((z * z).astype(BF16), bd_ref[...])
    zn = (z * lax.rsqrt(ms + EPS) * hg_ref[...]).astype(BF16)
    half = wo_ref.shape[0] // 2
    mix = _dot(attn_ref[...], wo_ref[:half, :]) + _dot(zn, wo_ref[half:, :])
    x1 = _layer_norm(DEEPNORM_ALPHA * x_ref[...] + m[2:3] * mix, lng_ref[...], lnb_ref[...])
    h2 = x1 * (1.0 + m[4:5]) + m[3:4]
    x1_ref[...] = x1
    h2t_ref[...] = h2.T.astype(BF16)
    hh, hl = _split_bf16(h2)
    logits = _dot(hh, rwh_ref[...]) + _dot(hl, rwh_ref[...]) + _dot(hh, rwl_ref[...]) + rb_ref[...]
    comb_ref[...] = _route(logits)


def _outproj(attn, hyz, x, mod, w_out, hy_gain, bd_hy, ln_g, ln_b, rw_hi, rw_lo, rb, latent, tm):
    nb, seq, _ = x.shape
    mod_map = (lambda b, i: (b, 0, 0)) if latent else (lambda b, i: (0, 0, 0))
    const = lambda shape: pl.BlockSpec(shape, lambda b, i: (0,) * len(shape))
    tok = lambda width: pl.BlockSpec((None, tm, width), lambda b, i: (b, i, 0))
    return pl.pallas_call(
        _outproj_kernel,
        out_shape=[
            jax.ShapeDtypeStruct((nb, seq, D_MODEL), F32),
            jax.ShapeDtypeStruct((D_MODEL, nb * seq), BF16),
            jax.ShapeDtypeStruct((nb, seq, LANES), F32),
        ],
        grid=(nb, seq // tm),
        in_specs=[
            tok(Q_DIM), tok(HYENA_WIDTH), tok(D_MODEL),
            pl.BlockSpec((None, 6, D_MODEL), mod_map),
            const((D_MODEL, D_MODEL)), const((1, HYENA_WIDTH)), const((HYENA_WIDTH, HYENA_WIDTH)),
            const((1, D_MODEL)), const((1, D_MODEL)),
            const((D_MODEL, LANES)), const((D_MODEL, LANES)), const((1, LANES)),
        ],
        out_specs=[tok(D_MODEL), pl.BlockSpec((D_MODEL, tm), lambda b, i: (0, b * (seq // tm) + i)),
                   tok(LANES)],
        compiler_params=_cparams(("parallel", "parallel")),
    )(attn, hyz, x, mod, w_out, hy_gain, bd_hy, ln_g, ln_b, rw_hi, rw_lo, rb)


def _moe_kernel(n_tok, tok_per_mod, ht_ref, meta_ref, tri_ref, wg_ref, wu_ref, wd_ref, x1_ref, mod0_ref, mod1_ref,
                lng_ref, lnb_ref, o_ref, yt_ref, ext_ref, mt_ref):
    i = pl.program_id(0)
    g = pl.program_id(1)
    tmo = meta_ref.shape[0]
    tok0 = i * tmo
    row_valid = tok0 + lax.broadcasted_iota(jnp.int32, (tmo, 1), 0) < n_tok

    @pl.when(g == 0)
    def _():
        col_valid = tok0 + lax.broadcasted_iota(jnp.int32, (1, tmo), 1) < n_tok
        yt_ref[...] = jnp.zeros_like(yt_ref)
        ext_ref[:D_MODEL, :] = jnp.where(col_valid, ht_ref[...], jnp.zeros((), BF16))
        mt_ref[...] = jnp.where(row_valid, meta_ref[...], 0.0).T

    gid = jnp.where(row_valid, meta_ref[:, GROUP_ID_LANE:GROUP_ID_LANE + 1], -1.0)
    member = gid == g.astype(F32)
    count = jnp.sum(jnp.where(member, 1.0, 0.0))
    ones = jnp.where(jnp.broadcast_to(member, (tmo, LANES)), 1.0, 0.0).astype(BF16)
    rank = _dot(tri_ref[...], ones)[:, 0:1]
    slot = jnp.where(member, rank, -1.0)
    cw_hi, cw_lo = _split_bf16(mt_ref[pl.ds(pl.multiple_of(g * EXPERTS_PER_GROUP, EXPERTS_PER_GROUP),
                                             EXPERTS_PER_GROUP), :])
    ext_ref[D_MODEL:, :] = jnp.concatenate([cw_hi, cw_lo], axis=0)
    lane = lax.broadcasted_iota(jnp.int32, (tmo, MOE_CHUNK), 1).astype(F32)

    def chunk(c, carry):
        sel = jnp.where(slot == lane + (c * MOE_CHUNK).astype(F32), 1.0, 0.0).astype(BF16)
        picked = _dot(ext_ref[...], sel)
        xs = picked[:D_MODEL].astype(BF16)
        cw = picked[D_MODEL:D_MODEL + EXPERTS_PER_GROUP] + picked[D_MODEL + EXPERTS_PER_GROUP:]
        act = _silu(_dot(wg_ref[...], xs)) * _dot(wu_ref[...], xs)
        act = jnp.concatenate(
            [act[e * EXPERT_FF:(e + 1) * EXPERT_FF] * cw[e:e + 1] for e in range(EXPERTS_PER_GROUP)], axis=0)
        ys = _dot(wd_ref[...], act.astype(BF16)).astype(BF16)
        yt_ref[...] += lax.dot_general(ys, sel, (((1,), (1,)), ((), ())), preferred_element_type=F32)
        return carry

    n_chunks = (count.astype(jnp.int32) + MOE_CHUNK - 1) // MOE_CHUNK
    lax.fori_loop(0, n_chunks, chunk, 0)

    @pl.when(g == pl.num_programs(1) - 1)
    def _():
        tok = tok0 + lax.broadcasted_iota(jnp.int32, (tmo, 1), 0)
        first = tok < (tok0 // tok_per_mod + 1) * tok_per_mod
        gate = jnp.where(first, mod0_ref[5:6, :], mod1_ref[5:6, :])
        y = DEEPNORM_ALPHA * x1_ref[...] + gate * yt_ref[...].T
        o_ref[...] = _layer_norm(y, lng_ref[...], lnb_ref[...])


def _moe(ht, meta, wgt, wut, wdt, x1, mod, ln_g, ln_b, tok_per_mod):
    tokens = meta.shape[0]
    n_mod = mod.shape[0]
    tmo = min(MOE_TILE, tokens)
    tri = jnp.asarray(np.tril(np.ones((tmo, tmo), np.float32), -1), BF16)
    ff = EXPERTS_PER_GROUP * EXPERT_FF
    const = lambda shape: pl.BlockSpec(shape, lambda i, g: (0,) * len(shape))
    tok = lambda width: pl.BlockSpec((tmo, width), lambda i, g: (i, 0))
    mod_spec = lambda off: pl.BlockSpec(
        (None, 6, D_MODEL), lambda i, g: (jnp.minimum((i * tmo) // tok_per_mod + off, n_mod - 1), 0, 0))
    return pl.pallas_call(
        functools.partial(_moe_kernel, tokens, tok_per_mod),
        out_shape=jax.ShapeDtypeStruct((tokens, D_MODEL), F32),
        grid=(pl.cdiv(tokens, tmo), N_GROUPS),
        in_specs=[
            pl.BlockSpec((D_MODEL, tmo), lambda i, g: (0, i)), tok(LANES), const((tmo, tmo)),
            pl.BlockSpec((None, ff, D_MODEL), lambda i, g: (g, 0, 0)),
            pl.BlockSpec((None, ff, D_MODEL), lambda i, g: (g, 0, 0)),
            pl.BlockSpec((None, D_MODEL, ff), lambda i, g: (g, 0, 0)),
            tok(D_MODEL), mod_spec(0), mod_spec(1),
            const((1, D_MODEL)), const((1, D_MODEL)),
        ],
        out_specs=tok(D_MODEL),
        scratch_shapes=[
            pltpu.VMEM((D_MODEL, tmo), F32),
            pltpu.VMEM((D_MODEL + 2 * EXPERTS_PER_GROUP, tmo), BF16),
            pltpu.VMEM((LANES, tmo), F32),
        ],
        compiler_params=_cparams(("parallel", "arbitrary")),
    )(ht, meta, tri, wgt, wut, wdt, x1, mod, mod, ln_g, ln_b)


def _block_diag_mean(width, group):
    idx = np.arange(width) // group
    return jnp.asarray((idx[:, None] == idx[None, :]).astype(np.float32) / group, BF16)


def _group_major_t(w):
    w = w.astype(BF16).reshape(N_GROUPS, EXPERTS_PER_GROUP, D_MODEL, EXPERT_FF)
    return w.transpose(0, 1, 3, 2).reshape(N_GROUPS, EXPERTS_PER_GROUP * EXPERT_FF, D_MODEL)


def _pick_tile(n, target):
    t = min(n, target)
    while n % t:
        t //= 2
    return t


def _trunk_layer(x, mod, p, latent, ctx_k=None, ctx_v=None):
    nb, seq, _ = x.shape
    tm = _pick_tile(seq, 512)
    q, kt, v, hy, *knat = _inproj(x, mod, p['w_in'], p['qk_gain'], p['bd_qk'], latent, tm)
    if latent:
        ckt = jnp.transpose(ctx_k, (0, 2, 3, 1)).astype(BF16)
        cv = ctx_v.reshape(nb, -1, KV_DIM)
    else:
        ckt = cv = None
    attn = _attention(q, kt, v, p['attn_gain'], ckt, cv, _pick_tile(seq, 256), _pick_tile(seq, 1024))

    r = _hyena_decimation(seq)
    fwd, inv, tw_r, tw_i = _dft_tables(seq, r)
    fwd_hi, fwd_lo = _split_bf16(jnp.asarray(fwd, F32))
    gre, gim = _hyena_filters(seq, r, p['hy_f_w1'], p['hy_f_b1'], p['hy_f_w2'], p['hy_f_b2'], p['hy_f_w3'],
                              p['hy_freq'], p['hy_decay'], fwd_hi, fwd_lo)
    hyz = _hyena(hy, p['hy_short_w'], p['hy_short_b'], p['hy_skip'], gre, gim,
                 fwd_hi[:, :seq // r], jnp.asarray(inv, F32).astype(BF16), jnp.asarray(tw_r, F32),
                 jnp.asarray(tw_i, F32), r)

    x1, h2t, comb = _outproj(attn, hyz, x, mod, p['w_out'], p['hy_gain'], p['bd_hy'], p['ln1_g'], p['ln1_b'],
                            p['rw_hi'], p['rw_lo'], p['rb'], latent, tm)
    tokens = nb * seq
    y = _moe(h2t, comb.reshape(tokens, LANES), p['wgt'], p['wut'], p['wdt'], x1.reshape(tokens, D_MODEL), mod,
             p['ln2_g'], p['ln2_b'], seq if latent else tokens)
    return y.reshape(nb, seq, D_MODEL), knat, v


def _prepare(w_in, q_gain, k_gain, attn_out_gain, hy_short_w, hy_short_b, hy_f_w1, hy_f_b1, hy_f_w2, hy_f_b2,
             hy_f_w3, hy_freq, hy_decay, hy_skip, hy_out_gain, w_out, ln1_g, ln1_b, router_grp_w, router_grp_b,
             router_exp_w, router_exp_b, exp_w_gate, exp_w_up, exp_w_down, ln2_g, ln2_b, l):
    row = lambda a: a.reshape(1, -1)
    rw = jnp.concatenate([router_exp_w[l], router_grp_w[l]], axis=1)
    rw = jnp.pad(rw, ((0, 0), (0, LANES - rw.shape[1])))
    rb = jnp.concatenate([router_exp_b[l], router_grp_b[l]])
    rb = jnp.pad(rb, (0, LANES - rb.shape[0]))
    rw_hi, rw_lo = _split_bf16(rw)
    return {
        'w_in': w_in[l].astype(BF16),
        'qk_gain': row(jnp.concatenate([jnp.tile(q_gain[l], N_HEADS), jnp.tile(k_gain[l], N_KV_HEADS)])),
        'bd_qk': _block_diag_mean(QK_DIM, HEAD_DIM),
        'attn_gain': row(attn_out_gain[l]),
        'hy_short_w': hy_short_w[l], 'hy_short_b': hy_short_b[l],
        'hy_f_w1': hy_f_w1[l], 'hy_f_b1': hy_f_b1[l], 'hy_f_w2': hy_f_w2[l], 'hy_f_b2': hy_f_b2[l],
        'hy_f_w3': hy_f_w3[l], 'hy_freq': hy_freq[l], 'hy_decay': hy_decay[l], 'hy_skip': hy_skip[l],
        'hy_gain': row(hy_out_gain[l]),
        'bd_hy': _block_diag_mean(HYENA_WIDTH, HY_GROUP_DIM),
        'w_out': w_out[l].astype(BF16),
        'ln1_g': row(ln1_g[l]), 'ln1_b': row(ln1_b[l]),
        'rw_hi': rw_hi, 'rw_lo': rw_lo, 'rb': row(rb),
        'wgt': _group_major_t(exp_w_gate[l]), 'wut': _group_major_t(exp_w_up[l]),
        'wdt': exp_w_down[l].astype(BF16).reshape(N_GROUPS, EXPERTS_PER_GROUP * EXPERT_FF, D_MODEL)
                            .transpose(0, 2, 1),
        'ln2_g': row(ln2_g[l]), 'ln2_b': row(ln2_b[l]),
    }


def kernel(x_prompt, x_sample, c, cache_k, cache_v, c_ctx, w_mod, b_mod, w_in, q_gain, k_gain, attn_out_gain, hy_short_w, hy_short_b, hy_f_w1, hy_f_b1, hy_f_w2, hy_f_b2, hy_f_w3, hy_freq, hy_decay, hy_skip, hy_out_gain, w_out, ln1_g, ln1_b, router_grp_w, router_grp_b, router_exp_w, router_exp_b, exp_w_gate, exp_w_up, exp_w_down, ln2_g, ln2_b):
    depth = w_mod.shape[0]
    n_lat = c.shape[0]
    cond = jnp.concatenate([c, c_ctx[None, :]], axis=0)
    rows = -(-cond.shape[0] // SUBLANES) * SUBLANES
    cond = jnp.pad(cond, ((0, rows - cond.shape[0]), (0, 0)))
    y_prompt, y_sample = x_prompt, x_sample
    ks_new, vs_new = [], []
    for l in range(depth):
        p = _prepare(w_in, q_gain, k_gain, attn_out_gain, hy_short_w, hy_short_b, hy_f_w1, hy_f_b1, hy_f_w2,
                     hy_f_b2, hy_f_w3, hy_freq, hy_decay, hy_skip, hy_out_gain, w_out, ln1_g, ln1_b,
                     router_grp_w, router_grp_b, router_exp_w, router_exp_b, exp_w_gate, exp_w_up, exp_w_down,
                     ln2_g, ln2_b, l)
        mod = _adaln(cond, w_mod[l], b_mod[l])
        mod_lat = mod[:n_lat].reshape(n_lat, 6, D_MODEL)
        mod_ctx = mod[n_lat:n_lat + 1].reshape(1, 6, D_MODEL)
        y_prompt, knat, v = _trunk_layer(y_prompt, mod_ctx, p, False)
        ks_new.append(knat[0].reshape(knat[0].shape[:2] + (N_KV_HEADS, HEAD_DIM)))
        vs_new.append(v.reshape(v.shape[:2] + (N_KV_HEADS, HEAD_DIM)))
        y_sample, _, _ = _trunk_layer(y_sample, mod_lat, p, True, cache_k[:, l], cache_v[:, l])
    return (y_prompt, y_sample, jnp.stack(ks_new, axis=1), jnp.stack(vs_new, axis=1))
```

```python
import functools

import numpy as np
import jax
import jax.numpy as jnp
from jax import lax
from jax.experimental import pallas as pl
from jax.experimental.pallas import tpu as pltpu

F32 = jnp.float32
BF16 = jnp.bfloat16

D_MODEL = 1024
GRID_W = 64
HEAD_DIM = 64
N_HEADS = 8
N_KV_HEADS = 2
GQA_GROUP = N_HEADS // N_KV_HEADS
Q_DIM = N_HEADS * HEAD_DIM
KV_DIM = N_KV_HEADS * HEAD_DIM
QK_DIM = Q_DIM + KV_DIM
HYENA_WIDTH = 512
HY_ORDER = 2
HY_IN = (HY_ORDER + 1) * HYENA_WIDTH
HY_GROUP_DIM = 64
HY_BANDS = 16
HY_POS_DIM = 1 + 2 * HY_BANDS
HY_FILTER_HIDDEN = 64
HY_OC = HY_ORDER * HYENA_WIDTH
IN_WIDTH = Q_DIM + 2 * KV_DIM + HY_IN
ROPE_THETA = 10000.0
ROPE_FREQS = HEAD_DIM // 4
N_GROUPS = 4
EXPERTS_PER_GROUP = 8
N_EXPERTS = N_GROUPS * EXPERTS_PER_GROUP
EXPERT_FF = D_MODEL // 4
DEPTH = 1
DEEPNORM_ALPHA = (2.0 * DEPTH) ** 0.25
EPS = 1e-6

LANES = 128
SUBLANES = 8
VMEM_LIMIT = 56 * 1024 * 1024
MOE_TILE = 896
MOE_CHUNK = 256
GROUP_ID_LANE = N_EXPERTS
ATTN_SUBTILES = 2
NEG_BIG = -1e30
LOG2_E = 1.4426950408889634


def _cparams(sem):
    return pltpu.CompilerParams(dimension_semantics=sem, vmem_limit_bytes=VMEM_LIMIT)


def _split_bf16(a):
    hi = a.astype(BF16)
    lo = (a - hi.astype(F32)).astype(BF16)
    return hi, lo


def _dot(a, b):
    return jnp.dot(a, b, preferred_element_type=F32)


def _dot3(a, b):
    ah, al = _split_bf16(a)
    bh, bl = _split_bf16(b)
    return _dot(ah, bh) + _dot(al, bh) + _dot(ah, bl)


def _silu(x):
    return x / (1.0 + jnp.exp(-x))


def _layer_norm(y, g, b):
    mu = jnp.mean(y, axis=-1, keepdims=True)
    yc = y - mu
    var = jnp.mean(yc * yc, axis=-1, keepdims=True)
    return yc * lax.rsqrt(var + EPS) * g + b


def _adaln_kernel(c_ref, w_ref, b_ref, o_ref):
    o_ref[...] = _dot3(_silu(c_ref[...]), w_ref[...]) + b_ref[...]


def _adaln(cond, w_mod, b_mod):
    rows = cond.shape[0]
    n = w_mod.shape[1]
    tn = 1536
    return pl.pallas_call(
        _adaln_kernel,
        out_shape=jax.ShapeDtypeStruct((rows, n), F32),
        grid=(n // tn,),
        in_specs=[
            pl.BlockSpec((rows, D_MODEL), lambda j: (0, 0)),
            pl.BlockSpec((D_MODEL, tn), lambda j: (0, j)),
            pl.BlockSpec((1, tn), lambda j: (0, j)),
        ],
        out_specs=pl.BlockSpec((rows, tn), lambda j: (0, j)),
        compiler_params=_cparams(("arbitrary",)),
    )(cond, w_mod, b_mod.reshape(1, n))


def _rope_tables(seq):
    t = np.arange(seq)
    rows = (t // GRID_W).astype(np.float64)
    cols = (t % GRID_W).astype(np.float64)
    inv_freq = ROPE_THETA ** (-np.arange(ROPE_FREQS, dtype=np.float64) / ROPE_FREQS)
    d = np.arange(LANES) % HEAD_DIM
    axis = d // (2 * ROPE_FREQS)
    f = d % ROPE_FREQS
    pos = np.where(axis[None, :] == 0, rows[:, None], cols[:, None])
    ang = pos * inv_freq[f][None, :]
    first = (d % (2 * ROPE_FREQS)) < ROPE_FREQS
    cos = np.cos(ang)
    sin = np.where(first[None, :], -np.sin(ang), np.sin(ang))
    return jnp.asarray(cos, F32), jnp.asarray(sin, F32)


def _inproj_kernel(latent, x_ref, mod_ref, w_ref, gain_ref, bd_ref, *rest):
    if latent:
        cos_ref, sin_ref, q_ref, kt_ref, vx_ref, hy_ref = rest
    else:
        q_ref, kt_ref, vx_ref, hy_ref, v_ref, knat_ref = rest
    m = mod_ref[...]
    h = x_ref[...] * (1.0 + m[1:2]) + m[0:1]
    proj = _dot(h.astype(BF16), w_ref[...])
    qk = proj[:, :QK_DIM]
    ms = _dot((qk * qk).astype(BF16), bd_ref[...])
    qk = qk * lax.rsqrt(ms + EPS) * gain_ref[...]
    if not latent:
        knat_ref[...] = qk[:, Q_DIM:]
    else:
        cos = cos_ref[...]
        sin = sin_ref[...]
        lane = lax.broadcasted_iota(jnp.int32, cos.shape, 1)
        first = (lane % (2 * ROPE_FREQS)) < ROPE_FREQS
        chunks = []
        for c in range(QK_DIM // LANES):
            xc = qk[:, c * LANES:(c + 1) * LANES]
            below = pltpu.roll(xc, ROPE_FREQS, axis=1)
            above = pltpu.roll(xc, LANES - ROPE_FREQS, axis=1)
            chunks.append(xc * cos + jnp.where(first, above, below) * sin)
        qk = jnp.concatenate(chunks, axis=1)
    qs = (qk[:, :Q_DIM] * (HEAD_DIM ** -0.5 * LOG2_E)).astype(BF16)
    for hd in range(N_HEADS):
        q_ref[hd] = qs[:, hd * HEAD_DIM:(hd + 1) * HEAD_DIM]
    kt = qk[:, Q_DIM:].T
    for kh in range(N_KV_HEADS):
        kt_ref[kh] = kt[kh * HEAD_DIM:(kh + 1) * HEAD_DIM].astype(BF16)
    v = proj[:, QK_DIM:QK_DIM + KV_DIM]
    low = lax.broadcasted_iota(jnp.int32, v.shape, 1) < HEAD_DIM
    vx_ref[0] = jnp.where(low, v, 1.0).astype(BF16)
    vx_ref[1] = jnp.where(low, pltpu.roll(v, HEAD_DIM, axis=1), 1.0).astype(BF16)
    if not latent:
        v_ref[...] = v
    hy_ref[...] = proj[:, QK_DIM + KV_DIM:]


def _inproj(x, mod, w_in, qk_gain, bd_qk, latent, tm):
    nb, seq, _ = x.shape
    grid = (nb, seq // tm)
    mod_map = (lambda b, i: (b, 0, 0)) if latent else (lambda b, i: (0, 0, 0))
    in_specs = [
        pl.BlockSpec((None, tm, D_MODEL), lambda b, i: (b, i, 0)),
        pl.BlockSpec((None, 6, D_MODEL), mod_map),
        pl.BlockSpec((D_MODEL, IN_WIDTH), lambda b, i: (0, 0)),
        pl.BlockSpec((1, QK_DIM), lambda b, i: (0, 0)),
        pl.BlockSpec((QK_DIM, QK_DIM), lambda b, i: (0, 0)),
    ]
    args = [x, mod, w_in, qk_gain, bd_qk]
    out_shape = [
        jax.ShapeDtypeStruct((nb, N_HEADS, seq, HEAD_DIM), BF16),
        jax.ShapeDtypeStruct((nb, N_KV_HEADS, HEAD_DIM, seq), BF16),
        jax.ShapeDtypeStruct((nb, N_KV_HEADS, seq, KV_DIM), BF16),
        jax.ShapeDtypeStruct((nb, seq, HY_IN), F32),
    ]
    out_specs = [
        pl.BlockSpec((None, N_HEADS, tm, HEAD_DIM), lambda b, i: (b, 0, i, 0)),
        pl.BlockSpec((None, N_KV_HEADS, HEAD_DIM, tm), lambda b, i: (b, 0, 0, i)),
        pl.BlockSpec((None, N_KV_HEADS, tm, KV_DIM), lambda b, i: (b, 0, i, 0)),
        pl.BlockSpec((None, tm, HY_IN), lambda b, i: (b, i, 0)),
    ]
    if latent:
        cos, sin = _rope_tables(seq)
        in_specs += [pl.BlockSpec((tm, LANES), lambda b, i: (i, 0))] * 2
        args += [cos, sin]
    else:
        out_shape += [jax.ShapeDtypeStruct((nb, seq, KV_DIM), F32)] * 2
        out_specs += [pl.BlockSpec((None, tm, KV_DIM), lambda b, i: (b, i, 0))] * 2
    return pl.pallas_call(
        functools.partial(_inproj_kernel, latent),
        out_shape=out_shape,
        grid=grid,
        in_specs=in_specs,
        out_specs=out_specs,
        compiler_params=_cparams(("parallel", "parallel")),
    )(*args)


def _attn_kernel(n_ctx, chunk, q_ref, kt_ref, v_ref, gain_ref, *rest):
    if n_ctx:
        ckt_ref, cv_ref, o_ref = rest
    else:
        (o_ref,) = rest
    g, tq, _ = q_ref.shape
    seq = kt_ref.shape[1]
    pieces = [(kt_ref[:, c * chunk:(c + 1) * chunk], v_ref[c * chunk:(c + 1) * chunk, :])
              for c in range(seq // chunk)]
    if n_ctx:
        pieces.append((ckt_ref[...], cv_ref[...]))
    ts = tq // ATTN_SUBTILES
    for t in range(ATTN_SUBTILES):
        rows = slice(t * ts, (t + 1) * ts)
        qs = q_ref[:, rows, :].reshape(g * ts, HEAD_DIM)
        m = acc = None
        for kt_c, v_c in pieces:
            s = _dot(qs, kt_c)
            row_max = jnp.max(s, axis=1, keepdims=True)
            m_new = row_max if m is None else jnp.maximum(m, row_max)
            pv = _dot(jnp.exp2(s - m_new).astype(BF16), v_c)
            acc = pv if m is None else jnp.exp2(m - m_new) * acc + pv
            m = m_new
        o = acc[:, :HEAD_DIM] / acc[:, HEAD_DIM:HEAD_DIM + 1]
        o = o * lax.rsqrt(jnp.mean(o * o, axis=1, keepdims=True) + EPS)
        for i in range(g):
            cols = slice(i * HEAD_DIM, (i + 1) * HEAD_DIM)
            o_ref[rows, cols] = (o[i * ts:(i + 1) * ts] * gain_ref[:, cols]).astype(o_ref.dtype)


def _attention(q, kt, v, gain, ckt, cv, tq, chunk):
    nb, _, seq, _ = q.shape
    n_ctx = 0 if ckt is None else ckt.shape[-1]
    width = GQA_GROUP * HEAD_DIM
    in_specs = [
        pl.BlockSpec((None, GQA_GROUP, tq, HEAD_DIM), lambda b, k, i: (b, k, i, 0)),
        pl.BlockSpec((None, None, HEAD_DIM, seq), lambda b, k, i: (b, k, 0, 0)),
        pl.BlockSpec((None, None, seq, KV_DIM), lambda b, k, i: (b, k, 0, 0)),
        pl.BlockSpec((1, width), lambda b, k, i: (0, k)),
    ]
    args = [q, kt, v, gain]
    if n_ctx:
        in_specs += [
            pl.BlockSpec((None, None, HEAD_DIM, n_ctx), lambda b, k, i: (b, k, 0, 0)),
            pl.BlockSpec((None, None, n_ctx, KV_DIM), lambda b, k, i: (b, k, 0, 0)),
        ]
        args += [ckt, cv]
    return pl.pallas_call(
        functools.partial(_attn_kernel, n_ctx, chunk),
        out_shape=jax.ShapeDtypeStruct((nb, seq, Q_DIM), BF16),
        grid=(nb, N_KV_HEADS, seq // tq),
        in_specs=in_specs,
        out_specs=pl.BlockSpec((None, tq, width), lambda b, k, i: (b, i, k)),
        compiler_params=_cparams(("parallel", "parallel", "parallel")),
    )(*args)


def _hyena_decimation(seq):
    return 8 if seq >= 2048 else 1


def _dft_tables(seq, r):
    n_sub = 2 * seq // r
    half = n_sub // 2
    k = np.arange(half, dtype=np.float64)[:, None]
    m = np.arange(n_sub, dtype=np.float64)[None, :]
    ang = 2.0 * np.pi * k * m / n_sub
    fwd = np.concatenate([np.cos(ang), -np.sin(ang)], axis=0)
    fwd[half] = np.cos(np.pi * m[0])
    inv = fwd.T.copy() * (2.0 / n_sub)
    inv[:, 0] *= 0.5
    inv[:, half] *= 0.5
    inv = inv[:seq // r]
    kk = np.arange(half, dtype=np.float64)[:, None] * np.ones((1, LANES))
    tw_r = np.cos(2.0 * np.pi * kk / n_sub)
    tw_i = -np.sin(2.0 * np.pi * kk / n_sub)
    return fwd, inv, tw_r, tw_i


def _filter_positions(seq, r):
    n_tot = 2 * seq
    n = (np.arange(n_tot // r)[None, :] * r + np.arange(r)[:, None]).reshape(-1)
    j = np.where(n < seq, n, n_tot - n)
    t = j.astype(np.float64) / seq
    bands = np.arange(1, HY_BANDS + 1, dtype=np.float64)
    ang = 2.0 * np.pi * t[:, None] * bands
    z = np.concatenate([t[:, None], np.sin(ang), np.cos(ang)], axis=-1)
    ones = np.ones((1, HY_FILTER_HIDDEN))
    sel_f = (n < seq).astype(np.float64)[:, None] * ones
    sel_b = (n > seq).astype(np.float64)[:, None] * ones
    return z, t[:, None] * np.ones((1, LANES)), sel_f, sel_b


def _filter_ffn_kernel(z_ref, self_ref, selb_ref, w1_ref, b1_ref, w2_ref, b2_ref, fr_ref, hf_ref, hb_ref):
    fr = fr_ref[...]
    h = jnp.sin(fr * (_dot3(z_ref[...], w1_ref[...]) + b1_ref[...]))
    h = jnp.sin(fr * (_dot3(h, w2_ref[...]) + b2_ref[...]))
    hf_ref[...] = h * self_ref[...]
    hb_ref[...] = h * selb_ref[...]


def _filter_spec_kernel(r, hf_ref, hb_ref, t_ref, w3f_ref, w3b_ref, dcf_ref, dcb_ref, fh_ref, gre_ref, gim_ref):
    t = t_ref[...]
    g = (_dot3(hf_ref[...], w3f_ref[...]) * jnp.exp(-t * jnp.abs(dcf_ref[...]))
         + _dot3(hb_ref[...], w3b_ref[...]) * jnp.exp(-t * jnp.abs(dcb_ref[...])))
    g = g * lax.rsqrt(jnp.sum(g * g, axis=0, keepdims=True) + EPS)
    n_sub = g.shape[0] // r
    half = n_sub // 2
    fh = fh_ref[...]
    for p in range(r):
        spec = _dot(fh, g[p * n_sub:(p + 1) * n_sub].astype(BF16))
        gre_ref[p] = spec[:half]
        gim_ref[p] = spec[half:]


def _hyena_filters(seq, r, w1, b1, w2, b2, w3, freq, decay, fwd):
    n_tot = 2 * seq
    n_sub = n_tot // r
    half = n_sub // 2
    z, t, sel_f, sel_b = _filter_positions(seq, r)
    pad = (-HY_POS_DIM) % SUBLANES
    z = jnp.asarray(np.pad(z, ((0, 0), (0, pad))), F32)
    w1p = jnp.pad(w1, ((0, pad), (0, 0)))
    kin = HY_POS_DIM + pad
    hid = HY_FILTER_HIDDEN
    tr = min(n_tot, 512)
    rows = lambda width: pl.BlockSpec((tr, width), lambda i: (i, 0))
    full = lambda shape: pl.BlockSpec(shape, lambda j: (0,) * len(shape))
    hf, hb = pl.pallas_call(
        _filter_ffn_kernel,
        out_shape=[jax.ShapeDtypeStruct((n_tot, hid), F32)] * 2,
        grid=(n_tot // tr,),
        in_specs=[rows(kin), rows(hid), rows(hid), full((kin, hid)), full((1, hid)), full((hid, hid)),
                  full((1, hid)), full((1, hid))],
        out_specs=[rows(hid)] * 2,
        compiler_params=_cparams(("parallel",)),
    )(z, jnp.asarray(sel_f, F32), jnp.asarray(sel_b, F32), w1p, b1.reshape(1, hid), w2, b2.reshape(1, hid),
      freq.reshape(1, hid))
    ncb = HY_OC // LANES
    return pl.pallas_call(
        functools.partial(_filter_spec_kernel, r),
        out_shape=[jax.ShapeDtypeStruct((r, half, HY_OC), F32)] * 2,
        grid=(ncb,),
        in_specs=[
            full((n_tot, hid)), full((n_tot, hid)), full((n_tot, LANES)),
            pl.BlockSpec((hid, LANES), lambda j: (0, j)),
            pl.BlockSpec((hid, LANES), lambda j: (0, j + ncb)),
            pl.BlockSpec((1, LANES), lambda j: (0, j)),
            pl.BlockSpec((1, LANES), lambda j: (0, j + ncb)),
            full((2 * half, n_sub)),
        ],
        out_specs=[pl.BlockSpec((r, half, LANES), lambda j: (0, 0, j))] * 2,
        compiler_params=_cparams(("parallel",)),
    )(hf, hb, jnp.asarray(t, F32), w3, w3, decay.reshape(1, -1), decay.reshape(1, -1), fwd)


def _hyena_kernel(r, hy0_ref, hy1_ref, hy2_ref, sw_ref, sb_ref, skip_ref,
                  gre0_ref, gim0_ref, gre1_ref, gim1_ref, fwd_ref, inv_ref, twr_ref, twi_ref,
                  o_ref, z_ref, ph_ref, rhs_ref, x_ref):
    seq = hy0_ref.shape[0]
    m_len = seq // r
    half = fwd_ref.shape[0] // 2
    row = lax.broadcasted_iota(jnp.int32, (m_len, LANES), 0)
    hy_refs = (hy0_ref, hy1_ref, hy2_ref)

    def load_phases(part):
        for j in range(r):
            ph_ref[j] = hy_refs[part][pl.ds(j, m_len, stride=r), :]

    def short_conv(part, j):
        w = sw_ref[:, part * LANES:(part + 1) * LANES]
        b = sb_ref[:, part * LANES:(part + 1) * LANES]
        if j > 0:
            prev = ph_ref[j - 1]
        else:
            prev = jnp.where(row == 0, 0.0, pltpu.roll(ph_ref[r - 1], 1, axis=0))
        if j < r - 1:
            nxt = ph_ref[j + 1]
        else:
            nxt = jnp.where(row == m_len - 1, 0.0, pltpu.roll(ph_ref[0], m_len - 1, axis=0))
        return prev * w[0:1] + ph_ref[j] * w[1:2] + nxt * w[2:3] + b

    load_phases(0)
    for j in range(r):
        z_ref[j] = short_conv(0, j)

    for o, (gre_ref, gim_ref) in enumerate(((gre0_ref, gim0_ref), (gre1_ref, gim1_ref))):
        for j in range(r):
            rhs_ref[:, j * LANES:(j + 1) * LANES] = z_ref[j].astype(BF16)
        x_ref[...] = _dot(fwd_ref[...], rhs_ref[...])
        dc = [x_ref[0:1, j * LANES:(j + 1) * LANES] for j in range(r)]
        ny = [x_ref[half:half + 1, j * LANES:(j + 1) * LANES] for j in range(r)]
        y_dc, y_ny = [], []
        for j in range(r):
            a = jnp.zeros((1, LANES), F32)
            c = jnp.zeros((1, LANES), F32)
            for jp in range(r):
                p = (j - jp) % r
                a = a + gre_ref[p, 0:1, :] * dc[jp]
                t = gim_ref[p, 0:1, :] * ny[jp]
                c = c + t if jp <= j else c - t
            y_dc.append(a)
            y_ny.append(c)

        def mix(i, carry):
            r0 = pl.multiple_of(i * SUBLANES, SUBLANES)
            rows_re = pl.ds(r0, SUBLANES)
            rows_im = pl.ds(half + r0, SUBLANES)
            xr = [x_ref[rows_re, j * LANES:(j + 1) * LANES] for j in range(r)]
            xi = [x_ref[rows_im, j * LANES:(j + 1) * LANES] for j in range(r)]
            wr = twr_ref[rows_re, :]
            wi = twi_ref[rows_re, :]
            for j in range(r):
                pr = pi = qr = qi = None
                for jp in range(r):
                    p = (j - jp) % r
                    gr = gre_ref[p, rows_re, :]
                    gi = gim_ref[p, rows_re, :]
                    tr = gr * xr[jp] - gi * xi[jp]
                    ti = gr * xi[jp] + gi * xr[jp]
                    if jp <= j:
                        pr = tr if pr is None else pr + tr
                        pi = ti if pi is None else pi + ti
                    else:
                        qr = tr if qr is None else qr + tr
                        qi = ti if qi is None else qi + ti
                if qr is not None:
                    pr = pr + wr * qr - wi * qi
                    pi = pi + wr * qi + wi * qr
                x_ref[rows_re, j * LANES:(j + 1) * LANES] = pr
                x_ref[rows_im, j * LANES:(j + 1) * LANES] = pi
            return carry

        lax.fori_loop(0, half // SUBLANES, mix, 0)
        for j in range(r):
            x_ref[0:1, j * LANES:(j + 1) * LANES] = y_dc[j]
            x_ref[half:half + 1, j * LANES:(j + 1) * LANES] = y_ny[j]
        y = _dot(inv_ref[...], x_ref[...].astype(BF16))
        sk = skip_ref[o:o + 1, :]
        load_phases(o + 1)
        for j in range(r):
            z_ref[j] = short_conv(o + 1, j) * (y[:, j * LANES:(j + 1) * LANES] + z_ref[j] * sk)
    for j in range(r):
        o_ref[pl.ds(j, m_len, stride=r), :] = z_ref[j]


def _hyena_direct_kernel(hy_ref, sw_ref, sb_ref, skip_ref, gre_ref, gim_ref, fwd_ref, inv_ref, o_ref):
    bb, seq, _ = hy_ref.shape
    half = fwd_ref.shape[0] // 2
    w = HYENA_WIDTH
    row = lax.broadcasted_iota(jnp.int32, (seq, w), 0)
    is_dc = lax.broadcasted_iota(jnp.int32, (half, w), 0) == 0

    def short_conv(b, part):
        cols = slice(part * w, (part + 1) * w)
        x = hy_ref[b, :, cols]
        prev = jnp.where(row == 0, 0.0, pltpu.roll(x, 1, axis=0))
        nxt = jnp.where(row == seq - 1, 0.0, pltpu.roll(x, seq - 1, axis=0))
        return prev * sw_ref[0:1, cols] + x * sw_ref[1:2, cols] + nxt * sw_ref[2:3, cols] + sb_ref[:, cols]

    for b in range(bb):
        z = short_conv(b, 0)
        for o in range(HY_ORDER):
            cols = slice(o * w, (o + 1) * w)
            x = _dot(fwd_ref[...], z.astype(BF16))
            xr, xi = x[:half], x[half:]
            gr, gi = gre_ref[0, :, cols], gim_ref[0, :, cols]
            vr = gr * xr - jnp.where(is_dc, 0.0, gi * xi)
            vi = jnp.where(is_dc, gi * xi, gr * xi + gi * xr)
            y = _dot(inv_ref[...], jnp.concatenate([vr, vi], axis=0).astype(BF16))
            z = short_conv(b, o + 1) * (y + z * skip_ref[o:o + 1, :])
        o_ref[b] = z


def _hyena_direct(hy, short_w, short_b, skip, gre, gim, fwd, inv):
    nb, seq, _ = hy.shape
    bb = _pick_tile(nb, 4)
    const = lambda a: pl.BlockSpec(a.shape, lambda i: (0,) * a.ndim)
    short_b = short_b.reshape(1, -1)
    return pl.pallas_call(
        _hyena_direct_kernel,
        out_shape=jax.ShapeDtypeStruct((nb, seq, HYENA_WIDTH), F32),
        grid=(nb // bb,),
        in_specs=[pl.BlockSpec((bb, seq, HY_IN), lambda i: (i, 0, 0)), const(short_w), const(short_b), const(skip),
                  const(gre), const(gim), const(fwd), const(inv)],
        out_specs=pl.BlockSpec((bb, seq, HYENA_WIDTH), lambda i: (i, 0, 0)),
        compiler_params=_cparams(("parallel",)),
    )(hy, short_w, short_b, skip, gre, gim, fwd, inv)


def _hyena(hy, short_w, short_b, skip, gre, gim, fwd, inv, tw_r, tw_i, r):
    if r == 1:
        return _hyena_direct(hy, short_w, short_b, skip, gre, gim, fwd, inv)
    nb, seq, _ = hy.shape
    m_len = seq // r
    n_half2 = fwd.shape[0]
    half = n_half2 // 2
    ncb = HYENA_WIDTH // LANES
    parts = HY_ORDER + 1
    once = pl.Buffered(1)
    hy_spec = lambda part: pl.BlockSpec((None, seq, LANES), lambda c, b: (b, 0, part * ncb + c))
    g_spec = lambda o: pl.BlockSpec((r, half, LANES), lambda c, b: (0, 0, o * ncb + c), pipeline_mode=once)
    const = lambda shape: pl.BlockSpec(shape, lambda c, b: (0,) * len(shape), pipeline_mode=once)
    sw = short_w.reshape(3, parts, ncb, LANES).transpose(2, 0, 1, 3).reshape(ncb, 3, parts * LANES)
    sb = short_b.reshape(1, parts, ncb, LANES).transpose(2, 0, 1, 3).reshape(ncb, 1, parts * LANES)
    return pl.pallas_call(
        functools.partial(_hyena_kernel, r),
        out_shape=jax.ShapeDtypeStruct((nb, seq, HYENA_WIDTH), F32),
        grid=(ncb, nb),
        in_specs=[
            hy_spec(0), hy_spec(1), hy_spec(2),
            pl.BlockSpec((None, 3, parts * LANES), lambda c, b: (c, 0, 0)),
            pl.BlockSpec((None, 1, parts * LANES), lambda c, b: (c, 0, 0)),
            pl.BlockSpec((HY_ORDER, LANES), lambda c, b: (0, c)),
            g_spec(0), g_spec(0), g_spec(1), g_spec(1),
            const((n_half2, m_len)), const((m_len, n_half2)),
            const((half, LANES)), const((half, LANES)),
        ],
        out_specs=pl.BlockSpec((None, seq, LANES), lambda c, b: (b, 0, c)),
        scratch_shapes=[
            pltpu.VMEM((r, m_len, LANES), F32),
            pltpu.VMEM((r, m_len, LANES), F32),
            pltpu.VMEM((m_len, r * LANES), BF16),
            pltpu.VMEM((n_half2, r * LANES), F32),
        ],
        compiler_params=_cparams(("parallel", "parallel")),
    )(hy, hy, hy, sw, sb, skip, gre, gim, gre, gim, fwd, inv, tw_r, tw_i)


def _route(logits):
    lane = lax.broadcasted_iota(jnp.int32, logits.shape, 1).astype(F32)
    big = jnp.float32(1e9)
    is_grp = (lane >= N_EXPERTS) & (lane < N_EXPERTS + N_GROUPS)
    gl = jnp.where(is_grp, logits, NEG_BIG)
    gmax = jnp.max(gl, axis=1, keepdims=True)
    gidx = jnp.min(jnp.where(gl == gmax, lane, big), axis=1, keepdims=True) - N_EXPERTS
    den = jnp.sum(jnp.where(is_grp, jnp.exp(gl - gmax), 0.0), axis=1, keepdims=True)
    pg_top = 1.0 / den
    lo = gidx * EXPERTS_PER_GROUP
    sel = jnp.where((lane >= lo) & (lane < lo + EXPERTS_PER_GROUP), logits, NEG_BIG)
    m1 = jnp.max(sel, axis=1, keepdims=True)
    i1 = jnp.min(jnp.where(sel == m1, lane, big), axis=1, keepdims=True)
    sel2 = jnp.where(lane == i1, NEG_BIG, sel)
    m2 = jnp.max(sel2, axis=1, keepdims=True)
    i2 = jnp.min(jnp.where(sel2 == m2, lane, big), axis=1, keepdims=True)
    e2 = jnp.exp(m2 - m1)
    w1 = pg_top / (1.0 + e2)
    w2 = pg_top * e2 / (1.0 + e2)
    comb = jnp.where(lane == i1, w1, 0.0) + jnp.where(lane == i2, w2, 0.0)
    return comb + jnp.where(lane == GROUP_ID_LANE, gidx, 0.0)


def _outproj_kernel(attn_ref, hyz_ref, x_ref, mod_ref, wo_ref, hg_ref, bd_ref, lng_ref, lnb_ref,
                    rwh_ref, rwl_ref, rb_ref, x1_ref, h2t_ref, comb_ref):
    m = mod_ref[...]
    z = hyz_ref[...]
    ms = _dot((z * z).astype(BF16), bd_ref[...])
    zn = (z * lax.rsqrt(ms + EPS) * hg_ref[...]).astype(BF16)
    half = wo_ref.shape[0] // 2
    mix = _dot(attn_ref[...], wo_ref[:half, :]) + _dot(zn, wo_ref[half:, :])
    x1 = _layer_norm(DEEPNORM_ALPHA * x_ref[...] + m[2:3] * mix, lng_ref[...], lnb_ref[...])
    h2 = x1 * (1.0 + m[4:5]) + m[3:4]
    x1_ref[...] = x1
    h2t_ref[...] = h2.T.astype(BF16)
    hh, hl = _split_bf16(h2)
    logits = _dot(hh, rwh_ref[...]) + _dot(hl, rwh_ref[...]) + _dot(hh, rwl_ref[...]) + rb_ref[...]
    comb_ref[...] = _route(logits)


def _outproj(attn, hyz, x, mod, w_out, hy_gain, bd_hy, ln_g, ln_b, rw_hi, rw_lo, rb, latent, tm):
    nb, seq, _ = x.shape
    mod_map = (lambda b, i: (b, 0, 0)) if latent else (lambda b, i: (0, 0, 0))
    const = lambda shape: pl.BlockSpec(shape, lambda b, i: (0,) * len(shape))
    tok = lambda width: pl.BlockSpec((None, tm, width), lambda b, i: (b, i, 0))
    return pl.pallas_call(
        _outproj_kernel,
        out_shape=[
            jax.ShapeDtypeStruct((nb, seq, D_MODEL), F32),
            jax.ShapeDtypeStruct((D_MODEL, nb * seq), BF16),
            jax.ShapeDtypeStruct((nb, seq, LANES), F32),
        ],
        grid=(nb, seq // tm),
        in_specs=[
            tok(Q_DIM), tok(HYENA_WIDTH), tok(D_MODEL),
            pl.BlockSpec((None, 6, D_MODEL), mod_map),
            const((D_MODEL, D_MODEL)), const((1, HYENA_WIDTH)), const((HYENA_WIDTH, HYENA_WIDTH)),
            const((1, D_MODEL)), const((1, D_MODEL)),
            const((D_MODEL, LANES)), const((D_MODEL, LANES)), const((1, LANES)),
        ],
        out_specs=[tok(D_MODEL), pl.BlockSpec((D_MODEL, tm), lambda b, i: (0, b * (seq // tm) + i)),
                   tok(LANES)],
        compiler_params=_cparams(("parallel", "parallel")),
    )(attn, hyz, x, mod, w_out, hy_gain, bd_hy, ln_g, ln_b, rw_hi, rw_lo, rb)


def _moe_kernel(n_tok, tok_per_mod, ht_ref, meta_ref, tri_ref, wg_ref, wu_ref, wd_ref, x1_ref, mod0_ref, mod1_ref,
                lng_ref, lnb_ref, o_ref, yt_ref, ext_ref, mt_ref):
    i = pl.program_id(0)
    g = pl.program_id(1)
    tmo = meta_ref.shape[0]
    tok0 = i * tmo
    row_valid = tok0 + lax.broadcasted_iota(jnp.int32, (tmo, 1), 0) < n_tok

    @pl.when(g == 0)
    def _():
        col_valid = tok0 + lax.broadcasted_iota(jnp.int32, (1, tmo), 1) < n_tok
        yt_ref[...] = jnp.zeros_like(yt_ref)
        ext_ref[:D_MODEL, :] = jnp.where(col_valid, ht_ref[...], jnp.zeros((), BF16))
        mt_ref[...] = jnp.where(row_valid, meta_ref[...], 0.0).T

    gid = jnp.where(row_valid, meta_ref[:, GROUP_ID_LANE:GROUP_ID_LANE + 1], -1.0)
    member = gid == g.astype(F32)
    count = jnp.sum(jnp.where(member, 1.0, 0.0))
    ones = jnp.where(jnp.broadcast_to(member, (tmo, LANES)), 1.0, 0.0).astype(BF16)
    rank = _dot(tri_ref[...], ones)[:, 0:1]
    slot = jnp.where(member, rank, -1.0)
    cw_hi, cw_lo = _split_bf16(mt_ref[pl.ds(pl.multiple_of(g * EXPERTS_PER_GROUP, EXPERTS_PER_GROUP),
                                             EXPERTS_PER_GROUP), :])
    ext_ref[D_MODEL:, :] = jnp.concatenate([cw_hi, cw_lo], axis=0)
    lane = lax.broadcasted_iota(jnp.int32, (tmo, MOE_CHUNK), 1).astype(F32)

    def chunk(c, carry):
        sel = jnp.where(slot == lane + (c * MOE_CHUNK).astype(F32), 1.0, 0.0).astype(BF16)
        picked = _dot(ext_ref[...], sel)
        xs = picked[:D_MODEL].astype(BF16)
        cw = picked[D_MODEL:D_MODEL + EXPERTS_PER_GROUP] + picked[D_MODEL + EXPERTS_PER_GROUP:]
        ys = None
        for e in range(EXPERTS_PER_GROUP):
            ff = slice(e * EXPERT_FF, (e + 1) * EXPERT_FF)
            act = _silu(_dot(wg_ref[ff, :], xs)) * _dot(wu_ref[ff, :], xs) * cw[e:e + 1]
            part = _dot(wd_ref[:, ff], act.astype(BF16))
            ys = part if ys is None else ys + part
        yt_ref[...] += lax.dot_general(ys.astype(BF16), sel, (((1,), (1,)), ((), ())),
                                       preferred_element_type=F32)
        return carry

    n_chunks = (count.astype(jnp.int32) + MOE_CHUNK - 1) // MOE_CHUNK
    lax.fori_loop(0, n_chunks, chunk, 0)

    @pl.when(g == pl.num_programs(1) - 1)
    def _():
        tok = tok0 + lax.broadcasted_iota(jnp.int32, (tmo, 1), 0)
        first = tok < (tok0 // tok_per_mod + 1) * tok_per_mod
        gate = jnp.where(first, mod0_ref[5:6, :], mod1_ref[5:6, :])
        y = DEEPNORM_ALPHA * x1_ref[...] + gate * yt_ref[...].T
        o_ref[...] = _layer_norm(y, lng_ref[...], lnb_ref[...])


def _moe(ht, meta, wgt, wut, wdt, x1, mod, ln_g, ln_b, tok_per_mod):
    tokens = meta.shape[0]
    n_mod = mod.shape[0]
    tmo = min(MOE_TILE, tokens)
    tri = jnp.asarray(np.tril(np.ones((tmo, tmo), np.float32), -1), BF16)
    ff = EXPERTS_PER_GROUP * EXPERT_FF
    const = lambda shape: pl.BlockSpec(shape, lambda i, g: (0,) * len(shape))
    tok = lambda width: pl.BlockSpec((tmo, width), lambda i, g: (i, 0))
    mod_spec = lambda off: pl.BlockSpec(
        (None, 6, D_MODEL), lambda i, g: (jnp.minimum((i * tmo) // tok_per_mod + off, n_mod - 1), 0, 0))
    return pl.pallas_call(
        functools.partial(_moe_kernel, tokens, tok_per_mod),
        out_shape=jax.ShapeDtypeStruct((tokens, D_MODEL), F32),
        grid=(pl.cdiv(tokens, tmo), N_GROUPS),
        in_specs=[
            pl.BlockSpec((D_MODEL, tmo), lambda i, g: (0, i)), tok(LANES), const((tmo, tmo)),
            pl.BlockSpec((None, ff, D_MODEL), lambda i, g: (g, 0, 0)),
            pl.BlockSpec((None, ff, D_MODEL), lambda i, g: (g, 0, 0)),
            pl.BlockSpec((None, D_MODEL, ff), lambda i, g: (g, 0, 0)),
            tok(D_MODEL), mod_spec(0), mod_spec(1),
            const((1, D_MODEL)), const((1, D_MODEL)),
        ],
        out_specs=tok(D_MODEL),
        scratch_shapes=[
            pltpu.VMEM((D_MODEL, tmo), F32),
            pltpu.VMEM((D_MODEL + 2 * EXPERTS_PER_GROUP, tmo), BF16),
            pltpu.VMEM((LANES, tmo), F32),
        ],
        compiler_params=_cparams(("parallel", "arbitrary")),
    )(ht, meta, tri, wgt, wut, wdt, x1, mod, mod, ln_g, ln_b)


def _block_diag_mean(width, group):
    idx = np.arange(width) // group
    return jnp.asarray((idx[:, None] == idx[None, :]).astype(np.float32) / group, BF16)


def _group_major_t(w):
    w = w.astype(BF16).reshape(N_GROUPS, EXPERTS_PER_GROUP, D_MODEL, EXPERT_FF)
    return w.transpose(0, 1, 3, 2).reshape(N_GROUPS, EXPERTS_PER_GROUP * EXPERT_FF, D_MODEL)


def _pick_tile(n, target):
    t = min(n, target)
    while n % t:
        t //= 2
    return t


def _trunk_layer(x, mod, p, latent, ctx_k=None, ctx_v=None):
    nb, seq, _ = x.shape
    tm = _pick_tile(seq, 512)
    q, kt, vx, hy, *kv_new = _inproj(x, mod, p['w_in'], p['qk_gain'], p['bd_qk'], latent, tm)
    if latent:
        ckt = jnp.transpose(ctx_k, (0, 2, 3, 1)).astype(BF16)
        cv = jnp.transpose(ctx_v, (0, 2, 1, 3))
        cvx = jnp.concatenate([cv, jnp.ones_like(cv)], axis=-1).astype(BF16)
    else:
        ckt = cvx = None
    attn = _attention(q, kt, vx, p['attn_gain'], ckt, cvx, _pick_tile(seq, 256), _pick_tile(seq, 1024))

    r = _hyena_decimation(seq)
    fwd, inv, tw_r, tw_i = _dft_tables(seq, r)
    fwd = jnp.asarray(fwd, F32).astype(BF16)
    gre, gim = _hyena_filters(seq, r, p['hy_f_w1'], p['hy_f_b1'], p['hy_f_w2'], p['hy_f_b2'], p['hy_f_w3'],
                              p['hy_freq'], p['hy_decay'], fwd)
    hyz = _hyena(hy, p['hy_short_w'], p['hy_short_b'], p['hy_skip'], gre, gim,
                 fwd[:, :seq // r], jnp.asarray(inv, F32).astype(BF16), jnp.asarray(tw_r, F32),
                 jnp.asarray(tw_i, F32), r)

    x1, h2t, comb = _outproj(attn, hyz, x, mod, p['w_out'], p['hy_gain'], p['bd_hy'], p['ln1_g'], p['ln1_b'],
                            p['rw_hi'], p['rw_lo'], p['rb'], latent, tm)
    tokens = nb * seq
    y = _moe(h2t, comb.reshape(tokens, LANES), p['wgt'], p['wut'], p['wdt'], x1.reshape(tokens, D_MODEL), mod,
             p['ln2_g'], p['ln2_b'], seq if latent else tokens)
    return y.reshape(nb, seq, D_MODEL), kv_new


def _prepare(w_in, q_gain, k_gain, attn_out_gain, hy_short_w, hy_short_b, hy_f_w1, hy_f_b1, hy_f_w2, hy_f_b2,
             hy_f_w3, hy_freq, hy_decay, hy_skip, hy_out_gain, w_out, ln1_g, ln1_b, router_grp_w, router_grp_b,
             router_exp_w, router_exp_b, exp_w_gate, exp_w_up, exp_w_down, ln2_g, ln2_b, l):
    row = lambda a: a.reshape(1, -1)
    rw = jnp.concatenate([router_exp_w[l], router_grp_w[l]], axis=1)
    rw = jnp.pad(rw, ((0, 0), (0, LANES - rw.shape[1])))
    rb = jnp.concatenate([router_exp_b[l], router_grp_b[l]])
    rb = jnp.pad(rb, (0, LANES - rb.shape[0]))
    rw_hi, rw_lo = _split_bf16(rw)
    return {
        'w_in': w_in[l].astype(BF16),
        'qk_gain': row(jnp.concatenate([jnp.tile(q_gain[l], N_HEADS), jnp.tile(k_gain[l], N_KV_HEADS)])),
        'bd_qk': _block_diag_mean(QK_DIM, HEAD_DIM),
        'attn_gain': row(attn_out_gain[l]),
        'hy_short_w': hy_short_w[l], 'hy_short_b': hy_short_b[l],
        'hy_f_w1': hy_f_w1[l], 'hy_f_b1': hy_f_b1[l], 'hy_f_w2': hy_f_w2[l], 'hy_f_b2': hy_f_b2[l],
        'hy_f_w3': hy_f_w3[l], 'hy_freq': hy_freq[l], 'hy_decay': hy_decay[l], 'hy_skip': hy_skip[l],
        'hy_gain': row(hy_out_gain[l]),
        'bd_hy': _block_diag_mean(HYENA_WIDTH, HY_GROUP_DIM),
        'w_out': w_out[l].astype(BF16),
        'ln1_g': row(ln1_g[l]), 'ln1_b': row(ln1_b[l]),
        'rw_hi': rw_hi, 'rw_lo': rw_lo, 'rb': row(rb),
        'wgt': _group_major_t(exp_w_gate[l]), 'wut': _group_major_t(exp_w_up[l]),
        'wdt': exp_w_down[l].astype(BF16).reshape(N_GROUPS, EXPERTS_PER_GROUP * EXPERT_FF, D_MODEL)
                            .transpose(0, 2, 1),
        'ln2_g': row(ln2_g[l]), 'ln2_b': row(ln2_b[l]),
    }


def kernel(x_prompt, x_sample, c, cache_k, cache_v, c_ctx, w_mod, b_mod, w_in, q_gain, k_gain, attn_out_gain, hy_short_w, hy_short_b, hy_f_w1, hy_f_b1, hy_f_w2, hy_f_b2, hy_f_w3, hy_freq, hy_decay, hy_skip, hy_out_gain, w_out, ln1_g, ln1_b, router_grp_w, router_grp_b, router_exp_w, router_exp_b, exp_w_gate, exp_w_up, exp_w_down, ln2_g, ln2_b):
    depth = w_mod.shape[0]
    n_lat = c.shape[0]
    cond = jnp.concatenate([c, c_ctx[None, :]], axis=0)
    rows = -(-cond.shape[0] // SUBLANES) * SUBLANES
    cond = jnp.pad(cond, ((0, rows - cond.shape[0]), (0, 0)))
    y_prompt, y_sample = x_prompt, x_sample
    ks_new, vs_new = [], []
    for l in range(depth):
        p = _prepare(w_in, q_gain, k_gain, attn_out_gain, hy_short_w, hy_short_b, hy_f_w1, hy_f_b1, hy_f_w2,
                     hy_f_b2, hy_f_w3, hy_freq, hy_decay, hy_skip, hy_out_gain, w_out, ln1_g, ln1_b,
                     router_grp_w, router_grp_b, router_exp_w, router_exp_b, exp_w_gate, exp_w_up, exp_w_down,
                     ln2_g, ln2_b, l)
        mod = _adaln(cond, w_mod[l], b_mod[l])
        mod_lat = mod[:n_lat].reshape(n_lat, 6, D_MODEL)
        mod_ctx = mod[n_lat:n_lat + 1].reshape(1, 6, D_MODEL)
        y_prompt, (v_new, k_new) = _trunk_layer(y_prompt, mod_ctx, p, False)
        ks_new.append(k_new.reshape(k_new.shape[:2] + (N_KV_HEADS, HEAD_DIM)))
        vs_new.append(v_new.reshape(v_new.shape[:2] + (N_KV_HEADS, HEAD_DIM)))
        y_sample, _ = _trunk_layer(y_sample, mod_lat, p, True, cache_k[:, l], cache_v[:, l])
    return (y_prompt, y_sample, jnp.stack(ks_new, axis=1), jnp.stack(vs_new, axis=1))
```

```python
import functools

import numpy as np
import jax
import jax.numpy as jnp
from jax import lax
from jax.experimental import pallas as pl
from jax.experimental.pallas import tpu as pltpu

F32 = jnp.float32
BF16 = jnp.bfloat16

D_MODEL = 1024
GRID_W = 64
HEAD_DIM = 64
N_HEADS = 8
N_KV_HEADS = 2
GQA_GROUP = N_HEADS // N_KV_HEADS
Q_DIM = N_HEADS * HEAD_DIM
KV_DIM = N_KV_HEADS * HEAD_DIM
QK_DIM = Q_DIM + KV_DIM
HYENA_WIDTH = 512
HY_ORDER = 2
HY_IN = (HY_ORDER + 1) * HYENA_WIDTH
HY_GROUP_DIM = 64
HY_BANDS = 16
HY_POS_DIM = 1 + 2 * HY_BANDS
HY_FILTER_HIDDEN = 64
HY_OC = HY_ORDER * HYENA_WIDTH
IN_WIDTH = Q_DIM + 2 * KV_DIM + HY_IN
ROPE_THETA = 10000.0
ROPE_FREQS = HEAD_DIM // 4
N_GROUPS = 4
EXPERTS_PER_GROUP = 8
N_EXPERTS = N_GROUPS * EXPERTS_PER_GROUP
EXPERT_FF = D_MODEL // 4
DEPTH = 1
DEEPNORM_ALPHA = (2.0 * DEPTH) ** 0.25
EPS = 1e-6

LANES = 128
SUBLANES = 8
VMEM_LIMIT = 56 * 1024 * 1024
MOE_TILE = 896
MOE_CHUNK = 256
GROUP_ID_LANE = N_EXPERTS
PROJ_SUBTILES = 2
ATTN_SUBTILES = 4
ATTN_LONG_SEQ = 1024
NEG_BIG = -1e30
LOG2_E = 1.4426950408889634


def _cparams(sem):
    return pltpu.CompilerParams(dimension_semantics=sem, vmem_limit_bytes=VMEM_LIMIT)


def _split_bf16(a):
    hi = a.astype(BF16)
    lo = (a - hi.astype(F32)).astype(BF16)
    return hi, lo


def _dot(a, b):
    return jnp.dot(a, b, preferred_element_type=F32)


def _dot3(a, b):
    ah, al = _split_bf16(a)
    bh, bl = _split_bf16(b)
    return _dot(ah, bh) + _dot(al, bh) + _dot(ah, bl)


def _silu(x):
    return x / (1.0 + jnp.exp(-x))


def _layer_norm(y, g, b):
    mu = jnp.mean(y, axis=-1, keepdims=True)
    yc = y - mu
    var = jnp.mean(yc * yc, axis=-1, keepdims=True)
    return yc * lax.rsqrt(var + EPS) * g + b


def _adaln_kernel(c_ref, w_ref, b_ref, o_ref):
    o_ref[...] = _dot3(_silu(c_ref[...]), w_ref[...]) + b_ref[...]


def _adaln(cond, w_mod, b_mod):
    rows = cond.shape[0]
    n = w_mod.shape[1]
    tn = 1536
    return pl.pallas_call(
        _adaln_kernel,
        out_shape=jax.ShapeDtypeStruct((rows, n), F32),
        grid=(n // tn,),
        in_specs=[
            pl.BlockSpec((rows, D_MODEL), lambda j: (0, 0)),
            pl.BlockSpec((D_MODEL, tn), lambda j: (0, j)),
            pl.BlockSpec((1, tn), lambda j: (0, j)),
        ],
        out_specs=pl.BlockSpec((rows, tn), lambda j: (0, j)),
        compiler_params=_cparams(("arbitrary",)),
    )(cond, w_mod, b_mod.reshape(1, n))


def _rope_tables(seq):
    t = np.arange(seq)
    rows = (t // GRID_W).astype(np.float64)
    cols = (t % GRID_W).astype(np.float64)
    inv_freq = ROPE_THETA ** (-np.arange(ROPE_FREQS, dtype=np.float64) / ROPE_FREQS)
    d = np.arange(LANES) % HEAD_DIM
    axis = d // (2 * ROPE_FREQS)
    f = d % ROPE_FREQS
    pos = np.where(axis[None, :] == 0, rows[:, None], cols[:, None])
    ang = pos * inv_freq[f][None, :]
    first = (d % (2 * ROPE_FREQS)) < ROPE_FREQS
    cos = np.cos(ang)
    sin = np.where(first[None, :], -np.sin(ang), np.sin(ang))
    return jnp.asarray(cos, F32), jnp.asarray(sin, F32)


def _inproj_kernel(latent, x_ref, mod_ref, w_ref, gain_ref, bd_ref, *rest):
    if latent:
        cos_ref, sin_ref, q_ref, kt_ref, vx_ref, hy_ref = rest
    else:
        q_ref, kt_ref, vx_ref, hy_ref, v_ref, knat_ref = rest
    m = mod_ref[...]
    tm = x_ref.shape[0]
    ts = tm // PROJ_SUBTILES
    for t in range(PROJ_SUBTILES):
        rows = slice(t * ts, (t + 1) * ts)
        h = x_ref[rows, :] * (1.0 + m[1:2]) + m[0:1]
        proj = _dot(h.astype(BF16), w_ref[...])
        qk = proj[:, :QK_DIM]
        ms = _dot((qk * qk).astype(BF16), bd_ref[...])
        qk = qk * lax.rsqrt(ms + EPS) * gain_ref[...]
        if not latent:
            knat_ref[rows, :] = qk[:, Q_DIM:]
        else:
            cos = cos_ref[rows, :]
            sin = sin_ref[rows, :]
            lane = lax.broadcasted_iota(jnp.int32, cos.shape, 1)
            first = (lane % (2 * ROPE_FREQS)) < ROPE_FREQS
            chunks = []
            for c in range(QK_DIM // LANES):
                xc = qk[:, c * LANES:(c + 1) * LANES]
                below = pltpu.roll(xc, ROPE_FREQS, axis=1)
                above = pltpu.roll(xc, LANES - ROPE_FREQS, axis=1)
                chunks.append(xc * cos + jnp.where(first, above, below) * sin)
            qk = jnp.concatenate(chunks, axis=1)
        qs = (qk[:, :Q_DIM] * (HEAD_DIM ** -0.5 * LOG2_E)).astype(BF16)
        for hd in range(N_HEADS):
            q_ref[hd, rows, :] = qs[:, hd * HEAD_DIM:(hd + 1) * HEAD_DIM]
        kt = qk[:, Q_DIM:].T
        for kh in range(N_KV_HEADS):
            kt_ref[kh, :, rows] = kt[kh * HEAD_DIM:(kh + 1) * HEAD_DIM].astype(BF16)
        v = proj[:, QK_DIM:QK_DIM + KV_DIM]
        low = lax.broadcasted_iota(jnp.int32, v.shape, 1) < HEAD_DIM
        vx_ref[0, rows, :] = jnp.where(low, v, 1.0).astype(BF16)
        vx_ref[1, rows, :] = jnp.where(low, pltpu.roll(v, HEAD_DIM, axis=1), 1.0).astype(BF16)
        if not latent:
            v_ref[rows, :] = v
        hy_ref[rows, :] = proj[:, QK_DIM + KV_DIM:]


def _inproj(x, mod, w_in, qk_gain, bd_qk, latent, tm):
    nb, seq, _ = x.shape
    grid = (nb, seq // tm)
    mod_map = (lambda b, i: (b, 0, 0)) if latent else (lambda b, i: (0, 0, 0))
    in_specs = [
        pl.BlockSpec((None, tm, D_MODEL), lambda b, i: (b, i, 0)),
        pl.BlockSpec((None, 6, D_MODEL), mod_map),
        pl.BlockSpec((D_MODEL, IN_WIDTH), lambda b, i: (0, 0)),
        pl.BlockSpec((1, QK_DIM), lambda b, i: (0, 0)),
        pl.BlockSpec((QK_DIM, QK_DIM), lambda b, i: (0, 0)),
    ]
    args = [x, mod, w_in, qk_gain, bd_qk]
    out_shape = [
        jax.ShapeDtypeStruct((nb, N_HEADS, seq, HEAD_DIM), BF16),
        jax.ShapeDtypeStruct((nb, N_KV_HEADS, HEAD_DIM, seq), BF16),
        jax.ShapeDtypeStruct((nb, N_KV_HEADS, seq, KV_DIM), BF16),
        jax.ShapeDtypeStruct((nb, seq, HY_IN), F32),
    ]
    out_specs = [
        pl.BlockSpec((None, N_HEADS, tm, HEAD_DIM), lambda b, i: (b, 0, i, 0)),
        pl.BlockSpec((None, N_KV_HEADS, HEAD_DIM, tm), lambda b, i: (b, 0, 0, i)),
        pl.BlockSpec((None, N_KV_HEADS, tm, KV_DIM), lambda b, i: (b, 0, i, 0)),
        pl.BlockSpec((None, tm, HY_IN), lambda b, i: (b, i, 0)),
    ]
    if latent:
        cos, sin = _rope_tables(seq)
        in_specs += [pl.BlockSpec((tm, LANES), lambda b, i: (i, 0))] * 2
        args += [cos, sin]
    else:
        out_shape += [jax.ShapeDtypeStruct((nb, seq, KV_DIM), F32)] * 2
        out_specs += [pl.BlockSpec((None, tm, KV_DIM), lambda b, i: (b, i, 0))] * 2
    return pl.pallas_call(
        functools.partial(_inproj_kernel, latent),
        out_shape=out_shape,
        grid=grid,
        in_specs=in_specs,
        out_specs=out_specs,
        compiler_params=_cparams(("parallel", "parallel")),
    )(*args)


def _attn_kernel(n_ctx, chunk, subtiles, q_ref, kt_ref, v_ref, gain_ref, *rest):
    if n_ctx:
        ckt_ref, cv_ref, o_ref = rest
    else:
        (o_ref,) = rest
    bb, heads, tq, _ = q_ref.shape
    seq = kt_ref.shape[-1]
    g = GQA_GROUP
    ts = tq // subtiles
    for b in range(bb):
        for k in range(heads // g):
            pieces = [(kt_ref[b, k, :, c * chunk:(c + 1) * chunk], v_ref[b, k, c * chunk:(c + 1) * chunk, :])
                      for c in range(seq // chunk)]
            if n_ctx:
                pieces.append((ckt_ref[b, k], cv_ref[b, k]))
            for t in range(subtiles):
                rows = slice(t * ts, (t + 1) * ts)
                qs = q_ref[b, k * g:(k + 1) * g, rows, :].reshape(g * ts, HEAD_DIM)
                m = acc = None
                for kt_c, v_c in pieces:
                    s = _dot(qs, kt_c)
                    row_max = jnp.max(s, axis=1, keepdims=True)
                    m_new = row_max if m is None else jnp.maximum(m, row_max)
                    pv = _dot(jnp.exp2(s - m_new).astype(BF16), v_c)
                    acc = pv if m is None else jnp.exp2(m - m_new) * acc + pv
                    m = m_new
                o = acc[:, :HEAD_DIM] / acc[:, HEAD_DIM:HEAD_DIM + 1]
                o = o * lax.rsqrt(jnp.mean(o * o, axis=1, keepdims=True) + EPS)
                for i in range(g):
                    cols = slice((k * g + i) * HEAD_DIM, (k * g + i + 1) * HEAD_DIM)
                    o_ref[b, rows, cols] = (o[i * ts:(i + 1) * ts] * gain_ref[:, cols]).astype(o_ref.dtype)


def _attention(q, kt, v, gain, ckt, cv):
    nb, _, seq, _ = q.shape
    n_ctx = 0 if ckt is None else ckt.shape[-1]
    chunk = _pick_tile(seq, 1024)
    if seq >= ATTN_LONG_SEQ:
        bb, kv, tq, subtiles = 1, 1, _pick_tile(seq, 512), ATTN_SUBTILES
    else:
        bb, kv, tq, subtiles = _pick_tile(nb, 4), N_KV_HEADS, seq, 1
    width = kv * GQA_GROUP * HEAD_DIM
    in_specs = [
        pl.BlockSpec((bb, kv * GQA_GROUP, tq, HEAD_DIM), lambda b, k, i: (b, k, i, 0)),
        pl.BlockSpec((bb, kv, HEAD_DIM, seq), lambda b, k, i: (b, k, 0, 0)),
        pl.BlockSpec((bb, kv, seq, KV_DIM), lambda b, k, i: (b, k, 0, 0)),
        pl.BlockSpec((1, width), lambda b, k, i: (0, k)),
    ]
    args = [q, kt, v, gain]
    if n_ctx:
        in_specs += [
            pl.BlockSpec((bb, kv, HEAD_DIM, n_ctx), lambda b, k, i: (b, k, 0, 0)),
            pl.BlockSpec((bb, kv, n_ctx, KV_DIM), lambda b, k, i: (b, k, 0, 0)),
        ]
        args += [ckt, cv]
    return pl.pallas_call(
        functools.partial(_attn_kernel, n_ctx, chunk, subtiles),
        out_shape=jax.ShapeDtypeStruct((nb, seq, Q_DIM), BF16),
        grid=(nb // bb, N_KV_HEADS // kv, seq // tq),
        in_specs=in_specs,
        out_specs=pl.BlockSpec((bb, tq, width), lambda b, k, i: (b, i, k)),
        compiler_params=_cparams(("parallel", "parallel", "parallel")),
    )(*args)


def _hyena_decimation(seq):
    return 8 if seq >= 2048 else 1


def _dft_tables(seq, r):
    n_sub = 2 * seq // r
    half = n_sub // 2
    k = np.arange(half, dtype=np.float64)[:, None]
    m = np.arange(n_sub, dtype=np.float64)[None, :]
    ang = 2.0 * np.pi * k * m / n_sub
    fwd = np.concatenate([np.cos(ang), -np.sin(ang)], axis=0)
    fwd[half] = np.cos(np.pi * m[0])
    inv = fwd.T.copy() * (2.0 / n_sub)
    inv[:, 0] *= 0.5
    inv[:, half] *= 0.5
    inv = inv[:seq // r]
    kk = np.arange(half, dtype=np.float64)[:, None] * np.ones((1, LANES))
    tw_r = np.cos(2.0 * np.pi * kk / n_sub)
    tw_i = -np.sin(2.0 * np.pi * kk / n_sub)
    return fwd, inv, tw_r, tw_i


def _filter_positions(seq, r):
    n_tot = 2 * seq
    n = (np.arange(n_tot // r)[None, :] * r + np.arange(r)[:, None]).reshape(-1)
    j = np.where(n < seq, n, n_tot - n)
    t = j.astype(np.float64) / seq
    bands = np.arange(1, HY_BANDS + 1, dtype=np.float64)
    ang = 2.0 * np.pi * t[:, None] * bands
    z = np.concatenate([t[:, None], np.sin(ang), np.cos(ang)], axis=-1)
    ones = np.ones((1, HY_FILTER_HIDDEN))
    sel_f = (n < seq).astype(np.float64)[:, None] * ones
    sel_b = (n > seq).astype(np.float64)[:, None] * ones
    return z, t[:, None] * np.ones((1, LANES)), sel_f, sel_b


def _filter_ffn_kernel(z_ref, self_ref, selb_ref, w1_ref, b1_ref, w2_ref, b2_ref, fr_ref, hf_ref, hb_ref):
    fr = fr_ref[...]
    h = jnp.sin(fr * (_dot3(z_ref[...], w1_ref[...]) + b1_ref[...]))
    h = jnp.sin(fr * (_dot3(h, w2_ref[...]) + b2_ref[...]))
    hf_ref[...] = h * self_ref[...]
    hb_ref[...] = h * selb_ref[...]


def _filter_spec_kernel(r, hf_ref, hb_ref, t_ref, w3f_ref, w3b_ref, dcf_ref, dcb_ref, fh_ref, ga_ref, gb_ref, gc_ref):
    t = t_ref[...]
    g = (_dot3(hf_ref[...], w3f_ref[...]) * jnp.exp(-t * jnp.abs(dcf_ref[...]))
         + _dot3(hb_ref[...], w3b_ref[...]) * jnp.exp(-t * jnp.abs(dcb_ref[...])))
    g = g * lax.rsqrt(jnp.sum(g * g, axis=0, keepdims=True) + EPS)
    n_sub = g.shape[0] // r
    half = n_sub // 2
    fh = fh_ref[...]
    for p in range(r):
        spec = _dot(fh, g[p * n_sub:(p + 1) * n_sub].astype(BF16))
        ga_ref[p] = spec[:half]
        gb_ref[p] = spec[half:] - spec[:half]
        gc_ref[p] = spec[half:] + spec[:half]


def _hyena_filters(seq, r, w1, b1, w2, b2, w3, freq, decay, fwd):
    n_tot = 2 * seq
    n_sub = n_tot // r
    half = n_sub // 2
    z, t, sel_f, sel_b = _filter_positions(seq, r)
    pad = (-HY_POS_DIM) % SUBLANES
    z = jnp.asarray(np.pad(z, ((0, 0), (0, pad))), F32)
    w1p = jnp.pad(w1, ((0, pad), (0, 0)))
    kin = HY_POS_DIM + pad
    hid = HY_FILTER_HIDDEN
    tr = min(n_tot, 512)
    rows = lambda width: pl.BlockSpec((tr, width), lambda i: (i, 0))
    full = lambda shape: pl.BlockSpec(shape, lambda j: (0,) * len(shape))
    hf, hb = pl.pallas_call(
        _filter_ffn_kernel,
        out_shape=[jax.ShapeDtypeStruct((n_tot, hid), F32)] * 2,
        grid=(n_tot // tr,),
        in_specs=[rows(kin), rows(hid), rows(hid), full((kin, hid)), full((1, hid)), full((hid, hid)),
                  full((1, hid)), full((1, hid))],
        out_specs=[rows(hid)] * 2,
        compiler_params=_cparams(("parallel",)),
    )(z, jnp.asarray(sel_f, F32), jnp.asarray(sel_b, F32), w1p, b1.reshape(1, hid), w2, b2.reshape(1, hid),
      freq.reshape(1, hid))
    ncb = HY_OC // LANES
    return pl.pallas_call(
        functools.partial(_filter_spec_kernel, r),
        out_shape=[jax.ShapeDtypeStruct((r, half, HY_OC), F32)] * 3,
        grid=(ncb,),
        in_specs=[
            full((n_tot, hid)), full((n_tot, hid)), full((n_tot, LANES)),
            pl.BlockSpec((hid, LANES), lambda j: (0, j)),
            pl.BlockSpec((hid, LANES), lambda j: (0, j + ncb)),
            pl.BlockSpec((1, LANES), lambda j: (0, j)),
            pl.BlockSpec((1, LANES), lambda j: (0, j + ncb)),
            full((2 * half, n_sub)),
        ],
        out_specs=[pl.BlockSpec((r, half, LANES), lambda j: (0, 0, j))] * 3,
        compiler_params=_cparams(("parallel",)),
    )(hf, hb, jnp.asarray(t, F32), w3, w3, decay.reshape(1, -1), decay.reshape(1, -1), fwd)


def _hyena_kernel(r, hy0_ref, hy1_ref, hy2_ref, sw_ref, sb_ref, skip_ref,
                  ga0_ref, gb0_ref, gc0_ref, ga1_ref, gb1_ref, gc1_ref, fwd_ref, inv_ref, twr_ref, twi_ref,
                  o_ref, z_ref, ph_ref, rhs_ref, x_ref):
    seq = hy0_ref.shape[0]
    m_len = seq // r
    half = fwd_ref.shape[0] // 2
    row = lax.broadcasted_iota(jnp.int32, (m_len, LANES), 0)
    hy_refs = (hy0_ref, hy1_ref, hy2_ref)

    def load_phases(part):
        for j in range(r):
            ph_ref[j] = hy_refs[part][pl.ds(j, m_len, stride=r), :]

    def short_conv(part, j):
        w = sw_ref[:, part * LANES:(part + 1) * LANES]
        b = sb_ref[:, part * LANES:(part + 1) * LANES]
        if j > 0:
            prev = ph_ref[j - 1]
        else:
            prev = jnp.where(row == 0, 0.0, pltpu.roll(ph_ref[r - 1], 1, axis=0))
        if j < r - 1:
            nxt = ph_ref[j + 1]
        else:
            nxt = jnp.where(row == m_len - 1, 0.0, pltpu.roll(ph_ref[0], m_len - 1, axis=0))
        return prev * w[0:1] + ph_ref[j] * w[1:2] + nxt * w[2:3] + b

    load_phases(0)
    for j in range(r):
        z_ref[j] = short_conv(0, j)

    for o, (ga_ref, gb_ref, gc_ref) in enumerate(((ga0_ref, gb0_ref, gc0_ref), (ga1_ref, gb1_ref, gc1_ref))):
        for j in range(r):
            rhs_ref[:, j * LANES:(j + 1) * LANES] = z_ref[j].astype(BF16)
        x_ref[...] = _dot(fwd_ref[...], rhs_ref[...])
        dc = [x_ref[0:1, j * LANES:(j + 1) * LANES] for j in range(r)]
        ny = [x_ref[half:half + 1, j * LANES:(j + 1) * LANES] for j in range(r)]
        y_dc, y_ny = [], []
        for j in range(r):
            a = jnp.zeros((1, LANES), F32)
            c = jnp.zeros((1, LANES), F32)
            for jp in range(r):
                p = (j - jp) % r
                a = a + ga_ref[p, 0:1, :] * dc[jp]
                t = (ga_ref[p, 0:1, :] + gb_ref[p, 0:1, :]) * ny[jp]
                c = c + t if jp <= j else c - t
            y_dc.append(a)
            y_ny.append(c)

        def mix(i, carry):
            r0 = pl.multiple_of(i * SUBLANES, SUBLANES)
            rows_re = pl.ds(r0, SUBLANES)
            rows_im = pl.ds(half + r0, SUBLANES)
            xr = [x_ref[rows_re, j * LANES:(j + 1) * LANES] for j in range(r)]
            xi = [x_ref[rows_im, j * LANES:(j + 1) * LANES] for j in range(r)]
            xs = [a + b for a, b in zip(xr, xi)]
            wr = twr_ref[rows_re, :]
            wi = twi_ref[rows_re, :]
            for j in range(r):
                acc = {}
                for jp in range(r):
                    p = (j - jp) % r
                    k1 = ga_ref[p, rows_re, :] * xs[jp]
                    k2 = gb_ref[p, rows_re, :] * xr[jp]
                    k3 = gc_ref[p, rows_re, :] * xi[jp]
                    side = jp <= j
                    ks = (k1, k2, k3)
                    acc[side] = ks if side not in acc else tuple(a + b for a, b in zip(acc[side], ks))
                pr = acc[True][0] - acc[True][2]
                pi = acc[True][0] + acc[True][1]
                if False in acc:
                    qr = acc[False][0] - acc[False][2]
                    qi = acc[False][0] + acc[False][1]
                    pr = pr + wr * qr - wi * qi
                    pi = pi + wr * qi + wi * qr
                x_ref[rows_re, j * LANES:(j + 1) * LANES] = pr
                x_ref[rows_im, j * LANES:(j + 1) * LANES] = pi
            return carry

        lax.fori_loop(0, half // SUBLANES, mix, 0)
        for j in range(r):
            x_ref[0:1, j * LANES:(j + 1) * LANES] = y_dc[j]
            x_ref[half:half + 1, j * LANES:(j + 1) * LANES] = y_ny[j]
        y = _dot(inv_ref[...], x_ref[...].astype(BF16))
        sk = skip_ref[o:o + 1, :]
        load_phases(o + 1)
        for j in range(r):
            z_ref[j] = short_conv(o + 1, j) * (y[:, j * LANES:(j + 1) * LANES] + z_ref[j] * sk)
    for j in range(r):
        o_ref[pl.ds(j, m_len, stride=r), :] = z_ref[j]


def _hyena_direct_kernel(hy_ref, sw_ref, sb_ref, skip_ref, ga_ref, gb_ref, fwd_ref, inv_ref, o_ref):
    bb, seq, _ = hy_ref.shape
    half = fwd_ref.shape[0] // 2
    w = HYENA_WIDTH
    row = lax.broadcasted_iota(jnp.int32, (seq, w), 0)
    is_dc = lax.broadcasted_iota(jnp.int32, (half, w), 0) == 0

    def short_conv(b, part):
        cols = slice(part * w, (part + 1) * w)
        x = hy_ref[b, :, cols]
        prev = jnp.where(row == 0, 0.0, pltpu.roll(x, 1, axis=0))
        nxt = jnp.where(row == seq - 1, 0.0, pltpu.roll(x, seq - 1, axis=0))
        return prev * sw_ref[0:1, cols] + x * sw_ref[1:2, cols] + nxt * sw_ref[2:3, cols] + sb_ref[:, cols]

    for b in range(bb):
        z = short_conv(b, 0)
        for o in range(HY_ORDER):
            cols = slice(o * w, (o + 1) * w)
            x = _dot(fwd_ref[...], z.astype(BF16))
            xr, xi = x[:half], x[half:]
            gr = ga_ref[0, :, cols]
            gi = gb_ref[0, :, cols] + gr
            vr = gr * xr - jnp.where(is_dc, 0.0, gi * xi)
            vi = jnp.where(is_dc, gi * xi, gr * xi + gi * xr)
            y = _dot(inv_ref[...], jnp.concatenate([vr, vi], axis=0).astype(BF16))
            z = short_conv(b, o + 1) * (y + z * skip_ref[o:o + 1, :])
        o_ref[b] = z


def _hyena_direct(hy, short_w, short_b, skip, ga, gb, fwd, inv):
    nb, seq, _ = hy.shape
    bb = _pick_tile(nb, 4)
    const = lambda a: pl.BlockSpec(a.shape, lambda i: (0,) * a.ndim)
    short_b = short_b.reshape(1, -1)
    return pl.pallas_call(
        _hyena_direct_kernel,
        out_shape=jax.ShapeDtypeStruct((nb, seq, HYENA_WIDTH), F32),
        grid=(nb // bb,),
        in_specs=[pl.BlockSpec((bb, seq, HY_IN), lambda i: (i, 0, 0)), const(short_w), const(short_b), const(skip),
                  const(ga), const(gb), const(fwd), const(inv)],
        out_specs=pl.BlockSpec((bb, seq, HYENA_WIDTH), lambda i: (i, 0, 0)),
        compiler_params=_cparams(("parallel",)),
    )(hy, short_w, short_b, skip, ga, gb, fwd, inv)


def _hyena(hy, short_w, short_b, skip, spectra, fwd, inv, tw_r, tw_i, r):
    ga, gb, gc = spectra
    if r == 1:
        return _hyena_direct(hy, short_w, short_b, skip, ga, gb, fwd, inv)
    nb, seq, _ = hy.shape
    m_len = seq // r
    n_half2 = fwd.shape[0]
    half = n_half2 // 2
    ncb = HYENA_WIDTH // LANES
    parts = HY_ORDER + 1
    once = pl.Buffered(1)
    hy_spec = lambda part: pl.BlockSpec((None, seq, LANES), lambda c, b: (b, 0, part * ncb + c))
    g_spec = lambda o: pl.BlockSpec((r, half, LANES), lambda c, b: (0, 0, o * ncb + c), pipeline_mode=once)
    const = lambda shape: pl.BlockSpec(shape, lambda c, b: (0,) * len(shape), pipeline_mode=once)
    sw = short_w.reshape(3, parts, ncb, LANES).transpose(2, 0, 1, 3).reshape(ncb, 3, parts * LANES)
    sb = short_b.reshape(1, parts, ncb, LANES).transpose(2, 0, 1, 3).reshape(ncb, 1, parts * LANES)
    return pl.pallas_call(
        functools.partial(_hyena_kernel, r),
        out_shape=jax.ShapeDtypeStruct((nb, seq, HYENA_WIDTH), F32),
        grid=(ncb, nb),
        in_specs=[
            hy_spec(0), hy_spec(1), hy_spec(2),
            pl.BlockSpec((None, 3, parts * LANES), lambda c, b: (c, 0, 0)),
            pl.BlockSpec((None, 1, parts * LANES), lambda c, b: (c, 0, 0)),
            pl.BlockSpec((HY_ORDER, LANES), lambda c, b: (0, c)),
            g_spec(0), g_spec(0), g_spec(0), g_spec(1), g_spec(1), g_spec(1),
            const((n_half2, m_len)), const((m_len, n_half2)),
            const((half, LANES)), const((half, LANES)),
        ],
        out_specs=pl.BlockSpec((None, seq, LANES), lambda c, b: (b, 0, c)),
        scratch_shapes=[
            pltpu.VMEM((r, m_len, LANES), F32),
            pltpu.VMEM((r, m_len, LANES), F32),
            pltpu.VMEM((m_len, r * LANES), BF16),
            pltpu.VMEM((n_half2, r * LANES), F32),
        ],
        compiler_params=_cparams(("parallel", "parallel")),
    )(hy, hy, hy, sw, sb, skip, ga, gb, gc, ga, gb, gc, fwd, inv, tw_r, tw_i)


def _route(logits):
    lane = lax.broadcasted_iota(jnp.int32, logits.shape, 1).astype(F32)
    big = jnp.float32(1e9)
    is_grp = (lane >= N_EXPERTS) & (lane < N_EXPERTS + N_GROUPS)
    gl = jnp.where(is_grp, logits, NEG_BIG)
    gmax = jnp.max(gl, axis=1, keepdims=True)
    gidx = jnp.min(jnp.where(gl == gmax, lane, big), axis=1, keepdims=True) - N_EXPERTS
    den = jnp.sum(jnp.where(is_grp, jnp.exp(gl - gmax), 0.0), axis=1, keepdims=True)
    pg_top = 1.0 / den
    lo = gidx * EXPERTS_PER_GROUP
    sel = jnp.where((lane >= lo) & (lane < lo + EXPERTS_PER_GROUP), logits, NEG_BIG)
    m1 = jnp.max(sel, axis=1, keepdims=True)
    i1 = jnp.min(jnp.where(sel == m1, lane, big), axis=1, keepdims=True)
    sel2 = jnp.where(lane == i1, NEG_BIG, sel)
    m2 = jnp.max(sel2, axis=1, keepdims=True)
    i2 = jnp.min(jnp.where(sel2 == m2, lane, big), axis=1, keepdims=True)
    e2 = jnp.exp(m2 - m1)
    w1 = pg_top / (1.0 + e2)
    w2 = pg_top * e2 / (1.0 + e2)
    comb = jnp.where(lane == i1, w1, 0.0) + jnp.where(lane == i2, w2, 0.0)
    return comb + jnp.where(lane == GROUP_ID_LANE, gidx, 0.0)


def _outproj_kernel(attn_ref, hyz_ref, x_ref, mod_ref, wo_ref, hg_ref, bd_ref, lng_ref, lnb_ref,
                    rwh_ref, rwl_ref, rb_ref, x1_ref, h2t_ref, comb_ref):
    m = mod_ref[...]
    tm = x_ref.shape[0]
    ts = tm // PROJ_SUBTILES
    half = wo_ref.shape[0] // 2
    for t in range(PROJ_SUBTILES):
        rows = slice(t * ts, (t + 1) * ts)
        z = hyz_ref[rows, :]
        ms = _dot((z * z).astype(BF16), bd_ref[...])
        zn = (z * lax.rsqrt(ms + EPS) * hg_ref[...]).astype(BF16)
        mix = _dot(attn_ref[rows, :], wo_ref[:half, :]) + _dot(zn, wo_ref[half:, :])
        x1 = _layer_norm(DEEPNORM_ALPHA * x_ref[rows, :] + m[2:3] * mix, lng_ref[...], lnb_ref[...])
        h2 = x1 * (1.0 + m[4:5]) + m[3:4]
        x1_ref[rows, :] = x1
        h2t_ref[:, rows] = h2.T.astype(BF16)
        hh, hl = _split_bf16(h2)
        logits = _dot(hh, rwh_ref[...]) + _dot(hl, rwh_ref[...]) + _dot(hh, rwl_ref[...]) + rb_ref[...]
        comb_ref[rows, :] = _route(logits)


def _outproj(attn, hyz, x, mod, w_out, hy_gain, bd_hy, ln_g, ln_b, rw_hi, rw_lo, rb, latent, tm):
    nb, seq, _ = x.shape
    mod_map = (lambda b, i: (b, 0, 0)) if latent else (lambda b, i: (0, 0, 0))
    const = lambda shape: pl.BlockSpec(shape, lambda b, i: (0,) * len(shape))
    tok = lambda width: pl.BlockSpec((None, tm, width), lambda b, i: (b, i, 0))
    return pl.pallas_call(
        _outproj_kernel,
        out_shape=[
            jax.ShapeDtypeStruct((nb, seq, D_MODEL), F32),
            jax.ShapeDtypeStruct((D_MODEL, nb * seq), BF16),
            jax.ShapeDtypeStruct((nb, seq, LANES), F32),
        ],
        grid=(nb, seq // tm),
        in_specs=[
            tok(Q_DIM), tok(HYENA_WIDTH), tok(D_MODEL),
            pl.BlockSpec((None, 6, D_MODEL), mod_map),
            const((D_MODEL, D_MODEL)), const((1, HYENA_WIDTH)), const((HYENA_WIDTH, HYENA_WIDTH)),
            const((1, D_MODEL)), const((1, D_MODEL)),
            const((D_MODEL, LANES)), const((D_MODEL, LANES)), const((1, LANES)),
        ],
        out_specs=[tok(D_MODEL), pl.BlockSpec((D_MODEL, tm), lambda b, i: (0, b * (seq // tm) + i)),
                   tok(LANES)],
        compiler_params=_cparams(("parallel", "parallel")),
    )(attn, hyz, x, mod, w_out, hy_gain, bd_hy, ln_g, ln_b, rw_hi, rw_lo, rb)


def _moe_kernel(n_tok, tok_per_mod, ht_ref, meta_ref, tri_ref, wg_ref, wu_ref, wd_ref, x1_ref, mod0_ref, mod1_ref,
                lng_ref, lnb_ref, o_ref, yt_ref, ext_ref, mt_ref):
    i = pl.program_id(0)
    g = pl.program_id(1)
    tmo = meta_ref.shape[0]
    tok0 = i * tmo
    row_valid = tok0 + lax.broadcasted_iota(jnp.int32, (tmo, 1), 0) < n_tok

    @pl.when(g == 0)
    def _():
        col_valid = tok0 + lax.broadcasted_iota(jnp.int32, (1, tmo), 1) < n_tok
        yt_ref[...] = jnp.zeros_like(yt_ref)
        ext_ref[:D_MODEL, :] = jnp.where(col_valid, ht_ref[...], jnp.zeros((), BF16))
        mt_ref[...] = jnp.where(row_valid, meta_ref[...], 0.0).T

    gid = jnp.where(row_valid, meta_ref[:, GROUP_ID_LANE:GROUP_ID_LANE + 1], -1.0)
    member = gid == g.astype(F32)
    count = jnp.sum(jnp.where(member, 1.0, 0.0))
    ones = jnp.where(jnp.broadcast_to(member, (tmo, LANES)), 1.0, 0.0).astype(BF16)
    rank = _dot(tri_ref[...], ones)[:, 0:1]
    slot = jnp.where(member, rank, -1.0)
    cw_hi, cw_lo = _split_bf16(mt_ref[pl.ds(pl.multiple_of(g * EXPERTS_PER_GROUP, EXPERTS_PER_GROUP),
                                             EXPERTS_PER_GROUP), :])
    ext_ref[D_MODEL:, :] = jnp.concatenate([cw_hi, cw_lo], axis=0)
    lane = lax.broadcasted_iota(jnp.int32, (tmo, MOE_CHUNK), 1).astype(F32)

    def chunk(c, carry):
        sel = jnp.where(slot == lane + (c * MOE_CHUNK).astype(F32), 1.0, 0.0).astype(BF16)
        picked = _dot(ext_ref[...], sel)
        xs = picked[:D_MODEL].astype(BF16)
        cw = picked[D_MODEL:D_MODEL + EXPERTS_PER_GROUP] + picked[D_MODEL + EXPERTS_PER_GROUP:]
        ys = None
        for e in range(EXPERTS_PER_GROUP):
            ff = slice(e * EXPERT_FF, (e + 1) * EXPERT_FF)
            act = _silu(_dot(wg_ref[ff, :], xs)) * _dot(wu_ref[ff, :], xs) * cw[e:e + 1]
            part = _dot(wd_ref[:, ff], act.astype(BF16))
            ys = part if ys is None else ys + part
        yt_ref[...] += lax.dot_general(ys.astype(BF16), sel, (((1,), (1,)), ((), ())),
                                       preferred_element_type=F32)
        return carry

    n_chunks = (count.astype(jnp.int32) + MOE_CHUNK - 1) // MOE_CHUNK
    lax.fori_loop(0, n_chunks, chunk, 0)

    @pl.when(g == pl.num_programs(1) - 1)
    def _():
        tok = tok0 + lax.broadcasted_iota(jnp.int32, (tmo, 1), 0)
        first = tok < (tok0 // tok_per_mod + 1) * tok_per_mod
        gate = jnp.where(first, mod0_ref[5:6, :], mod1_ref[5:6, :])
        y = DEEPNORM_ALPHA * x1_ref[...] + gate * yt_ref[...].T
        o_ref[...] = _layer_norm(y, lng_ref[...], lnb_ref[...])


def _moe(ht, meta, wgt, wut, wdt, x1, mod, ln_g, ln_b, tok_per_mod):
    tokens = meta.shape[0]
    n_mod = mod.shape[0]
    tmo = min(MOE_TILE, tokens)
    tri = jnp.asarray(np.tril(np.ones((tmo, tmo), np.float32), -1), BF16)
    ff = EXPERTS_PER_GROUP * EXPERT_FF
    const = lambda shape: pl.BlockSpec(shape, lambda i, g: (0,) * len(shape))
    tok = lambda width: pl.BlockSpec((tmo, width), lambda i, g: (i, 0))
    mod_spec = lambda off: pl.BlockSpec(
        (None, 6, D_MODEL), lambda i, g: (jnp.minimum((i * tmo) // tok_per_mod + off, n_mod - 1), 0, 0))
    return pl.pallas_call(
        functools.partial(_moe_kernel, tokens, tok_per_mod),
        out_shape=jax.ShapeDtypeStruct((tokens, D_MODEL), F32),
        grid=(pl.cdiv(tokens, tmo), N_GROUPS),
        in_specs=[
            pl.BlockSpec((D_MODEL, tmo), lambda i, g: (0, i)), tok(LANES), const((tmo, tmo)),
            pl.BlockSpec((None, ff, D_MODEL), lambda i, g: (g, 0, 0)),
            pl.BlockSpec((None, ff, D_MODEL), lambda i, g: (g, 0, 0)),
            pl.BlockSpec((None, D_MODEL, ff), lambda i, g: (g, 0, 0)),
            tok(D_MODEL), mod_spec(0), mod_spec(1),
            const((1, D_MODEL)), const((1, D_MODEL)),
        ],
        out_specs=tok(D_MODEL),
        scratch_shapes=[
            pltpu.VMEM((D_MODEL, tmo), F32),
            pltpu.VMEM((D_MODEL + 2 * EXPERTS_PER_GROUP, tmo), BF16),
            pltpu.VMEM((LANES, tmo), F32),
        ],
        compiler_params=_cparams(("parallel", "arbitrary")),
    )(ht, meta, tri, wgt, wut, wdt, x1, mod, mod, ln_g, ln_b)


def _block_diag_mean(width, group):
    idx = np.arange(width) // group
    return jnp.asarray((idx[:, None] == idx[None, :]).astype(np.float32) / group, BF16)


def _group_major_t(w):
    w = w.astype(BF16).reshape(N_GROUPS, EXPERTS_PER_GROUP, D_MODEL, EXPERT_FF)
    return w.transpose(0, 1, 3, 2).reshape(N_GROUPS, EXPERTS_PER_GROUP * EXPERT_FF, D_MODEL)


def _pick_tile(n, target):
    t = min(n, target)
    while n % t:
        t //= 2
    return t


def _trunk_layer(x, mod, p, latent, ctx_k=None, ctx_v=None):
    nb, seq, _ = x.shape
    tm = _pick_tile(seq, 1024)
    q, kt, vx, hy, *kv_new = _inproj(x, mod, p['w_in'], p['qk_gain'], p['bd_qk'], latent, tm)
    if latent:
        ckt = jnp.transpose(ctx_k, (0, 2, 3, 1)).astype(BF16)
        cv = jnp.transpose(ctx_v, (0, 2, 1, 3))
        cvx = jnp.concatenate([cv, jnp.ones_like(cv)], axis=-1).astype(BF16)
    else:
        ckt = cvx = None
    attn = _attention(q, kt, vx, p['attn_gain'], ckt, cvx)

    r = _hyena_decimation(seq)
    fwd, inv, tw_r, tw_i = _dft_tables(seq, r)
    fwd = jnp.asarray(fwd, F32).astype(BF16)
    spectra = _hyena_filters(seq, r, p['hy_f_w1'], p['hy_f_b1'], p['hy_f_w2'], p['hy_f_b2'], p['hy_f_w3'],
                              p['hy_freq'], p['hy_decay'], fwd)
    hyz = _hyena(hy, p['hy_short_w'], p['hy_short_b'], p['hy_skip'], spectra,
                 fwd[:, :seq // r], jnp.asarray(inv, F32).astype(BF16), jnp.asarray(tw_r, F32),
                 jnp.asarray(tw_i, F32), r)

    x1, h2t, comb = _outproj(attn, hyz, x, mod, p['w_out'], p['hy_gain'], p['bd_hy'], p['ln1_g'], p['ln1_b'],
                            p['rw_hi'], p['rw_lo'], p['rb'], latent, tm)
    tokens = nb * seq
    y = _moe(h2t, comb.reshape(tokens, LANES), p['wgt'], p['wut'], p['wdt'], x1.reshape(tokens, D_MODEL), mod,
             p['ln2_g'], p['ln2_b'], seq if latent else tokens)
    return y.reshape(nb, seq, D_MODEL), kv_new


def _prepare(w_in, q_gain, k_gain, attn_out_gain, hy_short_w, hy_short_b, hy_f_w1, hy_f_b1, hy_f_w2, hy_f_b2,
             hy_f_w3, hy_freq, hy_decay, hy_skip, hy_out_gain, w_out, ln1_g, ln1_b, router_grp_w, router_grp_b,
             router_exp_w, router_exp_b, exp_w_gate, exp_w_up, exp_w_down, ln2_g, ln2_b, l):
    row = lambda a: a.reshape(1, -1)
    rw = jnp.concatenate([router_exp_w[l], router_grp_w[l]], axis=1)
    rw = jnp.pad(rw, ((0, 0), (0, LANES - rw.shape[1])))
    rb = jnp.concatenate([router_exp_b[l], router_grp_b[l]])
    rb = jnp.pad(rb, (0, LANES - rb.shape[0]))
    rw_hi, rw_lo = _split_bf16(rw)
    return {
        'w_in': w_in[l].astype(BF16),
        'qk_gain': row(jnp.concatenate([jnp.tile(q_gain[l], N_HEADS), jnp.tile(k_gain[l], N_KV_HEADS)])),
        'bd_qk': _block_diag_mean(QK_DIM, HEAD_DIM),
        'attn_gain': row(attn_out_gain[l]),
        'hy_short_w': hy_short_w[l], 'hy_short_b': hy_short_b[l],
        'hy_f_w1': hy_f_w1[l], 'hy_f_b1': hy_f_b1[l], 'hy_f_w2': hy_f_w2[l], 'hy_f_b2': hy_f_b2[l],
        'hy_f_w3': hy_f_w3[l], 'hy_freq': hy_freq[l], 'hy_decay': hy_decay[l], 'hy_skip': hy_skip[l],
        'hy_gain': row(hy_out_gain[l]),
        'bd_hy': _block_diag_mean(HYENA_WIDTH, HY_GROUP_DIM),
        'w_out': w_out[l].astype(BF16),
        'ln1_g': row(ln1_g[l]), 'ln1_b': row(ln1_b[l]),
        'rw_hi': rw_hi, 'rw_lo': rw_lo, 'rb': row(rb),
        'wgt': _group_major_t(exp_w_gate[l]), 'wut': _group_major_t(exp_w_up[l]),
        'wdt': exp_w_down[l].astype(BF16).reshape(N_GROUPS, EXPERTS_PER_GROUP * EXPERT_FF, D_MODEL)
                            .transpose(0, 2, 1),
        'ln2_g': row(ln2_g[l]), 'ln2_b': row(ln2_b[l]),
    }


def kernel(x_prompt, x_sample, c, cache_k, cache_v, c_ctx, w_mod, b_mod, w_in, q_gain, k_gain, attn_out_gain, hy_short_w, hy_short_b, hy_f_w1, hy_f_b1, hy_f_w2, hy_f_b2, hy_f_w3, hy_freq, hy_decay, hy_skip, hy_out_gain, w_out, ln1_g, ln1_b, router_grp_w, router_grp_b, router_exp_w, router_exp_b, exp_w_gate, exp_w_up, exp_w_down, ln2_g, ln2_b):
    depth = w_mod.shape[0]
    n_lat = c.shape[0]
    cond = jnp.concatenate([c, c_ctx[None, :]], axis=0)
    rows = -(-cond.shape[0] // SUBLANES) * SUBLANES
    cond = jnp.pad(cond, ((0, rows - cond.shape[0]), (0, 0)))
    y_prompt, y_sample = x_prompt, x_sample
    ks_new, vs_new = [], []
    for l in range(depth):
        p = _prepare(w_in, q_gain, k_gain, attn_out_gain, hy_short_w, hy_short_b, hy_f_w1, hy_f_b1, hy_f_w2,
                     hy_f_b2, hy_f_w3, hy_freq, hy_decay, hy_skip, hy_out_gain, w_out, ln1_g, ln1_b,
                     router_grp_w, router_grp_b, router_exp_w, router_exp_b, exp_w_gate, exp_w_up, exp_w_down,
                     ln2_g, ln2_b, l)
        mod = _adaln(cond, w_mod[l], b_mod[l])
        mod_lat = mod[:n_lat].reshape(n_lat, 6, D_MODEL)
        mod_ctx = mod[n_lat:n_lat + 1].reshape(1, 6, D_MODEL)
        y_prompt, (v_new, k_new) = _trunk_layer(y_prompt, mod_ctx, p, False)
        ks_new.append(k_new.reshape(k_new.shape[:2] + (N_KV_HEADS, HEAD_DIM)))
        vs_new.append(v_new.reshape(v_new.shape[:2] + (N_KV_HEADS, HEAD_DIM)))
        y_sample, _ = _trunk_layer(y_sample, mod_lat, p, True, cache_k[:, l], cache_v[:, l])
    return (y_prompt, y_sample, jnp.stack(ks_new, axis=1), jnp.stack(vs_new, axis=1))
```

```python
import functools

import numpy as np
import jax
import jax.numpy as jnp
from jax import lax
from jax.experimental import pallas as pl
from jax.experimental.pallas import tpu as pltpu

F32 = jnp.float32
BF16 = jnp.bfloat16

D_MODEL = 1024
GRID_W = 64
HEAD_DIM = 64
N_HEADS = 8
N_KV_HEADS = 2
GQA_GROUP = N_HEADS // N_KV_HEADS
Q_DIM = N_HEADS * HEAD_DIM
KV_DIM = N_KV_HEADS * HEAD_DIM
QK_DIM = Q_DIM + KV_DIM
HYENA_WIDTH = 512
HY_ORDER = 2
HY_IN = (HY_ORDER + 1) * HYENA_WIDTH
HY_GROUP_DIM = 64
HY_BANDS = 16
HY_POS_DIM = 1 + 2 * HY_BANDS
HY_FILTER_HIDDEN = 64
HY_OC = HY_ORDER * HYENA_WIDTH
IN_WIDTH = Q_DIM + 2 * KV_DIM + HY_IN
ROPE_THETA = 10000.0
ROPE_FREQS = HEAD_DIM // 4
N_GROUPS = 4
EXPERTS_PER_GROUP = 8
N_EXPERTS = N_GROUPS * EXPERTS_PER_GROUP
EXPERT_FF = D_MODEL // 4
DEPTH = 1
DEEPNORM_ALPHA = (2.0 * DEPTH) ** 0.25
EPS = 1e-6

LANES = 128
SUBLANES = 8
VMEM_LIMIT = 56 * 1024 * 1024
MOE_TILE = 896
MOE_CHUNK = 256
GROUP_ID_LANE = N_EXPERTS
PROJ_SUBTILES = 2
PROJ_MIN_SUBTILE = 256
ATTN_SUBTILES = 4
ATTN_LONG_SEQ = 1024
NEG_BIG = -1e30
LOG2_E = 1.4426950408889634


def _cparams(sem):
    return pltpu.CompilerParams(dimension_semantics=sem, vmem_limit_bytes=VMEM_LIMIT)


def _proj_subtiles(tm):
    return PROJ_SUBTILES if tm // PROJ_SUBTILES >= PROJ_MIN_SUBTILE else 1


def _split_bf16(a):
    hi = a.astype(BF16)
    lo = (a - hi.astype(F32)).astype(BF16)
    return hi, lo


def _dot(a, b):
    return jnp.dot(a, b, preferred_element_type=F32)


def _dot3(a, b):
    ah, al = _split_bf16(a)
    bh, bl = _split_bf16(b)
    return _dot(ah, bh) + _dot(al, bh) + _dot(ah, bl)


def _silu(x):
    return x / (1.0 + jnp.exp(-x))


def _layer_norm(y, g, b):
    mu = jnp.mean(y, axis=-1, keepdims=True)
    yc = y - mu
    var = jnp.mean(yc * yc, axis=-1, keepdims=True)
    return yc * lax.rsqrt(var + EPS) * g + b


def _adaln_kernel(c_ref, w_ref, b_ref, o_ref):
    o_ref[...] = _dot3(_silu(c_ref[...]), w_ref[...]) + b_ref[...]


def _adaln(cond, w_mod, b_mod):
    rows = cond.shape[0]
    n = w_mod.shape[1]
    tn = 1536
    return pl.pallas_call(
        _adaln_kernel,
        out_shape=jax.ShapeDtypeStruct((rows, n), F32),
        grid=(n // tn,),
        in_specs=[
            pl.BlockSpec((rows, D_MODEL), lambda j: (0, 0)),
            pl.BlockSpec((D_MODEL, tn), lambda j: (0, j)),
            pl.BlockSpec((1, tn), lambda j: (0, j)),
        ],
        out_specs=pl.BlockSpec((rows, tn), lambda j: (0, j)),
        compiler_params=_cparams(("arbitrary",)),
    )(cond, w_mod, b_mod.reshape(1, n))


def _rope_tables(seq):
    t = np.arange(seq)
    rows = (t // GRID_W).astype(np.float64)
    cols = (t % GRID_W).astype(np.float64)
    inv_freq = ROPE_THETA ** (-np.arange(ROPE_FREQS, dtype=np.float64) / ROPE_FREQS)
    d = np.arange(LANES) % HEAD_DIM
    axis = d // (2 * ROPE_FREQS)
    f = d % ROPE_FREQS
    pos = np.where(axis[None, :] == 0, rows[:, None], cols[:, None])
    ang = pos * inv_freq[f][None, :]
    first = (d % (2 * ROPE_FREQS)) < ROPE_FREQS
    cos = np.cos(ang)
    sin = np.where(first[None, :], -np.sin(ang), np.sin(ang))
    return jnp.asarray(cos, F32), jnp.asarray(sin, F32)


def _inproj_kernel(latent, x_ref, mod_ref, w_ref, gain_ref, bd_ref, *rest):
    if latent:
        cos_ref, sin_ref, q_ref, kt_ref, vx_ref, hy_ref = rest
    else:
        q_ref, kt_ref, vx_ref, hy_ref, v_ref, knat_ref = rest
    m = mod_ref[...]
    tm = x_ref.shape[0]
    n_sub = _proj_subtiles(tm)
    ts = tm // n_sub
    for t in range(n_sub):
        rows = slice(t * ts, (t + 1) * ts)
        h = x_ref[rows, :] * (1.0 + m[1:2]) + m[0:1]
        proj = _dot(h.astype(BF16), w_ref[...])
        qk = proj[:, :QK_DIM]
        ms = _dot((qk * qk).astype(BF16), bd_ref[...])
        qk = qk * lax.rsqrt(ms + EPS) * gain_ref[...]
        if not latent:
            knat_ref[rows, :] = qk[:, Q_DIM:]
        else:
            cos = cos_ref[rows, :]
            sin = sin_ref[rows, :]
            lane = lax.broadcasted_iota(jnp.int32, cos.shape, 1)
            first = (lane % (2 * ROPE_FREQS)) < ROPE_FREQS
            chunks = []
            for c in range(QK_DIM // LANES):
                xc = qk[:, c * LANES:(c + 1) * LANES]
                below = pltpu.roll(xc, ROPE_FREQS, axis=1)
                above = pltpu.roll(xc, LANES - ROPE_FREQS, axis=1)
                chunks.append(xc * cos + jnp.where(first, above, below) * sin)
            qk = jnp.concatenate(chunks, axis=1)
        qs = (qk[:, :Q_DIM] * (HEAD_DIM ** -0.5 * LOG2_E)).astype(BF16)
        for hd in range(N_HEADS):
            q_ref[hd, rows, :] = qs[:, hd * HEAD_DIM:(hd + 1) * HEAD_DIM]
        kt = qk[:, Q_DIM:].T
        for kh in range(N_KV_HEADS):
            kt_ref[kh, :, rows] = kt[kh * HEAD_DIM:(kh + 1) * HEAD_DIM].astype(BF16)
        v = proj[:, QK_DIM:QK_DIM + KV_DIM]
        low = lax.broadcasted_iota(jnp.int32, v.shape, 1) < HEAD_DIM
        vx_ref[0, rows, :] = jnp.where(low, v, 1.0).astype(BF16)
        vx_ref[1, rows, :] = jnp.where(low, pltpu.roll(v, HEAD_DIM, axis=1), 1.0).astype(BF16)
        if not latent:
            v_ref[rows, :] = v
        hy_ref[rows, :] = proj[:, QK_DIM + KV_DIM:]


def _inproj(x, mod, w_in, qk_gain, bd_qk, latent, tm):
    nb, seq, _ = x.shape
    grid = (nb, seq // tm)
    mod_map = (lambda b, i: (b, 0, 0)) if latent else (lambda b, i: (0, 0, 0))
    in_specs = [
        pl.BlockSpec((None, tm, D_MODEL), lambda b, i: (b, i, 0)),
        pl.BlockSpec((None, 6, D_MODEL), mod_map),
        pl.BlockSpec((D_MODEL, IN_WIDTH), lambda b, i: (0, 0)),
        pl.BlockSpec((1, QK_DIM), lambda b, i: (0, 0)),
        pl.BlockSpec((QK_DIM, QK_DIM), lambda b, i: (0, 0)),
    ]
    args = [x, mod, w_in, qk_gain, bd_qk]
    out_shape = [
        jax.ShapeDtypeStruct((nb, N_HEADS, seq, HEAD_DIM), BF16),
        jax.ShapeDtypeStruct((nb, N_KV_HEADS, HEAD_DIM, seq), BF16),
        jax.ShapeDtypeStruct((nb, N_KV_HEADS, seq, KV_DIM), BF16),
        jax.ShapeDtypeStruct((nb, seq, HY_IN), F32),
    ]
    out_specs = [
        pl.BlockSpec((None, N_HEADS, tm, HEAD_DIM), lambda b, i: (b, 0, i, 0)),
        pl.BlockSpec((None, N_KV_HEADS, HEAD_DIM, tm), lambda b, i: (b, 0, 0, i)),
        pl.BlockSpec((None, N_KV_HEADS, tm, KV_DIM), lambda b, i: (b, 0, i, 0)),
        pl.BlockSpec((None, tm, HY_IN), lambda b, i: (b, i, 0)),
    ]
    if latent:
        cos, sin = _rope_tables(seq)
        in_specs += [pl.BlockSpec((tm, LANES), lambda b, i: (i, 0))] * 2
        args += [cos, sin]
    else:
        out_shape += [jax.ShapeDtypeStruct((nb, seq, KV_DIM), F32)] * 2
        out_specs += [pl.BlockSpec((None, tm, KV_DIM), lambda b, i: (b, i, 0))] * 2
    return pl.pallas_call(
        functools.partial(_inproj_kernel, latent),
        out_shape=out_shape,
        grid=grid,
        in_specs=in_specs,
        out_specs=out_specs,
        compiler_params=_cparams(("parallel", "parallel")),
    )(*args)


def _attn_kernel(n_ctx, chunk, subtiles, q_ref, kt_ref, v_ref, gain_ref, *rest):
    if n_ctx:
        ckt_ref, cv_ref, o_ref = rest
    else:
        (o_ref,) = rest
    bb, heads, tq, _ = q_ref.shape
    seq = kt_ref.shape[-1]
    g = GQA_GROUP
    ts = tq // subtiles
    for b in range(bb):
        for k in range(heads // g):
            pieces = [(kt_ref[b, k, :, c * chunk:(c + 1) * chunk], v_ref[b, k, c * chunk:(c + 1) * chunk, :])
                      for c in range(seq // chunk)]
            if n_ctx:
                pieces.append((ckt_ref[b, k], cv_ref[b, k]))
            for t in range(subtiles):
                rows = slice(t * ts, (t + 1) * ts)
                qs = q_ref[b, k * g:(k + 1) * g, rows, :].reshape(g * ts, HEAD_DIM)
                m = acc = None
                for kt_c, v_c in pieces:
                    s = _dot(qs, kt_c)
                    row_max = jnp.max(s, axis=1, keepdims=True)
                    m_new = row_max if m is None else jnp.maximum(m, row_max)
                    pv = _dot(jnp.exp2(s - m_new).astype(BF16), v_c)
                    acc = pv if m is None else jnp.exp2(m - m_new) * acc + pv
                    m = m_new
                o = acc[:, :HEAD_DIM] / acc[:, HEAD_DIM:HEAD_DIM + 1]
                o = o * lax.rsqrt(jnp.mean(o * o, axis=1, keepdims=True) + EPS)
                for i in range(g):
                    cols = slice((k * g + i) * HEAD_DIM, (k * g + i + 1) * HEAD_DIM)
                    o_ref[b, rows, cols] = (o[i * ts:(i + 1) * ts] * gain_ref[:, cols]).astype(o_ref.dtype)


def _attention(q, kt, v, gain, ckt, cv):
    nb, _, seq, _ = q.shape
    n_ctx = 0 if ckt is None else ckt.shape[-1]
    chunk = _pick_tile(seq, 2048)
    if seq >= ATTN_LONG_SEQ:
        bb, kv, tq, subtiles = 1, 1, _pick_tile(seq, 512), ATTN_SUBTILES
    else:
        bb, kv, tq, subtiles = _pick_tile(nb, 4), N_KV_HEADS, seq, 1
    width = kv * GQA_GROUP * HEAD_DIM
    in_specs = [
        pl.BlockSpec((bb, kv * GQA_GROUP, tq, HEAD_DIM), lambda b, k, i: (b, k, i, 0)),
        pl.BlockSpec((bb, kv, HEAD_DIM, seq), lambda b, k, i: (b, k, 0, 0)),
        pl.BlockSpec((bb, kv, seq, KV_DIM), lambda b, k, i: (b, k, 0, 0)),
        pl.BlockSpec((1, width), lambda b, k, i: (0, k)),
    ]
    args = [q, kt, v, gain]
    if n_ctx:
        in_specs += [
            pl.BlockSpec((bb, kv, HEAD_DIM, n_ctx), lambda b, k, i: (b, k, 0, 0)),
            pl.BlockSpec((bb, kv, n_ctx, KV_DIM), lambda b, k, i: (b, k, 0, 0)),
        ]
        args += [ckt, cv]
    return pl.pallas_call(
        functools.partial(_attn_kernel, n_ctx, chunk, subtiles),
        out_shape=jax.ShapeDtypeStruct((nb, seq, Q_DIM), BF16),
        grid=(nb // bb, N_KV_HEADS // kv, seq // tq),
        in_specs=in_specs,
        out_specs=pl.BlockSpec((bb, tq, width), lambda b, k, i: (b, i, k)),
        compiler_params=_cparams(("parallel", "parallel", "parallel")),
    )(*args)


def _hyena_decimation(seq):
    return 8 if seq >= 2048 else 1


def _dft_tables(seq, r):
    n_sub = 2 * seq // r
    half = n_sub // 2
    k = np.arange(half, dtype=np.float64)[:, None]
    m = np.arange(n_sub, dtype=np.float64)[None, :]
    ang = 2.0 * np.pi * k * m / n_sub
    fwd = np.concatenate([np.cos(ang), -np.sin(ang)], axis=0)
    fwd[half] = np.cos(np.pi * m[0])
    inv = fwd.T.copy() * (2.0 / n_sub)
    inv[:, 0] *= 0.5
    inv[:, half] *= 0.5
    inv = inv[:seq // r]
    kk = np.arange(half, dtype=np.float64)[:, None] * np.ones((1, LANES))
    tw_r = np.cos(2.0 * np.pi * kk / n_sub)
    tw_i = -np.sin(2.0 * np.pi * kk / n_sub)
    return fwd, inv, tw_r, tw_i


def _filter_positions(seq, r):
    n_tot = 2 * seq
    n = (np.arange(n_tot // r)[None, :] * r + np.arange(r)[:, None]).reshape(-1)
    j = np.where(n < seq, n, n_tot - n)
    t = j.astype(np.float64) / seq
    bands = np.arange(1, HY_BANDS + 1, dtype=np.float64)
    ang = 2.0 * np.pi * t[:, None] * bands
    z = np.concatenate([t[:, None], np.sin(ang), np.cos(ang)], axis=-1)
    ones = np.ones((1, HY_FILTER_HIDDEN))
    sel_f = (n < seq).astype(np.float64)[:, None] * ones
    sel_b = (n > seq).astype(np.float64)[:, None] * ones
    return z, t[:, None] * np.ones((1, LANES)), sel_f, sel_b


def _filter_ffn_kernel(z_ref, self_ref, selb_ref, w1_ref, b1_ref, w2_ref, b2_ref, fr_ref, hf_ref, hb_ref):
    fr = fr_ref[...]
    h = jnp.sin(fr * (_dot3(z_ref[...], w1_ref[...]) + b1_ref[...]))
    h = jnp.sin(fr * (_dot3(h, w2_ref[...]) + b2_ref[...]))
    hf_ref[...] = h * self_ref[...]
    hb_ref[...] = h * selb_ref[...]


def _filter_spec_kernel(r, hf_ref, hb_ref, t_ref, w3f_ref, w3b_ref, dcf_ref, dcb_ref, fh_ref, ga_ref, gb_ref, gc_ref):
    t = t_ref[...]
    g = (_dot3(hf_ref[...], w3f_ref[...]) * jnp.exp(-t * jnp.abs(dcf_ref[...]))
         + _dot3(hb_ref[...], w3b_ref[...]) * jnp.exp(-t * jnp.abs(dcb_ref[...])))
    g = g * lax.rsqrt(jnp.sum(g * g, axis=0, keepdims=True) + EPS)
    n_sub = g.shape[0] // r
    half = n_sub // 2
    fh = fh_ref[...]
    for p in range(r):
        spec = _dot(fh, g[p * n_sub:(p + 1) * n_sub].astype(BF16))
        ga_ref[p] = spec[:half]
        gb_ref[p] = spec[half:] - spec[:half]
        gc_ref[p] = spec[half:] + spec[:half]


def _hyena_filters(seq, r, w1, b1, w2, b2, w3, freq, decay, fwd):
    n_tot = 2 * seq
    n_sub = n_tot // r
    half = n_sub // 2
    z, t, sel_f, sel_b = _filter_positions(seq, r)
    pad = (-HY_POS_DIM) % SUBLANES
    z = jnp.asarray(np.pad(z, ((0, 0), (0, pad))), F32)
    w1p = jnp.pad(w1, ((0, pad), (0, 0)))
    kin = HY_POS_DIM + pad
    hid = HY_FILTER_HIDDEN
    tr = min(n_tot, 512)
    rows = lambda width: pl.BlockSpec((tr, width), lambda i: (i, 0))
    full = lambda shape: pl.BlockSpec(shape, lambda j: (0,) * len(shape))
    hf, hb = pl.pallas_call(
        _filter_ffn_kernel,
        out_shape=[jax.ShapeDtypeStruct((n_tot, hid), F32)] * 2,
        grid=(n_tot // tr,),
        in_specs=[rows(kin), rows(hid), rows(hid), full((kin, hid)), full((1, hid)), full((hid, hid)),
                  full((1, hid)), full((1, hid))],
        out_specs=[rows(hid)] * 2,
        compiler_params=_cparams(("parallel",)),
    )(z, jnp.asarray(sel_f, F32), jnp.asarray(sel_b, F32), w1p, b1.reshape(1, hid), w2, b2.reshape(1, hid),
      freq.reshape(1, hid))
    ncb = HY_OC // LANES
    return pl.pallas_call(
        functools.partial(_filter_spec_kernel, r),
        out_shape=[jax.ShapeDtypeStruct((r, half, HY_OC), F32)] * 3,
        grid=(ncb,),
        in_specs=[
            full((n_tot, hid)), full((n_tot, hid)), full((n_tot, LANES)),
            pl.BlockSpec((hid, LANES), lambda j: (0, j)),
            pl.BlockSpec((hid, LANES), lambda j: (0, j + ncb)),
            pl.BlockSpec((1, LANES), lambda j: (0, j)),
            pl.BlockSpec((1, LANES), lambda j: (0, j + ncb)),
            full((2 * half, n_sub)),
        ],
        out_specs=[pl.BlockSpec((r, half, LANES), lambda j: (0, 0, j))] * 3,
        compiler_params=_cparams(("parallel",)),
    )(hf, hb, jnp.asarray(t, F32), w3, w3, decay.reshape(1, -1), decay.reshape(1, -1), fwd)


def _hyena_kernel(r, hy0_ref, hy1_ref, hy2_ref, sw_ref, sb_ref, skip_ref,
                  ga0_ref, gb0_ref, gc0_ref, ga1_ref, gb1_ref, gc1_ref, fwd_ref, inv_ref, twr_ref, twi_ref,
                  o_ref, z_ref, ph_ref, rhs_ref, x_ref):
    seq = hy0_ref.shape[0]
    m_len = seq // r
    half = fwd_ref.shape[0] // 2
    row = lax.broadcasted_iota(jnp.int32, (m_len, LANES), 0)
    hy_refs = (hy0_ref, hy1_ref, hy2_ref)

    def load_phases(part):
        for j in range(r):
            ph_ref[j] = hy_refs[part][pl.ds(j, m_len, stride=r), :]

    def short_conv(part, j):
        w = sw_ref[:, part * LANES:(part + 1) * LANES]
        b = sb_ref[:, part * LANES:(part + 1) * LANES]
        if j > 0:
            prev = ph_ref[j - 1]
        else:
            prev = jnp.where(row == 0, 0.0, pltpu.roll(ph_ref[r - 1], 1, axis=0))
        if j < r - 1:
            nxt = ph_ref[j + 1]
        else:
            nxt = jnp.where(row == m_len - 1, 0.0, pltpu.roll(ph_ref[0], m_len - 1, axis=0))
        return prev * w[0:1] + ph_ref[j] * w[1:2] + nxt * w[2:3] + b

    load_phases(0)
    for j in range(r):
        z_ref[j] = short_conv(0, j)

    for o, (ga_ref, gb_ref, gc_ref) in enumerate(((ga0_ref, gb0_ref, gc0_ref), (ga1_ref, gb1_ref, gc1_ref))):
        for j in range(r):
            rhs_ref[:, j * LANES:(j + 1) * LANES] = z_ref[j].astype(BF16)
        x_ref[...] = _dot(fwd_ref[...], rhs_ref[...])
        dc = [x_ref[0:1, j * LANES:(j + 1) * LANES] for j in range(r)]
        ny = [x_ref[half:half + 1, j * LANES:(j + 1) * LANES] for j in range(r)]
        y_dc, y_ny = [], []
        for j in range(r):
            a = jnp.zeros((1, LANES), F32)
            c = jnp.zeros((1, LANES), F32)
            for jp in range(r):
                p = (j - jp) % r
                a = a + ga_ref[p, 0:1, :] * dc[jp]
                t = (ga_ref[p, 0:1, :] + gb_ref[p, 0:1, :]) * ny[jp]
                c = c + t if jp <= j else c - t
            y_dc.append(a)
            y_ny.append(c)

        def mix(i, carry):
            r0 = pl.multiple_of(i * SUBLANES, SUBLANES)
            rows_re = pl.ds(r0, SUBLANES)
            rows_im = pl.ds(half + r0, SUBLANES)
            xr = [x_ref[rows_re, j * LANES:(j + 1) * LANES] for j in range(r)]
            xi = [x_ref[rows_im, j * LANES:(j + 1) * LANES] for j in range(r)]
            xs = [a + b for a, b in zip(xr, xi)]
            wr = twr_ref[rows_re, :]
            wi = twi_ref[rows_re, :]
            for j in range(r):
                acc = {}
                for jp in range(r):
                    p = (j - jp) % r
                    k1 = ga_ref[p, rows_re, :] * xs[jp]
                    k2 = gb_ref[p, rows_re, :] * xr[jp]
                    k3 = gc_ref[p, rows_re, :] * xi[jp]
                    side = jp <= j
                    ks = (k1, k2, k3)
                    acc[side] = ks if side not in acc else tuple(a + b for a, b in zip(acc[side], ks))
                pr = acc[True][0] - acc[True][2]
                pi = acc[True][0] + acc[True][1]
                if False in acc:
                    qr = acc[False][0] - acc[False][2]
                    qi = acc[False][0] + acc[False][1]
                    pr = pr + wr * qr - wi * qi
                    pi = pi + wr * qi + wi * qr
                x_ref[rows_re, j * LANES:(j + 1) * LANES] = pr
                x_ref[rows_im, j * LANES:(j + 1) * LANES] = pi
            return carry

        lax.fori_loop(0, half // SUBLANES, mix, 0)
        for j in range(r):
            x_ref[0:1, j * LANES:(j + 1) * LANES] = y_dc[j]
            x_ref[half:half + 1, j * LANES:(j + 1) * LANES] = y_ny[j]
        y = _dot(inv_ref[...], x_ref[...].astype(BF16))
        sk = skip_ref[o:o + 1, :]
        load_phases(o + 1)
        for j in range(r):
            z_ref[j] = short_conv(o + 1, j) * (y[:, j * LANES:(j + 1) * LANES] + z_ref[j] * sk)
    for j in range(r):
        o_ref[pl.ds(j, m_len, stride=r), :] = z_ref[j]


def _hyena_direct_kernel(hy_ref, sw_ref, sb_ref, skip_ref, ga_ref, gb_ref, fwd_ref, inv_ref, o_ref):
    bb, seq, _ = hy_ref.shape
    half = fwd_ref.shape[0] // 2
    w = HYENA_WIDTH
    row = lax.broadcasted_iota(jnp.int32, (seq, w), 0)
    is_dc = lax.broadcasted_iota(jnp.int32, (half, w), 0) == 0

    def short_conv(b, part):
        cols = slice(part * w, (part + 1) * w)
        x = hy_ref[b, :, cols]
        prev = jnp.where(row == 0, 0.0, pltpu.roll(x, 1, axis=0))
        nxt = jnp.where(row == seq - 1, 0.0, pltpu.roll(x, seq - 1, axis=0))
        return prev * sw_ref[0:1, cols] + x * sw_ref[1:2, cols] + nxt * sw_ref[2:3, cols] + sb_ref[:, cols]

    for b in range(bb):
        z = short_conv(b, 0)
        for o in range(HY_ORDER):
            cols = slice(o * w, (o + 1) * w)
            x = _dot(fwd_ref[...], z.astype(BF16))
            xr, xi = x[:half], x[half:]
            gr = ga_ref[0, :, cols]
            gi = gb_ref[0, :, cols] + gr
            vr = gr * xr - jnp.where(is_dc, 0.0, gi * xi)
            vi = jnp.where(is_dc, gi * xi, gr * xi + gi * xr)
            y = _dot(inv_ref[...], jnp.concatenate([vr, vi], axis=0).astype(BF16))
            z = short_conv(b, o + 1) * (y + z * skip_ref[o:o + 1, :])
        o_ref[b] = z


def _hyena_direct(hy, short_w, short_b, skip, ga, gb, fwd, inv):
    nb, seq, _ = hy.shape
    bb = _pick_tile(nb, 4)
    const = lambda a: pl.BlockSpec(a.shape, lambda i: (0,) * a.ndim)
    short_b = short_b.reshape(1, -1)
    return pl.pallas_call(
        _hyena_direct_kernel,
        out_shape=jax.ShapeDtypeStruct((nb, seq, HYENA_WIDTH), F32),
        grid=(nb // bb,),
        in_specs=[pl.BlockSpec((bb, seq, HY_IN), lambda i: (i, 0, 0)), const(short_w), const(short_b), const(skip),
                  const(ga), const(gb), const(fwd), const(inv)],
        out_specs=pl.BlockSpec((bb, seq, HYENA_WIDTH), lambda i: (i, 0, 0)),
        compiler_params=_cparams(("parallel",)),
    )(hy, short_w, short_b, skip, ga, gb, fwd, inv)


def _hyena(hy, short_w, short_b, skip, spectra, fwd, inv, tw_r, tw_i, r):
    ga, gb, gc = spectra
    if r == 1:
        return _hyena_direct(hy, short_w, short_b, skip, ga, gb, fwd, inv)
    nb, seq, _ = hy.shape
    m_len = seq // r
    n_half2 = fwd.shape[0]
    half = n_half2 // 2
    ncb = HYENA_WIDTH // LANES
    parts = HY_ORDER + 1
    once = pl.Buffered(1)
    hy_spec = lambda part: pl.BlockSpec((None, seq, LANES), lambda c, b: (b, 0, part * ncb + c))
    g_spec = lambda o: pl.BlockSpec((r, half, LANES), lambda c, b: (0, 0, o * ncb + c), pipeline_mode=once)
    const = lambda shape: pl.BlockSpec(shape, lambda c, b: (0,) * len(shape), pipeline_mode=once)
    sw = short_w.reshape(3, parts, ncb, LANES).transpose(2, 0, 1, 3).reshape(ncb, 3, parts * LANES)
    sb = short_b.reshape(1, parts, ncb, LANES).transpose(2, 0, 1, 3).reshape(ncb, 1, parts * LANES)
    return pl.pallas_call(
        functools.partial(_hyena_kernel, r),
        out_shape=jax.ShapeDtypeStruct((nb, seq, HYENA_WIDTH), F32),
        grid=(ncb, nb),
        in_specs=[
            hy_spec(0), hy_spec(1), hy_spec(2),
            pl.BlockSpec((None, 3, parts * LANES), lambda c, b: (c, 0, 0)),
            pl.BlockSpec((None, 1, parts * LANES), lambda c, b: (c, 0, 0)),
            pl.BlockSpec((HY_ORDER, LANES), lambda c, b: (0, c)),
            g_spec(0), g_spec(0), g_spec(0), g_spec(1), g_spec(1), g_spec(1),
            const((n_half2, m_len)), const((m_len, n_half2)),
            const((half, LANES)), const((half, LANES)),
        ],
        out_specs=pl.BlockSpec((None, seq, LANES), lambda c, b: (b, 0, c)),
        scratch_shapes=[
            pltpu.VMEM((r, m_len, LANES), F32),
            pltpu.VMEM((r, m_len, LANES), F32),
            pltpu.VMEM((m_len, r * LANES), BF16),
            pltpu.VMEM((n_half2, r * LANES), F32),
        ],
        compiler_params=_cparams(("parallel", "parallel")),
    )(hy, hy, hy, sw, sb, skip, ga, gb, gc, ga, gb, gc, fwd, inv, tw_r, tw_i)


def _route(logits):
    lane = lax.broadcasted_iota(jnp.int32, logits.shape, 1).astype(F32)
    big = jnp.float32(1e9)
    is_grp = (lane >= N_EXPERTS) & (lane < N_EXPERTS + N_GROUPS)
    gl = jnp.where(is_grp, logits, NEG_BIG)
    gmax = jnp.max(gl, axis=1, keepdims=True)
    gidx = jnp.min(jnp.where(gl == gmax, lane, big), axis=1, keepdims=True) - N_EXPERTS
    den = jnp.sum(jnp.where(is_grp, jnp.exp(gl - gmax), 0.0), axis=1, keepdims=True)
    pg_top = 1.0 / den
    lo = gidx * EXPERTS_PER_GROUP
    sel = jnp.where((lane >= lo) & (lane < lo + EXPERTS_PER_GROUP), logits, NEG_BIG)
    m1 = jnp.max(sel, axis=1, keepdims=True)
    i1 = jnp.min(jnp.where(sel == m1, lane, big), axis=1, keepdims=True)
    sel2 = jnp.where(lane == i1, NEG_BIG, sel)
    m2 = jnp.max(sel2, axis=1, keepdims=True)
    i2 = jnp.min(jnp.where(sel2 == m2, lane, big), axis=1, keepdims=True)
    e2 = jnp.exp(m2 - m1)
    w1 = pg_top / (1.0 + e2)
    w2 = pg_top * e2 / (1.0 + e2)
    comb = jnp.where(lane == i1, w1, 0.0) + jnp.where(lane == i2, w2, 0.0)
    return comb + jnp.where(lane == GROUP_ID_LANE, gidx, 0.0)


def _outproj_kernel(attn_ref, hyz_ref, x_ref, mod_ref, wo_ref, hg_ref, bd_ref, lng_ref, lnb_ref,
                    rwh_ref, rwl_ref, rb_ref, x1_ref, h2t_ref, comb_ref):
    m = mod_ref[...]
    tm = x_ref.shape[0]
    n_sub = _proj_subtiles(tm)
    ts = tm // n_sub
    half = wo_ref.shape[0] // 2
    for t in range(n_sub):
        rows = slice(t * ts, (t + 1) * ts)
        z = hyz_ref[rows, :]
        ms = _dot((z * z).astype(BF16), bd_ref[...])
        zn = (z * lax.rsqrt(ms + EPS) * hg_ref[...]).astype(BF16)
        mix = _dot(attn_ref[rows, :], wo_ref[:half, :]) + _dot(zn, wo_ref[half:, :])
        x1 = _layer_norm(DEEPNORM_ALPHA * x_ref[rows, :] + m[2:3] * mix, lng_ref[...], lnb_ref[...])
        h2 = x1 * (1.0 + m[4:5]) + m[3:4]
        x1_ref[rows, :] = x1
        h2t_ref[:, rows] = h2.T.astype(BF16)
        hh, hl = _split_bf16(h2)
        logits = _dot(hh, rwh_ref[...]) + _dot(hl, rwh_ref[...]) + _dot(hh, rwl_ref[...]) + rb_ref[...]
        comb_ref[rows, :] = _route(logits)


def _outproj(attn, hyz, x, mod, w_out, hy_gain, bd_hy, ln_g, ln_b, rw_hi, rw_lo, rb, latent, tm):
    nb, seq, _ = x.shape
    mod_map = (lambda b, i: (b, 0, 0)) if latent else (lambda b, i: (0, 0, 0))
    const = lambda shape: pl.BlockSpec(shape, lambda b, i: (0,) * len(shape))
    tok = lambda width: pl.BlockSpec((None, tm, width), lambda b, i: (b, i, 0))
    return pl.pallas_call(
        _outproj_kernel,
        out_shape=[
            jax.ShapeDtypeStruct((nb, seq, D_MODEL), F32),
            jax.ShapeDtypeStruct((D_MODEL, nb * seq), BF16),
            jax.ShapeDtypeStruct((nb, seq, LANES), F32),
        ],
        grid=(nb, seq // tm),
        in_specs=[
            tok(Q_DIM), tok(HYENA_WIDTH), tok(D_MODEL),
            pl.BlockSpec((None, 6, D_MODEL), mod_map),
            const((D_MODEL, D_MODEL)), const((1, HYENA_WIDTH)), const((HYENA_WIDTH, HYENA_WIDTH)),
            const((1, D_MODEL)), const((1, D_MODEL)),
            const((D_MODEL, LANES)), const((D_MODEL, LANES)), const((1, LANES)),
        ],
        out_specs=[tok(D_MODEL), pl.BlockSpec((D_MODEL, tm), lambda b, i: (0, b * (seq // tm) + i)),
                   tok(LANES)],
        compiler_params=_cparams(("parallel", "parallel")),
    )(attn, hyz, x, mod, w_out, hy_gain, bd_hy, ln_g, ln_b, rw_hi, rw_lo, rb)


def _moe_kernel(n_tok, tok_per_mod, ht_ref, meta_ref, tri_ref, wg_ref, wu_ref, wd_ref, x1_ref, mod0_ref, mod1_ref,
                lng_ref, lnb_ref, o_ref, yt_ref, ext_ref, mt_ref):
    i = pl.program_id(0)
    g = pl.program_id(1)
    tmo = meta_ref.shape[0]
    tok0 = i * tmo
    row_valid = tok0 + lax.broadcasted_iota(jnp.int32, (tmo, 1), 0) < n_tok

    @pl.when(g == 0)
    def _():
        col_valid = tok0 + lax.broadcasted_iota(jnp.int32, (1, tmo), 1) < n_tok
        yt_ref[...] = jnp.zeros_like(yt_ref)
        ext_ref[:D_MODEL, :] = jnp.where(col_valid, ht_ref[...], jnp.zeros((), BF16))
        mt_ref[...] = jnp.where(row_valid, meta_ref[...], 0.0).T

    gid = jnp.where(row_valid, meta_ref[:, GROUP_ID_LANE:GROUP_ID_LANE + 1], -1.0)
    member = gid == g.astype(F32)
    count = jnp.sum(jnp.where(member, 1.0, 0.0))
    ones = jnp.where(jnp.broadcast_to(member, (tmo, LANES)), 1.0, 0.0).astype(BF16)
    rank = _dot(tri_ref[...], ones)[:, 0:1]
    slot = jnp.where(member, rank, -1.0)
    cw_hi, cw_lo = _split_bf16(mt_ref[pl.ds(pl.multiple_of(g * EXPERTS_PER_GROUP, EXPERTS_PER_GROUP),
                                             EXPERTS_PER_GROUP), :])
    ext_ref[D_MODEL:, :] = jnp.concatenate([cw_hi, cw_lo], axis=0)
    lane = lax.broadcasted_iota(jnp.int32, (tmo, MOE_CHUNK), 1).astype(F32)

    def chunk(c, carry):
        sel = jnp.where(slot == lane + (c * MOE_CHUNK).astype(F32), 1.0, 0.0).astype(BF16)
        picked = _dot(ext_ref[...], sel)
        xs = picked[:D_MODEL].astype(BF16)
        cw = picked[D_MODEL:D_MODEL + EXPERTS_PER_GROUP] + picked[D_MODEL + EXPERTS_PER_GROUP:]
        ys = None
        for e in range(EXPERTS_PER_GROUP):
            ff = slice(e * EXPERT_FF, (e + 1) * EXPERT_FF)
            act = _silu(_dot(wg_ref[ff, :], xs)) * _dot(wu_ref[ff, :], xs) * cw[e:e + 1]
            part = _dot(wd_ref[:, ff], act.astype(BF16))
            ys = part if ys is None else ys + part
        yt_ref[...] += lax.dot_general(ys.astype(BF16), sel, (((1,), (1,)), ((), ())),
                                       preferred_element_type=F32)
        return carry

    n_chunks = (count.astype(jnp.int32) + MOE_CHUNK - 1) // MOE_CHUNK
    lax.fori_loop(0, n_chunks, chunk, 0)

    @pl.when(g == pl.num_programs(1) - 1)
    def _():
        tok = tok0 + lax.broadcasted_iota(jnp.int32, (tmo, 1), 0)
        first = tok < (tok0 // tok_per_mod + 1) * tok_per_mod
        gate = jnp.where(first, mod0_ref[5:6, :], mod1_ref[5:6, :])
        y = DEEPNORM_ALPHA * x1_ref[...] + gate * yt_ref[...].T
        o_ref[...] = _layer_norm(y, lng_ref[...], lnb_ref[...])


def _moe(ht, meta, wgt, wut, wdt, x1, mod, ln_g, ln_b, tok_per_mod):
    tokens = meta.shape[0]
    n_mod = mod.shape[0]
    tmo = min(MOE_TILE, tokens)
    tri = jnp.asarray(np.tril(np.ones((tmo, tmo), np.float32), -1), BF16)
    ff = EXPERTS_PER_GROUP * EXPERT_FF
    const = lambda shape: pl.BlockSpec(shape, lambda i, g: (0,) * len(shape))
    tok = lambda width: pl.BlockSpec((tmo, width), lambda i, g: (i, 0))
    mod_spec = lambda off: pl.BlockSpec(
        (None, 6, D_MODEL), lambda i, g: (jnp.minimum((i * tmo) // tok_per_mod + off, n_mod - 1), 0, 0))
    return pl.pallas_call(
        functools.partial(_moe_kernel, tokens, tok_per_mod),
        out_shape=jax.ShapeDtypeStruct((tokens, D_MODEL), F32),
        grid=(pl.cdiv(tokens, tmo), N_GROUPS),
        in_specs=[
            pl.BlockSpec((D_MODEL, tmo), lambda i, g: (0, i)), tok(LANES), const((tmo, tmo)),
            pl.BlockSpec((None, ff, D_MODEL), lambda i, g: (g, 0, 0)),
            pl.BlockSpec((None, ff, D_MODEL), lambda i, g: (g, 0, 0)),
            pl.BlockSpec((None, D_MODEL, ff), lambda i, g: (g, 0, 0)),
            tok(D_MODEL), mod_spec(0), mod_spec(1),
            const((1, D_MODEL)), const((1, D_MODEL)),
        ],
        out_specs=tok(D_MODEL),
        scratch_shapes=[
            pltpu.VMEM((D_MODEL, tmo), F32),
            pltpu.VMEM((D_MODEL + 2 * EXPERTS_PER_GROUP, tmo), BF16),
            pltpu.VMEM((LANES, tmo), F32),
        ],
        compiler_params=_cparams(("parallel", "arbitrary")),
    )(ht, meta, tri, wgt, wut, wdt, x1, mod, mod, ln_g, ln_b)


def _block_diag_mean(width, group):
    idx = np.arange(width) // group
    return jnp.asarray((idx[:, None] == idx[None, :]).astype(np.float32) / group, BF16)


def _group_major_t(w):
    w = w.astype(BF16).reshape(N_GROUPS, EXPERTS_PER_GROUP, D_MODEL, EXPERT_FF)
    return w.transpose(0, 1, 3, 2).reshape(N_GROUPS, EXPERTS_PER_GROUP * EXPERT_FF, D_MODEL)


def _pick_tile(n, target):
    t = min(n, target)
    while n % t:
        t //= 2
    return t


def _trunk_layer(x, mod, p, latent, ctx_k=None, ctx_v=None):
    nb, seq, _ = x.shape
    tm = _pick_tile(seq, 1024)
    q, kt, vx, hy, *kv_new = _inproj(x, mod, p['w_in'], p['qk_gain'], p['bd_qk'], latent, tm)
    if latent:
        ckt = jnp.transpose(ctx_k, (0, 2, 3, 1)).astype(BF16)
        cv = jnp.transpose(ctx_v, (0, 2, 1, 3))
        cvx = jnp.concatenate([cv, jnp.ones_like(cv)], axis=-1).astype(BF16)
    else:
        ckt = cvx = None
    attn = _attention(q, kt, vx, p['attn_gain'], ckt, cvx)

    r = _hyena_decimation(seq)
    fwd, inv, tw_r, tw_i = _dft_tables(seq, r)
    fwd = jnp.asarray(fwd, F32).astype(BF16)
    spectra = _hyena_filters(seq, r, p['hy_f_w1'], p['hy_f_b1'], p['hy_f_w2'], p['hy_f_b2'], p['hy_f_w3'],
                              p['hy_freq'], p['hy_decay'], fwd)
    hyz = _hyena(hy, p['hy_short_w'], p['hy_short_b'], p['hy_skip'], spectra,
                 fwd[:, :seq // r], jnp.asarray(inv, F32).astype(BF16), jnp.asarray(tw_r, F32),
                 jnp.asarray(tw_i, F32), r)

    x1, h2t, comb = _outproj(attn, hyz, x, mod, p['w_out'], p['hy_gain'], p['bd_hy'], p['ln1_g'], p['ln1_b'],
                            p['rw_hi'], p['rw_lo'], p['rb'], latent, tm)
    tokens = nb * seq
    y = _moe(h2t, comb.reshape(tokens, LANES), p['wgt'], p['wut'], p['wdt'], x1.reshape(tokens, D_MODEL), mod,
             p['ln2_g'], p['ln2_b'], seq if latent else tokens)
    return y.reshape(nb, seq, D_MODEL), kv_new


def _prepare(w_in, q_gain, k_gain, attn_out_gain, hy_short_w, hy_short_b, hy_f_w1, hy_f_b1, hy_f_w2, hy_f_b2,
             hy_f_w3, hy_freq, hy_decay, hy_skip, hy_out_gain, w_out, ln1_g, ln1_b, router_grp_w, router_grp_b,
             router_exp_w, router_exp_b, exp_w_gate, exp_w_up, exp_w_down, ln2_g, ln2_b, l):
    row = lambda a: a.reshape(1, -1)
    rw = jnp.concatenate([router_exp_w[l], router_grp_w[l]], axis=1)
    rw = jnp.pad(rw, ((0, 0), (0, LANES - rw.shape[1])))
    rb = jnp.concatenate([router_exp_b[l], router_grp_b[l]])
    rb = jnp.pad(rb, (0, LANES - rb.shape[0]))
    rw_hi, rw_lo = _split_bf16(rw)
    return {
        'w_in': w_in[l].astype(BF16),
        'qk_gain': row(jnp.concatenate([jnp.tile(q_gain[l], N_HEADS), jnp.tile(k_gain[l], N_KV_HEADS)])),
        'bd_qk': _block_diag_mean(QK_DIM, HEAD_DIM),
        'attn_gain': row(attn_out_gain[l]),
        'hy_short_w': hy_short_w[l], 'hy_short_b': hy_short_b[l],
        'hy_f_w1': hy_f_w1[l], 'hy_f_b1': hy_f_b1[l], 'hy_f_w2': hy_f_w2[l], 'hy_f_b2': hy_f_b2[l],
        'hy_f_w3': hy_f_w3[l], 'hy_freq': hy_freq[l], 'hy_decay': hy_decay[l], 'hy_skip': hy_skip[l],
        'hy_gain': row(hy_out_gain[l]),
        'bd_hy': _block_diag_mean(HYENA_WIDTH, HY_GROUP_DIM),
        'w_out': w_out[l].astype(BF16),
        'ln1_g': row(ln1_g[l]), 'ln1_b': row(ln1_b[l]),
        'rw_hi': rw_hi, 'rw_lo': rw_lo, 'rb': row(rb),
        'wgt': _group_major_t(exp_w_gate[l]), 'wut': _group_major_t(exp_w_up[l]),
        'wdt': exp_w_down[l].astype(BF16).reshape(N_GROUPS, EXPERTS_PER_GROUP * EXPERT_FF, D_MODEL)
                            .transpose(0, 2, 1),
        'ln2_g': row(ln2_g[l]), 'ln2_b': row(ln2_b[l]),
    }


def kernel(x_prompt, x_sample, c, cache_k, cache_v, c_ctx, w_mod, b_mod, w_in, q_gain, k_gain, attn_out_gain, hy_short_w, hy_short_b, hy_f_w1, hy_f_b1, hy_f_w2, hy_f_b2, hy_f_w3, hy_freq, hy_decay, hy_skip, hy_out_gain, w_out, ln1_g, ln1_b, router_grp_w, router_grp_b, router_exp_w, router_exp_b, exp_w_gate, exp_w_up, exp_w_down, ln2_g, ln2_b):
    depth = w_mod.shape[0]
    n_lat = c.shape[0]
    cond = jnp.concatenate([c, c_ctx[None, :]], axis=0)
    rows = -(-cond.shape[0] // SUBLANES) * SUBLANES
    cond = jnp.pad(cond, ((0, rows - cond.shape[0]), (0, 0)))
    y_prompt, y_sample = x_prompt, x_sample
    ks_new, vs_new = [], []
    for l in range(depth):
        p = _prepare(w_in, q_gain, k_gain, attn_out_gain, hy_short_w, hy_short_b, hy_f_w1, hy_f_b1, hy_f_w2,
                     hy_f_b2, hy_f_w3, hy_freq, hy_decay, hy_skip, hy_out_gain, w_out, ln1_g, ln1_b,
                     router_grp_w, router_grp_b, router_exp_w, router_exp_b, exp_w_gate, exp_w_up, exp_w_down,
                     ln2_g, ln2_b, l)
        mod = _adaln(cond, w_mod[l], b_mod[l])
        mod_lat = mod[:n_lat].reshape(n_lat, 6, D_MODEL)
        mod_ctx = mod[n_lat:n_lat + 1].reshape(1, 6, D_MODEL)
        y_prompt, (v_new, k_new) = _trunk_layer(y_prompt, mod_ctx, p, False)
        ks_new.append(k_new.reshape(k_new.shape[:2] + (N_KV_HEADS, HEAD_DIM)))
        vs_new.append(v_new.reshape(v_new.shape[:2] + (N_KV_HEADS, HEAD_DIM)))
        y_sample, _ = _trunk_layer(y_sample, mod_lat, p, True, cache_k[:, l], cache_v[:, l])
    return (y_prompt, y_sample, jnp.stack(ks_new, axis=1), jnp.stack(vs_new, axis=1))
```

```python
import functools

import numpy as np
import jax
import jax.numpy as jnp
from jax import lax
from jax.experimental import pallas as pl
from jax.experimental.pallas import tpu as pltpu
from jax.experimental.pallas import tpu_sc as plsc

F32 = jnp.float32
BF16 = jnp.bfloat16

D_MODEL = 1024
GRID_W = 64
HEAD_DIM = 64
N_HEADS = 8
N_KV_HEADS = 2
GQA_GROUP = N_HEADS // N_KV_HEADS
Q_DIM = N_HEADS * HEAD_DIM
KV_DIM = N_KV_HEADS * HEAD_DIM
QK_DIM = Q_DIM + KV_DIM
HYENA_WIDTH = 512
HY_ORDER = 2
HY_IN = (HY_ORDER + 1) * HYENA_WIDTH
HY_GROUP_DIM = 64
HY_BANDS = 16
HY_POS_DIM = 1 + 2 * HY_BANDS
HY_FILTER_HIDDEN = 64
HY_OC = HY_ORDER * HYENA_WIDTH
IN_WIDTH = Q_DIM + 2 * KV_DIM + HY_IN
ROPE_THETA = 10000.0
ROPE_FREQS = HEAD_DIM // 4
N_GROUPS = 4
EXPERTS_PER_GROUP = 8
N_EXPERTS = N_GROUPS * EXPERTS_PER_GROUP
EXPERT_FF = D_MODEL // 4
DEPTH = 1
DEEPNORM_ALPHA = (2.0 * DEPTH) ** 0.25
EPS = 1e-6

LANES = 128
SUBLANES = 8
VMEM_LIMIT = 56 * 1024 * 1024
MOE_TILE = 512
GATHER_ROWS = 32
GROUP_ID_LANE = N_EXPERTS
PROJ_SUBTILES = 2
PROJ_MIN_SUBTILE = 256
ATTN_SUBTILES = 4
ATTN_LONG_SEQ = 1024
NEG_BIG = -1e30
LOG2_E = 1.4426950408889634


def _cparams(sem):
    return pltpu.CompilerParams(dimension_semantics=sem, vmem_limit_bytes=VMEM_LIMIT)


def _proj_subtiles(tm):
    return PROJ_SUBTILES if tm // PROJ_SUBTILES >= PROJ_MIN_SUBTILE else 1


def _split_bf16(a):
    hi = a.astype(BF16)
    lo = (a - hi.astype(F32)).astype(BF16)
    return hi, lo


def _dot(a, b):
    return jnp.dot(a, b, preferred_element_type=F32)


def _dot3(a, b):
    ah, al = _split_bf16(a)
    bh, bl = _split_bf16(b)
    return _dot(ah, bh) + _dot(al, bh) + _dot(ah, bl)


def _silu(x):
    return x / (1.0 + jnp.exp(-x))


def _layer_norm(y, g, b):
    mu = jnp.mean(y, axis=-1, keepdims=True)
    yc = y - mu
    var = jnp.mean(yc * yc, axis=-1, keepdims=True)
    return yc * lax.rsqrt(var + EPS) * g + b


def _adaln_kernel(c_ref, w_ref, b_ref, o_ref):
    o_ref[...] = _dot3(_silu(c_ref[...]), w_ref[...]) + b_ref[...]


def _adaln(cond, w_mod, b_mod):
    rows = cond.shape[0]
    n = w_mod.shape[1]
    tn = 1536
    return pl.pallas_call(
        _adaln_kernel,
        out_shape=jax.ShapeDtypeStruct((rows, n), F32),
        grid=(n // tn,),
        in_specs=[
            pl.BlockSpec((rows, D_MODEL), lambda j: (0, 0)),
            pl.BlockSpec((D_MODEL, tn), lambda j: (0, j)),
            pl.BlockSpec((1, tn), lambda j: (0, j)),
        ],
        out_specs=pl.BlockSpec((rows, tn), lambda j: (0, j)),
        compiler_params=_cparams(("arbitrary",)),
    )(cond, w_mod, b_mod.reshape(1, n))


def _rope_tables(seq):
    t = np.arange(seq)
    rows = (t // GRID_W).astype(np.float64)
    cols = (t % GRID_W).astype(np.float64)
    inv_freq = ROPE_THETA ** (-np.arange(ROPE_FREQS, dtype=np.float64) / ROPE_FREQS)
    d = np.arange(LANES) % HEAD_DIM
    axis = d // (2 * ROPE_FREQS)
    f = d % ROPE_FREQS
    pos = np.where(axis[None, :] == 0, rows[:, None], cols[:, None])
    ang = pos * inv_freq[f][None, :]
    first = (d % (2 * ROPE_FREQS)) < ROPE_FREQS
    cos = np.cos(ang)
    sin = np.where(first[None, :], -np.sin(ang), np.sin(ang))
    return jnp.asarray(cos, F32), jnp.asarray(sin, F32)


def _inproj_kernel(latent, x_ref, mod_ref, w_ref, gain_ref, bd_ref, *rest):
    if latent:
        cos_ref, sin_ref, q_ref, kt_ref, vx_ref, hy_ref = rest
    else:
        q_ref, kt_ref, vx_ref, hy_ref, v_ref, knat_ref = rest
    m = mod_ref[...]
    tm = x_ref.shape[0]
    n_sub = _proj_subtiles(tm)
    ts = tm // n_sub
    for t in range(n_sub):
        rows = slice(t * ts, (t + 1) * ts)
        h = x_ref[rows, :] * (1.0 + m[1:2]) + m[0:1]
        proj = _dot(h.astype(BF16), w_ref[...])
        qk = proj[:, :QK_DIM]
        ms = _dot((qk * qk).astype(BF16), bd_ref[...])
        qk = qk * lax.rsqrt(ms + EPS) * gain_ref[...]
        if not latent:
            knat_ref[rows, :] = qk[:, Q_DIM:]
        else:
            cos = cos_ref[rows, :]
            sin = sin_ref[rows, :]
            lane = lax.broadcasted_iota(jnp.int32, cos.shape, 1)
            first = (lane % (2 * ROPE_FREQS)) < ROPE_FREQS
            chunks = []
            for c in range(QK_DIM // LANES):
                xc = qk[:, c * LANES:(c + 1) * LANES]
                below = pltpu.roll(xc, ROPE_FREQS, axis=1)
                above = pltpu.roll(xc, LANES - ROPE_FREQS, axis=1)
                chunks.append(xc * cos + jnp.where(first, above, below) * sin)
            qk = jnp.concatenate(chunks, axis=1)
        qs = (qk[:, :Q_DIM] * (HEAD_DIM ** -0.5 * LOG2_E)).astype(BF16)
        for hd in range(N_HEADS):
            q_ref[hd, rows, :] = qs[:, hd * HEAD_DIM:(hd + 1) * HEAD_DIM]
        kt = qk[:, Q_DIM:].T
        for kh in range(N_KV_HEADS):
            kt_ref[kh, :, rows] = kt[kh * HEAD_DIM:(kh + 1) * HEAD_DIM].astype(BF16)
        v = proj[:, QK_DIM:QK_DIM + KV_DIM]
        low = lax.broadcasted_iota(jnp.int32, v.shape, 1) < HEAD_DIM
        vx_ref[0, rows, :] = jnp.where(low, v, 1.0).astype(BF16)
        vx_ref[1, rows, :] = jnp.where(low, pltpu.roll(v, HEAD_DIM, axis=1), 1.0).astype(BF16)
        if not latent:
            v_ref[rows, :] = v
        hy_ref[rows, :] = proj[:, QK_DIM + KV_DIM:]


def _inproj(x, mod, w_in, qk_gain, bd_qk, latent, tm):
    nb, seq, _ = x.shape
    grid = (nb, seq // tm)
    mod_map = (lambda b, i: (b, 0, 0)) if latent else (lambda b, i: (0, 0, 0))
    in_specs = [
        pl.BlockSpec((None, tm, D_MODEL), lambda b, i: (b, i, 0)),
        pl.BlockSpec((None, 6, D_MODEL), mod_map),
        pl.BlockSpec((D_MODEL, IN_WIDTH), lambda b, i: (0, 0)),
        pl.BlockSpec((1, QK_DIM), lambda b, i: (0, 0)),
        pl.BlockSpec((QK_DIM, QK_DIM), lambda b, i: (0, 0)),
    ]
    args = [x, mod, w_in, qk_gain, bd_qk]
    out_shape = [
        jax.ShapeDtypeStruct((nb, N_HEADS, seq, HEAD_DIM), BF16),
        jax.ShapeDtypeStruct((nb, N_KV_HEADS, HEAD_DIM, seq), BF16),
        jax.ShapeDtypeStruct((nb, N_KV_HEADS, seq, KV_DIM), BF16),
        jax.ShapeDtypeStruct((nb, seq, HY_IN), F32),
    ]
    out_specs = [
        pl.BlockSpec((None, N_HEADS, tm, HEAD_DIM), lambda b, i: (b, 0, i, 0)),
        pl.BlockSpec((None, N_KV_HEADS, HEAD_DIM, tm), lambda b, i: (b, 0, 0, i)),
        pl.BlockSpec((None, N_KV_HEADS, tm, KV_DIM), lambda b, i: (b, 0, i, 0)),
        pl.BlockSpec((None, tm, HY_IN), lambda b, i: (b, i, 0)),
    ]
    if latent:
        cos, sin = _rope_tables(seq)
        in_specs += [pl.BlockSpec((tm, LANES), lambda b, i: (i, 0))] * 2
        args += [cos, sin]
    else:
        out_shape += [jax.ShapeDtypeStruct((nb, seq, KV_DIM), F32)] * 2
        out_specs += [pl.BlockSpec((None, tm, KV_DIM), lambda b, i: (b, i, 0))] * 2
    return pl.pallas_call(
        functools.partial(_inproj_kernel, latent),
        out_shape=out_shape,
        grid=grid,
        in_specs=in_specs,
        out_specs=out_specs,
        compiler_params=_cparams(("parallel", "parallel")),
    )(*args)


def _attn_kernel(n_ctx, chunk, subtiles, q_ref, kt_ref, v_ref, gain_ref, *rest):
    if n_ctx:
        ckt_ref, cv_ref, o_ref = rest
    else:
        (o_ref,) = rest
    bb, heads, tq, _ = q_ref.shape
    seq = kt_ref.shape[-1]
    g = GQA_GROUP
    ts = tq // subtiles
    for b in range(bb):
        for k in range(heads // g):
            pieces = [(kt_ref[b, k, :, c * chunk:(c + 1) * chunk], v_ref[b, k, c * chunk:(c + 1) * chunk, :])
                      for c in range(seq // chunk)]
            if n_ctx:
                pieces.append((ckt_ref[b, k], cv_ref[b, k]))
            for t in range(subtiles):
                rows = slice(t * ts, (t + 1) * ts)
                qs = q_ref[b, k * g:(k + 1) * g, rows, :].reshape(g * ts, HEAD_DIM)
                m = acc = None
                for kt_c, v_c in pieces:
                    s = _dot(qs, kt_c)
                    row_max = jnp.max(s, axis=1, keepdims=True)
                    m_new = row_max if m is None else jnp.maximum(m, row_max)
                    pv = _dot(jnp.exp2(s - m_new).astype(BF16), v_c)
                    acc = pv if m is None else jnp.exp2(m - m_new) * acc + pv
                    m = m_new
                o = acc[:, :HEAD_DIM] / acc[:, HEAD_DIM:HEAD_DIM + 1]
                o = o * lax.rsqrt(jnp.mean(o * o, axis=1, keepdims=True) + EPS)
                for i in range(g):
                    cols = slice((k * g + i) * HEAD_DIM, (k * g + i + 1) * HEAD_DIM)
                    o_ref[b, rows, cols] = (o[i * ts:(i + 1) * ts] * gain_ref[:, cols]).astype(o_ref.dtype)


def _attention(q, kt, v, gain, ckt, cv):
    nb, _, seq, _ = q.shape
    n_ctx = 0 if ckt is None else ckt.shape[-1]
    chunk = _pick_tile(seq, 2048)
    if seq >= ATTN_LONG_SEQ:
        bb, kv, tq, subtiles = 1, 1, _pick_tile(seq, 512), ATTN_SUBTILES
    else:
        bb, kv, tq, subtiles = _pick_tile(nb, 4), N_KV_HEADS, seq, 1
    width = kv * GQA_GROUP * HEAD_DIM
    in_specs = [
        pl.BlockSpec((bb, kv * GQA_GROUP, tq, HEAD_DIM), lambda b, k, i: (b, k, i, 0)),
        pl.BlockSpec((bb, kv, HEAD_DIM, seq), lambda b, k, i: (b, k, 0, 0)),
        pl.BlockSpec((bb, kv, seq, KV_DIM), lambda b, k, i: (b, k, 0, 0)),
        pl.BlockSpec((1, width), lambda b, k, i: (0, k)),
    ]
    args = [q, kt, v, gain]
    if n_ctx:
        in_specs += [
            pl.BlockSpec((bb, kv, HEAD_DIM, n_ctx), lambda b, k, i: (b, k, 0, 0)),
            pl.BlockSpec((bb, kv, n_ctx, KV_DIM), lambda b, k, i: (b, k, 0, 0)),
        ]
        args += [ckt, cv]
    return pl.pallas_call(
        functools.partial(_attn_kernel, n_ctx, chunk, subtiles),
        out_shape=jax.ShapeDtypeStruct((nb, seq, Q_DIM), BF16),
        grid=(nb // bb, N_KV_HEADS // kv, seq // tq),
        in_specs=in_specs,
        out_specs=pl.BlockSpec((bb, tq, width), lambda b, k, i: (b, i, k)),
        compiler_params=_cparams(("parallel", "parallel", "parallel")),
    )(*args)


def _hyena_decimation(seq):
    return 8 if seq >= 2048 else 1


def _dft_tables(seq, r):
    n_sub = 2 * seq // r
    half = n_sub // 2
    k = np.arange(half, dtype=np.float64)[:, None]
    m = np.arange(n_sub, dtype=np.float64)[None, :]
    ang = 2.0 * np.pi * k * m / n_sub
    fwd = np.concatenate([np.cos(ang), -np.sin(ang)], axis=0)
    fwd[half] = np.cos(np.pi * m[0])
    inv = fwd.T.copy() * (2.0 / n_sub)
    inv[:, 0] *= 0.5
    inv[:, half] *= 0.5
    inv = inv[:seq // r]
    kk = np.arange(half, dtype=np.float64)[:, None] * np.ones((1, LANES))
    tw_r = np.cos(2.0 * np.pi * kk / n_sub)
    tw_i = -np.sin(2.0 * np.pi * kk / n_sub)
    return fwd, inv, tw_r, tw_i


def _filter_positions(seq, r):
    n_tot = 2 * seq
    n = (np.arange(n_tot // r)[None, :] * r + np.arange(r)[:, None]).reshape(-1)
    j = np.where(n < seq, n, n_tot - n)
    t = j.astype(np.float64) / seq
    bands = np.arange(1, HY_BANDS + 1, dtype=np.float64)
    ang = 2.0 * np.pi * t[:, None] * bands
    z = np.concatenate([t[:, None], np.sin(ang), np.cos(ang)], axis=-1)
    ones = np.ones((1, HY_FILTER_HIDDEN))
    sel_f = (n < seq).astype(np.float64)[:, None] * ones
    sel_b = (n > seq).astype(np.float64)[:, None] * ones
    return z, t[:, None] * np.ones((1, LANES)), sel_f, sel_b


def _filter_ffn_kernel(z_ref, self_ref, selb_ref, w1_ref, b1_ref, w2_ref, b2_ref, fr_ref, hf_ref, hb_ref):
    fr = fr_ref[...]
    h = jnp.sin(fr * (_dot3(z_ref[...], w1_ref[...]) + b1_ref[...]))
    h = jnp.sin(fr * (_dot3(h, w2_ref[...]) + b2_ref[...]))
    hf_ref[...] = h * self_ref[...]
    hb_ref[...] = h * selb_ref[...]


def _filter_spec_kernel(r, hf_ref, hb_ref, t_ref, w3f_ref, w3b_ref, dcf_ref, dcb_ref, fh_ref, ga_ref, gb_ref, gc_ref):
    t = t_ref[...]
    g = (_dot3(hf_ref[...], w3f_ref[...]) * jnp.exp(-t * jnp.abs(dcf_ref[...]))
         + _dot3(hb_ref[...], w3b_ref[...]) * jnp.exp(-t * jnp.abs(dcb_ref[...])))
    g = g * lax.rsqrt(jnp.sum(g * g, axis=0, keepdims=True) + EPS)
    n_sub = g.shape[0] // r
    half = n_sub // 2
    fh = fh_ref[...]
    for p in range(r):
        spec = _dot(fh, g[p * n_sub:(p + 1) * n_sub].astype(BF16))
        ga_ref[p] = spec[:half]
        gb_ref[p] = spec[half:] - spec[:half]
        gc_ref[p] = spec[half:] + spec[:half]


def _hyena_filters(seq, r, w1, b1, w2, b2, w3, freq, decay, fwd):
    n_tot = 2 * seq
    n_sub = n_tot // r
    half = n_sub // 2
    z, t, sel_f, sel_b = _filter_positions(seq, r)
    pad = (-HY_POS_DIM) % SUBLANES
    z = jnp.asarray(np.pad(z, ((0, 0), (0, pad))), F32)
    w1p = jnp.pad(w1, ((0, pad), (0, 0)))
    kin = HY_POS_DIM + pad
    hid = HY_FILTER_HIDDEN
    tr = min(n_tot, 512)
    rows = lambda width: pl.BlockSpec((tr, width), lambda i: (i, 0))
    full = lambda shape: pl.BlockSpec(shape, lambda j: (0,) * len(shape))
    hf, hb = pl.pallas_call(
        _filter_ffn_kernel,
        out_shape=[jax.ShapeDtypeStruct((n_tot, hid), F32)] * 2,
        grid=(n_tot // tr,),
        in_specs=[rows(kin), rows(hid), rows(hid), full((kin, hid)), full((1, hid)), full((hid, hid)),
                  full((1, hid)), full((1, hid))],
        out_specs=[rows(hid)] * 2,
        compiler_params=_cparams(("parallel",)),
    )(z, jnp.asarray(sel_f, F32), jnp.asarray(sel_b, F32), w1p, b1.reshape(1, hid), w2, b2.reshape(1, hid),
      freq.reshape(1, hid))
    ncb = HY_OC // LANES
    return pl.pallas_call(
        functools.partial(_filter_spec_kernel, r),
        out_shape=[jax.ShapeDtypeStruct((r, half, HY_OC), F32)] * 3,
        grid=(ncb,),
        in_specs=[
            full((n_tot, hid)), full((n_tot, hid)), full((n_tot, LANES)),
            pl.BlockSpec((hid, LANES), lambda j: (0, j)),
            pl.BlockSpec((hid, LANES), lambda j: (0, j + ncb)),
            pl.BlockSpec((1, LANES), lambda j: (0, j)),
            pl.BlockSpec((1, LANES), lambda j: (0, j + ncb)),
            full((2 * half, n_sub)),
        ],
        out_specs=[pl.BlockSpec((r, half, LANES), lambda j: (0, 0, j))] * 3,
        compiler_params=_cparams(("parallel",)),
    )(hf, hb, jnp.asarray(t, F32), w3, w3, decay.reshape(1, -1), decay.reshape(1, -1), fwd)


def _hyena_kernel(r, hy0_ref, hy1_ref, hy2_ref, sw_ref, sb_ref, skip_ref,
                  ga0_ref, gb0_ref, gc0_ref, ga1_ref, gb1_ref, gc1_ref, fwd_ref, inv_ref, twr_ref, twi_ref,
                  o_ref, z_ref, ph_ref, rhs_ref, x_ref):
    seq = hy0_ref.shape[0]
    m_len = seq // r
    half = fwd_ref.shape[0] // 2
    row = lax.broadcasted_iota(jnp.int32, (m_len, LANES), 0)
    hy_refs = (hy0_ref, hy1_ref, hy2_ref)

    def load_phases(part):
        for j in range(r):
            ph_ref[j] = hy_refs[part][pl.ds(j, m_len, stride=r), :]

    def short_conv(part, j):
        w = sw_ref[:, part * LANES:(part + 1) * LANES]
        b = sb_ref[:, part * LANES:(part + 1) * LANES]
        if j > 0:
            prev = ph_ref[j - 1]
        else:
            prev = jnp.where(row == 0, 0.0, pltpu.roll(ph_ref[r - 1], 1, axis=0))
        if j < r - 1:
            nxt = ph_ref[j + 1]
        else:
            nxt = jnp.where(row == m_len - 1, 0.0, pltpu.roll(ph_ref[0], m_len - 1, axis=0))
        return prev * w[0:1] + ph_ref[j] * w[1:2] + nxt * w[2:3] + b

    load_phases(0)
    for j in range(r):
        z_ref[j] = short_conv(0, j)

    for o, (ga_ref, gb_ref, gc_ref) in enumerate(((ga0_ref, gb0_ref, gc0_ref), (ga1_ref, gb1_ref, gc1_ref))):
        for j in range(r):
            rhs_ref[:, j * LANES:(j + 1) * LANES] = z_ref[j].astype(BF16)
        x_ref[...] = _dot(fwd_ref[...], rhs_ref[...])
        dc = [x_ref[0:1, j * LANES:(j + 1) * LANES] for j in range(r)]
        ny = [x_ref[half:half + 1, j * LANES:(j + 1) * LANES] for j in range(r)]
        y_dc, y_ny = [], []
        for j in range(r):
            a = jnp.zeros((1, LANES), F32)
            c = jnp.zeros((1, LANES), F32)
            for jp in range(r):
                p = (j - jp) % r
                a = a + ga_ref[p, 0:1, :] * dc[jp]
                t = (ga_ref[p, 0:1, :] + gb_ref[p, 0:1, :]) * ny[jp]
                c = c + t if jp <= j else c - t
            y_dc.append(a)
            y_ny.append(c)

        def mix(i, carry):
            r0 = pl.multiple_of(i * SUBLANES, SUBLANES)
            rows_re = pl.ds(r0, SUBLANES)
            rows_im = pl.ds(half + r0, SUBLANES)
            xr = [x_ref[rows_re, j * LANES:(j + 1) * LANES] for j in range(r)]
            xi = [x_ref[rows_im, j * LANES:(j + 1) * LANES] for j in range(r)]
            xs = [a + b for a, b in zip(xr, xi)]
            wr = twr_ref[rows_re, :]
            wi = twi_ref[rows_re, :]
            for j in range(r):
                acc = {}
                for jp in range(r):
                    p = (j - jp) % r
                    k1 = ga_ref[p, rows_re, :] * xs[jp]
                    k2 = gb_ref[p, rows_re, :] * xr[jp]
                    k3 = gc_ref[p, rows_re, :] * xi[jp]
                    side = jp <= j
                    ks = (k1, k2, k3)
                    acc[side] = ks if side not in acc else tuple(a + b for a, b in zip(acc[side], ks))
                pr = acc[True][0] - acc[True][2]
                pi = acc[True][0] + acc[True][1]
                if False in acc:
                    qr = acc[False][0] - acc[False][2]
                    qi = acc[False][0] + acc[False][1]
                    pr = pr + wr * qr - wi * qi
                    pi = pi + wr * qi + wi * qr
                x_ref[rows_re, j * LANES:(j + 1) * LANES] = pr
                x_ref[rows_im, j * LANES:(j + 1) * LANES] = pi
            return carry

        lax.fori_loop(0, half // SUBLANES, mix, 0)
        for j in range(r):
            x_ref[0:1, j * LANES:(j + 1) * LANES] = y_dc[j]
            x_ref[half:half + 1, j * LANES:(j + 1) * LANES] = y_ny[j]
        y = _dot(inv_ref[...], x_ref[...].astype(BF16))
        sk = skip_ref[o:o + 1, :]
        load_phases(o + 1)
        for j in range(r):
            z_ref[j] = short_conv(o + 1, j) * (y[:, j * LANES:(j + 1) * LANES] + z_ref[j] * sk)
    for j in range(r):
        o_ref[pl.ds(j, m_len, stride=r), :] = z_ref[j]


def _hyena_direct_kernel(hy_ref, sw_ref, sb_ref, skip_ref, ga_ref, gb_ref, fwd_ref, inv_ref, o_ref):
    bb, seq, _ = hy_ref.shape
    half = fwd_ref.shape[0] // 2
    w = HYENA_WIDTH
    row = lax.broadcasted_iota(jnp.int32, (seq, w), 0)
    is_dc = lax.broadcasted_iota(jnp.int32, (half, w), 0) == 0

    def short_conv(b, part):
        cols = slice(part * w, (part + 1) * w)
        x = hy_ref[b, :, cols]
        prev = jnp.where(row == 0, 0.0, pltpu.roll(x, 1, axis=0))
        nxt = jnp.where(row == seq - 1, 0.0, pltpu.roll(x, seq - 1, axis=0))
        return prev * sw_ref[0:1, cols] + x * sw_ref[1:2, cols] + nxt * sw_ref[2:3, cols] + sb_ref[:, cols]

    for b in range(bb):
        z = short_conv(b, 0)
        for o in range(HY_ORDER):
            cols = slice(o * w, (o + 1) * w)
            x = _dot(fwd_ref[...], z.astype(BF16))
            xr, xi = x[:half], x[half:]
            gr = ga_ref[0, :, cols]
            gi = gb_ref[0, :, cols] + gr
            vr = gr * xr - jnp.where(is_dc, 0.0, gi * xi)
            vi = jnp.where(is_dc, gi * xi, gr * xi + gi * xr)
            y = _dot(inv_ref[...], jnp.concatenate([vr, vi], axis=0).astype(BF16))
            z = short_conv(b, o + 1) * (y + z * skip_ref[o:o + 1, :])
        o_ref[b] = z


def _hyena_direct(hy, short_w, short_b, skip, ga, gb, fwd, inv):
    nb, seq, _ = hy.shape
    bb = _pick_tile(nb, 4)
    const = lambda a: pl.BlockSpec(a.shape, lambda i: (0,) * a.ndim)
    short_b = short_b.reshape(1, -1)
    return pl.pallas_call(
        _hyena_direct_kernel,
        out_shape=jax.ShapeDtypeStruct((nb, seq, HYENA_WIDTH), F32),
        grid=(nb // bb,),
        in_specs=[pl.BlockSpec((bb, seq, HY_IN), lambda i: (i, 0, 0)), const(short_w), const(short_b), const(skip),
                  const(ga), const(gb), const(fwd), const(inv)],
        out_specs=pl.BlockSpec((bb, seq, HYENA_WIDTH), lambda i: (i, 0, 0)),
        compiler_params=_cparams(("parallel",)),
    )(hy, short_w, short_b, skip, ga, gb, fwd, inv)


def _hyena(hy, short_w, short_b, skip, spectra, fwd, inv, tw_r, tw_i, r):
    ga, gb, gc = spectra
    if r == 1:
        return _hyena_direct(hy, short_w, short_b, skip, ga, gb, fwd, inv)
    nb, seq, _ = hy.shape
    m_len = seq // r
    n_half2 = fwd.shape[0]
    half = n_half2 // 2
    ncb = HYENA_WIDTH // LANES
    parts = HY_ORDER + 1
    once = pl.Buffered(1)
    hy_spec = lambda part: pl.BlockSpec((None, seq, LANES), lambda c, b: (b, 0, part * ncb + c))
    g_spec = lambda o: pl.BlockSpec((r, half, LANES), lambda c, b: (0, 0, o * ncb + c), pipeline_mode=once)
    const = lambda shape: pl.BlockSpec(shape, lambda c, b: (0,) * len(shape), pipeline_mode=once)
    sw = short_w.reshape(3, parts, ncb, LANES).transpose(2, 0, 1, 3).reshape(ncb, 3, parts * LANES)
    sb = short_b.reshape(1, parts, ncb, LANES).transpose(2, 0, 1, 3).reshape(ncb, 1, parts * LANES)
    return pl.pallas_call(
        functools.partial(_hyena_kernel, r),
        out_shape=jax.ShapeDtypeStruct((nb, seq, HYENA_WIDTH), F32),
        grid=(ncb, nb),
        in_specs=[
            hy_spec(0), hy_spec(1), hy_spec(2),
            pl.BlockSpec((None, 3, parts * LANES), lambda c, b: (c, 0, 0)),
            pl.BlockSpec((None, 1, parts * LANES), lambda c, b: (c, 0, 0)),
            pl.BlockSpec((HY_ORDER, LANES), lambda c, b: (0, c)),
            g_spec(0), g_spec(0), g_spec(0), g_spec(1), g_spec(1), g_spec(1),
            const((n_half2, m_len)), const((m_len, n_half2)),
            const((half, LANES)), const((half, LANES)),
        ],
        out_specs=pl.BlockSpec((None, seq, LANES), lambda c, b: (b, 0, c)),
        scratch_shapes=[
            pltpu.VMEM((r, m_len, LANES), F32),
            pltpu.VMEM((r, m_len, LANES), F32),
            pltpu.VMEM((m_len, r * LANES), BF16),
            pltpu.VMEM((n_half2, r * LANES), F32),
        ],
        compiler_params=_cparams(("parallel", "parallel")),
    )(hy, hy, hy, sw, sb, skip, ga, gb, gc, ga, gb, gc, fwd, inv, tw_r, tw_i)


def _route(logits):
    lane = lax.broadcasted_iota(jnp.int32, logits.shape, 1).astype(F32)
    big = jnp.float32(1e9)
    is_grp = (lane >= N_EXPERTS) & (lane < N_EXPERTS + N_GROUPS)
    gl = jnp.where(is_grp, logits, NEG_BIG)
    gmax = jnp.max(gl, axis=1, keepdims=True)
    gidx = jnp.min(jnp.where(gl == gmax, lane, big), axis=1, keepdims=True) - N_EXPERTS
    den = jnp.sum(jnp.where(is_grp, jnp.exp(gl - gmax), 0.0), axis=1, keepdims=True)
    pg_top = 1.0 / den
    lo = gidx * EXPERTS_PER_GROUP
    sel = jnp.where((lane >= lo) & (lane < lo + EXPERTS_PER_GROUP), logits, NEG_BIG)
    m1 = jnp.max(sel, axis=1, keepdims=True)
    i1 = jnp.min(jnp.where(sel == m1, lane, big), axis=1, keepdims=True)
    sel2 = jnp.where(lane == i1, NEG_BIG, sel)
    m2 = jnp.max(sel2, axis=1, keepdims=True)
    i2 = jnp.min(jnp.where(sel2 == m2, lane, big), axis=1, keepdims=True)
    e2 = jnp.exp(m2 - m1)
    w1 = pg_top / (1.0 + e2)
    w2 = pg_top * e2 / (1.0 + e2)
    comb = jnp.where(lane == i1, w1, 0.0) + jnp.where(lane == i2, w2, 0.0)
    return comb + jnp.where(lane == GROUP_ID_LANE, gidx, 0.0)


def _outproj_kernel(attn_ref, hyz_ref, x_ref, mod_ref, wo_ref, hg_ref, bd_ref, lng_ref, lnb_ref,
                    rwh_ref, rwl_ref, rb_ref, x1_ref, tok_ref):
    m = mod_ref[...]
    tm = x_ref.shape[0]
    n_sub = _proj_subtiles(tm)
    ts = tm // n_sub
    half = wo_ref.shape[0] // 2
    for t in range(n_sub):
        rows = slice(t * ts, (t + 1) * ts)
        z = hyz_ref[rows, :]
        ms = _dot((z * z).astype(BF16), bd_ref[...])
        zn = (z * lax.rsqrt(ms + EPS) * hg_ref[...]).astype(BF16)
        mix = _dot(attn_ref[rows, :], wo_ref[:half, :]) + _dot(zn, wo_ref[half:, :])
        x1 = _layer_norm(DEEPNORM_ALPHA * x_ref[rows, :] + m[2:3] * mix, lng_ref[...], lnb_ref[...])
        h2 = x1 * (1.0 + m[4:5]) + m[3:4]
        x1_ref[rows, :] = x1
        tok_ref[rows, :D_MODEL] = h2
        hh, hl = _split_bf16(h2)
        logits = _dot(hh, rwh_ref[...]) + _dot(hl, rwh_ref[...]) + _dot(hh, rwl_ref[...]) + rb_ref[...]
        tok_ref[rows, D_MODEL:] = _route(logits)


def _outproj(attn, hyz, x, mod, w_out, hy_gain, bd_hy, ln_g, ln_b, rw_hi, rw_lo, rb, latent, tm):
    nb, seq, _ = x.shape
    mod_map = (lambda b, i: (b, 0, 0)) if latent else (lambda b, i: (0, 0, 0))
    const = lambda shape: pl.BlockSpec(shape, lambda b, i: (0,) * len(shape))
    tok = lambda width: pl.BlockSpec((None, tm, width), lambda b, i: (b, i, 0))
    return pl.pallas_call(
        _outproj_kernel,
        out_shape=[
            jax.ShapeDtypeStruct((nb, seq, D_MODEL), F32),
            jax.ShapeDtypeStruct((nb, seq, D_MODEL + LANES), F32),
        ],
        grid=(nb, seq // tm),
        in_specs=[
            tok(Q_DIM), tok(HYENA_WIDTH), tok(D_MODEL),
            pl.BlockSpec((None, 6, D_MODEL), mod_map),
            const((D_MODEL, D_MODEL)), const((1, HYENA_WIDTH)), const((HYENA_WIDTH, HYENA_WIDTH)),
            const((1, D_MODEL)), const((1, D_MODEL)),
            const((D_MODEL, LANES)), const((D_MODEL, LANES)), const((1, LANES)),
        ],
        out_specs=[tok(D_MODEL), tok(D_MODEL + LANES)],
        compiler_params=_cparams(("parallel", "parallel")),
    )(attn, hyz, x, mod, w_out, hy_gain, bd_hy, ln_g, ln_b, rw_hi, rw_lo, rb)


def _gather_rows(table, idx):
    n, d = idx.shape[0], table.shape[1]
    info = plsc.get_sparse_core_info()
    workers = info.num_cores * info.num_subcores
    per_worker = n // workers
    assert per_worker * workers == n and per_worker % GATHER_ROWS == 0
    mesh = plsc.VectorSubcoreMesh(core_axis_name="c", subcore_axis_name="s")

    @functools.partial(
        pl.kernel, mesh=mesh,
        out_type=jax.ShapeDtypeStruct((n, d), table.dtype),
        scratch_types=[pltpu.VMEM((GATHER_ROWS,), jnp.int32), pltpu.VMEM((GATHER_ROWS, d), table.dtype),
                       pltpu.SemaphoreType.DMA],
    )
    def gather(table_hbm, idx_hbm, out_hbm, idx_v, rows_v, sem):
        wid = lax.axis_index("s") * info.num_cores + lax.axis_index("c")
        base = wid * per_worker

        @pl.loop(0, per_worker // GATHER_ROWS)
        def _(j):
            off = base + j * GATHER_ROWS
            pltpu.sync_copy(idx_hbm.at[pl.ds(off, GATHER_ROWS)], idx_v)
            pltpu.async_copy(table_hbm.at[idx_v], rows_v, sem).wait()
            pltpu.sync_copy(rows_v, out_hbm.at[pl.ds(off, GATHER_ROWS)])

    return gather(table, idx)


def _sort_plan(record, tile):
    tokens = record.shape[0]
    gid = record[:, GROUP_ID_LANE].astype(jnp.int32)
    onehot = (gid[:, None] == jnp.arange(N_GROUPS, dtype=jnp.int32)[None, :]).astype(jnp.int32)
    csum = jnp.cumsum(onehot, axis=0)
    counts = csum[-1]
    rank = jnp.sum(csum * onehot, axis=1) - 1
    padded = (counts + tile - 1) // tile * tile
    ends = jnp.cumsum(padded)
    starts = ends - padded
    pos = jnp.sum(starts[None, :] * onehot, axis=1) + rank
    n_pad = tokens + N_GROUPS * tile
    src = jnp.zeros((n_pad,), jnp.int32).at[pos].set(jnp.arange(tokens, dtype=jnp.int32))
    tile_start = jnp.arange(n_pad // tile, dtype=jnp.int32) * tile
    tile_group = jnp.minimum(jnp.sum((tile_start[:, None] >= ends[None, :]).astype(jnp.int32), axis=1), N_GROUPS - 1)
    tile_valid = (tile_start < ends[-1]).astype(jnp.int32)
    return pos, src, tile_group, tile_valid


def _moe_kernel(tg_ref, tv_ref, x_ref, wg_ref, wu_ref, wd_ref, o_ref):
    i = pl.program_id(0)

    @pl.when(tv_ref[i] > 0)
    def _():
        first = tg_ref[i] * EXPERTS_PER_GROUP
        x = x_ref[:, :D_MODEL].astype(BF16)
        rec = x_ref[:, D_MODEL:]
        lane = lax.broadcasted_iota(jnp.int32, rec.shape, 1)
        y = None
        for e in range(EXPERTS_PER_GROUP):
            c = jnp.sum(jnp.where(lane == first + e, rec, 0.0), axis=1, keepdims=True)
            h = _silu(_dot(x, wg_ref[e])) * _dot(x, wu_ref[e]) * c
            part = _dot(h.astype(BF16), wd_ref[e])
            y = part if y is None else y + part
        o_ref[...] = y

    @pl.when(tv_ref[i] == 0)
    def _():
        o_ref[...] = jnp.zeros_like(o_ref)


def _moe(sorted_tok, tile_group, tile_valid, wg, wu, wd):
    rows = sorted_tok.shape[0]
    tile = MOE_TILE
    grp = lambda shape: pl.BlockSpec(shape, lambda i, tg, tv: (tg[i], 0, 0))
    return pl.pallas_call(
        _moe_kernel,
        out_shape=jax.ShapeDtypeStruct((rows, D_MODEL), F32),
        grid_spec=pltpu.PrefetchScalarGridSpec(
            num_scalar_prefetch=2,
            grid=(rows // tile,),
            in_specs=[
                pl.BlockSpec((tile, D_MODEL + LANES), lambda i, tg, tv: (i, 0)),
                grp((EXPERTS_PER_GROUP, D_MODEL, EXPERT_FF)), grp((EXPERTS_PER_GROUP, D_MODEL, EXPERT_FF)),
                grp((EXPERTS_PER_GROUP, EXPERT_FF, D_MODEL)),
            ],
            out_specs=pl.BlockSpec((tile, D_MODEL), lambda i, tg, tv: (i, 0)),
        ),
        compiler_params=_cparams(("arbitrary",)),
    )(tile_group, tile_valid, sorted_tok, wg, wu, wd)


def _final_norm_kernel(x1_ref, y_ref, mod_ref, lng_ref, lnb_ref, o_ref):
    o_ref[...] = _layer_norm(DEEPNORM_ALPHA * x1_ref[...] + mod_ref[5:6, :] * y_ref[...], lng_ref[...], lnb_ref[...])


def _final_norm(x1, y, mod, ln_g, ln_b, latent, tm):
    nb, seq, _ = x1.shape
    mod_map = (lambda b, i: (b, 0, 0)) if latent else (lambda b, i: (0, 0, 0))
    tok = pl.BlockSpec((None, tm, D_MODEL), lambda b, i: (b, i, 0))
    const = pl.BlockSpec((1, D_MODEL), lambda b, i: (0, 0))
    return pl.pallas_call(
        _final_norm_kernel,
        out_shape=jax.ShapeDtypeStruct((nb, seq, D_MODEL), F32),
        grid=(nb, seq // tm),
        in_specs=[tok, tok, pl.BlockSpec((None, 6, D_MODEL), mod_map), const, const],
        out_specs=tok,
        compiler_params=_cparams(("parallel", "parallel")),
    )(x1, y, mod, ln_g, ln_b)


def _block_diag_mean(width, group):
    idx = np.arange(width) // group
    return jnp.asarray((idx[:, None] == idx[None, :]).astype(np.float32) / group, BF16)


def _pick_tile(n, target):
    t = min(n, target)
    while n % t:
        t //= 2
    return t


def _trunk_layer(x, mod, p, latent, ctx_k=None, ctx_v=None):
    nb, seq, _ = x.shape
    tm = _pick_tile(seq, 1024)
    q, kt, vx, hy, *kv_new = _inproj(x, mod, p['w_in'], p['qk_gain'], p['bd_qk'], latent, tm)
    if latent:
        ckt = jnp.transpose(ctx_k, (0, 2, 3, 1)).astype(BF16)
        cv = jnp.transpose(ctx_v, (0, 2, 1, 3))
        cvx = jnp.concatenate([cv, jnp.ones_like(cv)], axis=-1).astype(BF16)
    else:
        ckt = cvx = None
    attn = _attention(q, kt, vx, p['attn_gain'], ckt, cvx)

    r = _hyena_decimation(seq)
    fwd, inv, tw_r, tw_i = _dft_tables(seq, r)
    fwd = jnp.asarray(fwd, F32).astype(BF16)
    spectra = _hyena_filters(seq, r, p['hy_f_w1'], p['hy_f_b1'], p['hy_f_w2'], p['hy_f_b2'], p['hy_f_w3'],
                              p['hy_freq'], p['hy_decay'], fwd)
    hyz = _hyena(hy, p['hy_short_w'], p['hy_short_b'], p['hy_skip'], spectra,
                 fwd[:, :seq // r], jnp.asarray(inv, F32).astype(BF16), jnp.asarray(tw_r, F32),
                 jnp.asarray(tw_i, F32), r)

    x1, tok = _outproj(attn, hyz, x, mod, p['w_out'], p['hy_gain'], p['bd_hy'], p['ln1_g'], p['ln1_b'],
                       p['rw_hi'], p['rw_lo'], p['rb'], latent, tm)
    tok = tok.reshape(nb * seq, D_MODEL + LANES)
    pos, src, tile_group, tile_valid = _sort_plan(tok[:, D_MODEL:], MOE_TILE)
    y_sorted = _moe(_gather_rows(tok, src), tile_group, tile_valid, p['wg'], p['wu'], p['wd'])
    y = _gather_rows(y_sorted, pos).reshape(nb, seq, D_MODEL)
    return _final_norm(x1, y, mod, p['ln2_g'], p['ln2_b'], latent, tm), kv_new


def _prepare(w_in, q_gain, k_gain, attn_out_gain, hy_short_w, hy_short_b, hy_f_w1, hy_f_b1, hy_f_w2, hy_f_b2,
             hy_f_w3, hy_freq, hy_decay, hy_skip, hy_out_gain, w_out, ln1_g, ln1_b, router_grp_w, router_grp_b,
             router_exp_w, router_exp_b, exp_w_gate, exp_w_up, exp_w_down, ln2_g, ln2_b, l):
    row = lambda a: a.reshape(1, -1)
    rw = jnp.concatenate([router_exp_w[l], router_grp_w[l]], axis=1)
    rw = jnp.pad(rw, ((0, 0), (0, LANES - rw.shape[1])))
    rb = jnp.concatenate([router_exp_b[l], router_grp_b[l]])
    rb = jnp.pad(rb, (0, LANES - rb.shape[0]))
    rw_hi, rw_lo = _split_bf16(rw)
    return {
        'w_in': w_in[l].astype(BF16),
        'qk_gain': row(jnp.concatenate([jnp.tile(q_gain[l], N_HEADS), jnp.tile(k_gain[l], N_KV_HEADS)])),
        'bd_qk': _block_diag_mean(QK_DIM, HEAD_DIM),
        'attn_gain': row(attn_out_gain[l]),
        'hy_short_w': hy_short_w[l], 'hy_short_b': hy_short_b[l],
        'hy_f_w1': hy_f_w1[l], 'hy_f_b1': hy_f_b1[l], 'hy_f_w2': hy_f_w2[l], 'hy_f_b2': hy_f_b2[l],
        'hy_f_w3': hy_f_w3[l], 'hy_freq': hy_freq[l], 'hy_decay': hy_decay[l], 'hy_skip': hy_skip[l],
        'hy_gain': row(hy_out_gain[l]),
        'bd_hy': _block_diag_mean(HYENA_WIDTH, HY_GROUP_DIM),
        'w_out': w_out[l].astype(BF16),
        'ln1_g': row(ln1_g[l]), 'ln1_b': row(ln1_b[l]),
        'rw_hi': rw_hi, 'rw_lo': rw_lo, 'rb': row(rb),
        'wg': exp_w_gate[l].astype(BF16), 'wu': exp_w_up[l].astype(BF16), 'wd': exp_w_down[l].astype(BF16),
        'ln2_g': row(ln2_g[l]), 'ln2_b': row(ln2_b[l]),
    }


def kernel(x_prompt, x_sample, c, cache_k, cache_v, c_ctx, w_mod, b_mod, w_in, q_gain, k_gain, attn_out_gain, hy_short_w, hy_short_b, hy_f_w1, hy_f_b1, hy_f_w2, hy_f_b2, hy_f_w3, hy_freq, hy_decay, hy_skip, hy_out_gain, w_out, ln1_g, ln1_b, router_grp_w, router_grp_b, router_exp_w, router_exp_b, exp_w_gate, exp_w_up, exp_w_down, ln2_g, ln2_b):
    depth = w_mod.shape[0]
    n_lat = c.shape[0]
    cond = jnp.concatenate([c, c_ctx[None, :]], axis=0)
    rows = -(-cond.shape[0] // SUBLANES) * SUBLANES
    cond = jnp.pad(cond, ((0, rows - cond.shape[0]), (0, 0)))
    y_prompt, y_sample = x_prompt, x_sample
    ks_new, vs_new = [], []
    for l in range(depth):
        p = _prepare(w_in, q_gain, k_gain, attn_out_gain, hy_short_w, hy_short_b, hy_f_w1, hy_f_b1, hy_f_w2,
                     hy_f_b2, hy_f_w3, hy_freq, hy_decay, hy_skip, hy_out_gain, w_out, ln1_g, ln1_b,
                     router_grp_w, router_grp_b, router_exp_w, router_exp_b, exp_w_gate, exp_w_up, exp_w_down,
                     ln2_g, ln2_b, l)
        mod = _adaln(cond, w_mod[l], b_mod[l])
        mod_lat = mod[:n_lat].reshape(n_lat, 6, D_MODEL)
        mod_ctx = mod[n_lat:n_lat + 1].reshape(1, 6, D_MODEL)
        y_prompt, (v_new, k_new) = _trunk_layer(y_prompt, mod_ctx, p, False)
        ks_new.append(k_new.reshape(k_new.shape[:2] + (N_KV_HEADS, HEAD_DIM)))
        vs_new.append(v_new.reshape(v_new.shape[:2] + (N_KV_HEADS, HEAD_DIM)))
        y_sample, _ = _trunk_layer(y_sample, mod_lat, p, True, cache_k[:, l], cache_v[:, l])
    return (y_prompt, y_sample, jnp.stack(ks_new, axis=1), jnp.stack(vs_new, axis=1))
```

```python
import functools

import numpy as np
import jax
import jax.numpy as jnp
from jax import lax
from jax.experimental import pallas as pl
from jax.experimental.pallas import tpu as pltpu
from jax.experimental.pallas import tpu_sc as plsc

F32 = jnp.float32
BF16 = jnp.bfloat16

D_MODEL = 1024
GRID_W = 64
HEAD_DIM = 64
N_HEADS = 8
N_KV_HEADS = 2
GQA_GROUP = N_HEADS // N_KV_HEADS
Q_DIM = N_HEADS * HEAD_DIM
KV_DIM = N_KV_HEADS * HEAD_DIM
QK_DIM = Q_DIM + KV_DIM
HYENA_WIDTH = 512
HY_ORDER = 2
HY_IN = (HY_ORDER + 1) * HYENA_WIDTH
HY_GROUP_DIM = 64
HY_BANDS = 16
HY_POS_DIM = 1 + 2 * HY_BANDS
HY_FILTER_HIDDEN = 64
HY_OC = HY_ORDER * HYENA_WIDTH
IN_WIDTH = Q_DIM + 2 * KV_DIM + HY_IN
ROPE_THETA = 10000.0
ROPE_FREQS = HEAD_DIM // 4
N_GROUPS = 4
EXPERTS_PER_GROUP = 8
N_EXPERTS = N_GROUPS * EXPERTS_PER_GROUP
EXPERT_FF = D_MODEL // 4
DEPTH = 1
DEEPNORM_ALPHA = (2.0 * DEPTH) ** 0.25
EPS = 1e-6

LANES = 128
SUBLANES = 8
VMEM_LIMIT = 56 * 1024 * 1024
MOE_TILE = 512
GATHER_ROWS = 32
GROUP_ID_LANE = N_EXPERTS
PROJ_SUBTILES = 2
PROJ_MIN_SUBTILE = 256
ATTN_SUBTILES = 4
ATTN_LONG_SEQ = 1024
NEG_BIG = -1e30
LOG2_E = 1.4426950408889634


def _cparams(sem):
    return pltpu.CompilerParams(dimension_semantics=sem, vmem_limit_bytes=VMEM_LIMIT)


def _proj_subtiles(tm):
    return PROJ_SUBTILES if tm // PROJ_SUBTILES >= PROJ_MIN_SUBTILE else 1


def _split_bf16(a):
    hi = a.astype(BF16)
    lo = (a - hi.astype(F32)).astype(BF16)
    return hi, lo


def _dot(a, b):
    return jnp.dot(a, b, preferred_element_type=F32)


def _dot3(a, b):
    ah, al = _split_bf16(a)
    bh, bl = _split_bf16(b)
    return _dot(ah, bh) + _dot(al, bh) + _dot(ah, bl)


def _silu(x):
    return x / (1.0 + jnp.exp(-x))


def _layer_norm(y, g, b):
    mu = jnp.mean(y, axis=-1, keepdims=True)
    yc = y - mu
    var = jnp.mean(yc * yc, axis=-1, keepdims=True)
    return yc * lax.rsqrt(var + EPS) * g + b


def _adaln_kernel(c_ref, w_ref, b_ref, o_ref):
    o_ref[...] = _dot3(_silu(c_ref[...]), w_ref[...]) + b_ref[...]


def _adaln(cond, w_mod, b_mod):
    rows = cond.shape[0]
    n = w_mod.shape[1]
    tn = 1536
    return pl.pallas_call(
        _adaln_kernel,
        out_shape=jax.ShapeDtypeStruct((rows, n), F32),
        grid=(n // tn,),
        in_specs=[
            pl.BlockSpec((rows, D_MODEL), lambda j: (0, 0)),
            pl.BlockSpec((D_MODEL, tn), lambda j: (0, j)),
            pl.BlockSpec((1, tn), lambda j: (0, j)),
        ],
        out_specs=pl.BlockSpec((rows, tn), lambda j: (0, j)),
        compiler_params=_cparams(("arbitrary",)),
    )(cond, w_mod, b_mod.reshape(1, n))


def _rope_tables(seq):
    t = np.arange(seq)
    rows = (t // GRID_W).astype(np.float64)
    cols = (t % GRID_W).astype(np.float64)
    inv_freq = ROPE_THETA ** (-np.arange(ROPE_FREQS, dtype=np.float64) / ROPE_FREQS)
    d = np.arange(LANES) % HEAD_DIM
    axis = d // (2 * ROPE_FREQS)
    f = d % ROPE_FREQS
    pos = np.where(axis[None, :] == 0, rows[:, None], cols[:, None])
    ang = pos * inv_freq[f][None, :]
    first = (d % (2 * ROPE_FREQS)) < ROPE_FREQS
    cos = np.cos(ang)
    sin = np.where(first[None, :], -np.sin(ang), np.sin(ang))
    return jnp.asarray(cos, F32), jnp.asarray(sin, F32)


def _inproj_kernel(latent, x_ref, mod_ref, w_ref, gain_ref, bd_ref, *rest):
    if latent:
        cos_ref, sin_ref, q_ref, kt_ref, vx_ref, hy_ref = rest
    else:
        q_ref, kt_ref, vx_ref, hy_ref, v_ref, knat_ref = rest
    m = mod_ref[...]
    tm = x_ref.shape[0]
    n_sub = _proj_subtiles(tm)
    ts = tm // n_sub
    for t in range(n_sub):
        rows = slice(t * ts, (t + 1) * ts)
        h = x_ref[rows, :] * (1.0 + m[1:2]) + m[0:1]
        proj = _dot(h.astype(BF16), w_ref[...])
        qk = proj[:, :QK_DIM]
        ms = _dot((qk * qk).astype(BF16), bd_ref[...])
        qk = qk * lax.rsqrt(ms + EPS) * gain_ref[...]
        if not latent:
            knat_ref[rows, :] = qk[:, Q_DIM:]
        else:
            cos = cos_ref[rows, :]
            sin = sin_ref[rows, :]
            lane = lax.broadcasted_iota(jnp.int32, cos.shape, 1)
            first = (lane % (2 * ROPE_FREQS)) < ROPE_FREQS
            chunks = []
            for c in range(QK_DIM // LANES):
                xc = qk[:, c * LANES:(c + 1) * LANES]
                below = pltpu.roll(xc, ROPE_FREQS, axis=1)
                above = pltpu.roll(xc, LANES - ROPE_FREQS, axis=1)
                chunks.append(xc * cos + jnp.where(first, above, below) * sin)
            qk = jnp.concatenate(chunks, axis=1)
        qs = (qk[:, :Q_DIM] * (HEAD_DIM ** -0.5 * LOG2_E)).astype(BF16)
        for hd in range(N_HEADS):
            q_ref[hd, rows, :] = qs[:, hd * HEAD_DIM:(hd + 1) * HEAD_DIM]
        kt = qk[:, Q_DIM:].T
        for kh in range(N_KV_HEADS):
            kt_ref[kh, :, rows] = kt[kh * HEAD_DIM:(kh + 1) * HEAD_DIM].astype(BF16)
        v = proj[:, QK_DIM:QK_DIM + KV_DIM]
        low = lax.broadcasted_iota(jnp.int32, v.shape, 1) < HEAD_DIM
        vx_ref[0, rows, :] = jnp.where(low, v, 1.0).astype(BF16)
        vx_ref[1, rows, :] = jnp.where(low, pltpu.roll(v, HEAD_DIM, axis=1), 1.0).astype(BF16)
        if not latent:
            v_ref[rows, :] = v
        hy_ref[rows, :] = proj[:, QK_DIM + KV_DIM:]


def _inproj(x, mod, w_in, qk_gain, bd_qk, latent, tm):
    nb, seq, _ = x.shape
    grid = (nb, seq // tm)
    mod_map = (lambda b, i: (b, 0, 0)) if latent else (lambda b, i: (0, 0, 0))
    in_specs = [
        pl.BlockSpec((None, tm, D_MODEL), lambda b, i: (b, i, 0)),
        pl.BlockSpec((None, 6, D_MODEL), mod_map),
        pl.BlockSpec((D_MODEL, IN_WIDTH), lambda b, i: (0, 0)),
        pl.BlockSpec((1, QK_DIM), lambda b, i: (0, 0)),
        pl.BlockSpec((QK_DIM, QK_DIM), lambda b, i: (0, 0)),
    ]
    args = [x, mod, w_in, qk_gain, bd_qk]
    out_shape = [
        jax.ShapeDtypeStruct((nb, N_HEADS, seq, HEAD_DIM), BF16),
        jax.ShapeDtypeStruct((nb, N_KV_HEADS, HEAD_DIM, seq), BF16),
        jax.ShapeDtypeStruct((nb, N_KV_HEADS, seq, KV_DIM), BF16),
        jax.ShapeDtypeStruct((nb, seq, HY_IN), F32),
    ]
    out_specs = [
        pl.BlockSpec((None, N_HEADS, tm, HEAD_DIM), lambda b, i: (b, 0, i, 0)),
        pl.BlockSpec((None, N_KV_HEADS, HEAD_DIM, tm), lambda b, i: (b, 0, 0, i)),
        pl.BlockSpec((None, N_KV_HEADS, tm, KV_DIM), lambda b, i: (b, 0, i, 0)),
        pl.BlockSpec((None, tm, HY_IN), lambda b, i: (b, i, 0)),
    ]
    if latent:
        cos, sin = _rope_tables(seq)
        in_specs += [pl.BlockSpec((tm, LANES), lambda b, i: (i, 0))] * 2
        args += [cos, sin]
    else:
        out_shape += [jax.ShapeDtypeStruct((nb, seq, KV_DIM), F32)] * 2
        out_specs += [pl.BlockSpec((None, tm, KV_DIM), lambda b, i: (b, i, 0))] * 2
    return pl.pallas_call(
        functools.partial(_inproj_kernel, latent),
        out_shape=out_shape,
        grid=grid,
        in_specs=in_specs,
        out_specs=out_specs,
        compiler_params=_cparams(("parallel", "parallel")),
    )(*args)


def _attn_kernel(n_ctx, chunk, subtiles, q_ref, kt_ref, v_ref, gain_ref, *rest):
    if n_ctx:
        ckt_ref, cv_ref, o_ref = rest
    else:
        (o_ref,) = rest
    bb, heads, tq, _ = q_ref.shape
    seq = kt_ref.shape[-1]
    g = GQA_GROUP
    ts = tq // subtiles
    for b in range(bb):
        for k in range(heads // g):
            pieces = [(kt_ref[b, k, :, c * chunk:(c + 1) * chunk], v_ref[b, k, c * chunk:(c + 1) * chunk, :])
                      for c in range(seq // chunk)]
            if n_ctx:
                pieces.append((ckt_ref[b, k], cv_ref[b, k]))
            for t in range(subtiles):
                rows = slice(t * ts, (t + 1) * ts)
                qs = q_ref[b, k * g:(k + 1) * g, rows, :].reshape(g * ts, HEAD_DIM)
                m = acc = None
                for kt_c, v_c in pieces:
                    s = _dot(qs, kt_c)
                    row_max = jnp.max(s, axis=1, keepdims=True)
                    m_new = row_max if m is None else jnp.maximum(m, row_max)
                    pv = _dot(jnp.exp2(s - m_new).astype(BF16), v_c)
                    acc = pv if m is None else jnp.exp2(m - m_new) * acc + pv
                    m = m_new
                o = acc[:, :HEAD_DIM] / acc[:, HEAD_DIM:HEAD_DIM + 1]
                o = o * lax.rsqrt(jnp.mean(o * o, axis=1, keepdims=True) + EPS)
                for i in range(g):
                    cols = slice((k * g + i) * HEAD_DIM, (k * g + i + 1) * HEAD_DIM)
                    o_ref[b, rows, cols] = (o[i * ts:(i + 1) * ts] * gain_ref[:, cols]).astype(o_ref.dtype)


def _attention(q, kt, v, gain, ckt, cv):
    nb, _, seq, _ = q.shape
    n_ctx = 0 if ckt is None else ckt.shape[-1]
    chunk = _pick_tile(seq, 2048)
    if seq >= ATTN_LONG_SEQ:
        bb, kv, tq, subtiles = 1, 1, _pick_tile(seq, 512), ATTN_SUBTILES
    else:
        bb, kv, tq, subtiles = _pick_tile(nb, 4), N_KV_HEADS, seq, 1
    width = kv * GQA_GROUP * HEAD_DIM
    in_specs = [
        pl.BlockSpec((bb, kv * GQA_GROUP, tq, HEAD_DIM), lambda b, k, i: (b, k, i, 0)),
        pl.BlockSpec((bb, kv, HEAD_DIM, seq), lambda b, k, i: (b, k, 0, 0)),
        pl.BlockSpec((bb, kv, seq, KV_DIM), lambda b, k, i: (b, k, 0, 0)),
        pl.BlockSpec((1, width), lambda b, k, i: (0, k)),
    ]
    args = [q, kt, v, gain]
    if n_ctx:
        in_specs += [
            pl.BlockSpec((bb, kv, HEAD_DIM, n_ctx), lambda b, k, i: (b, k, 0, 0)),
            pl.BlockSpec((bb, kv, n_ctx, KV_DIM), lambda b, k, i: (b, k, 0, 0)),
        ]
        args += [ckt, cv]
    return pl.pallas_call(
        functools.partial(_attn_kernel, n_ctx, chunk, subtiles),
        out_shape=jax.ShapeDtypeStruct((nb, seq, Q_DIM), BF16),
        grid=(nb // bb, N_KV_HEADS // kv, seq // tq),
        in_specs=in_specs,
        out_specs=pl.BlockSpec((bb, tq, width), lambda b, k, i: (b, i, k)),
        compiler_params=_cparams(("parallel", "parallel", "parallel")),
    )(*args)


def _hyena_decimation(seq):
    return 8 if seq >= 2048 else 1


def _dft_tables(seq, r):
    n_sub = 2 * seq // r
    half = n_sub // 2
    k = np.arange(half, dtype=np.float64)[:, None]
    m = np.arange(n_sub, dtype=np.float64)[None, :]
    ang = 2.0 * np.pi * k * m / n_sub
    fwd = np.concatenate([np.cos(ang), -np.sin(ang)], axis=0)
    fwd[half] = np.cos(np.pi * m[0])
    inv = fwd.T.copy() * (2.0 / n_sub)
    inv[:, 0] *= 0.5
    inv[:, half] *= 0.5
    inv = inv[:seq // r]
    kk = np.arange(half, dtype=np.float64)[:, None] * np.ones((1, LANES))
    tw_r = np.cos(2.0 * np.pi * kk / n_sub)
    tw_i = -np.sin(2.0 * np.pi * kk / n_sub)
    return fwd, inv, tw_r, tw_i


def _filter_positions(seq, r):
    n_tot = 2 * seq
    n = (np.arange(n_tot // r)[None, :] * r + np.arange(r)[:, None]).reshape(-1)
    j = np.where(n < seq, n, n_tot - n)
    t = j.astype(np.float64) / seq
    bands = np.arange(1, HY_BANDS + 1, dtype=np.float64)
    ang = 2.0 * np.pi * t[:, None] * bands
    z = np.concatenate([t[:, None], np.sin(ang), np.cos(ang)], axis=-1)
    ones = np.ones((1, HY_FILTER_HIDDEN))
    sel_f = (n < seq).astype(np.float64)[:, None] * ones
    sel_b = (n > seq).astype(np.float64)[:, None] * ones
    return z, t[:, None] * np.ones((1, LANES)), sel_f, sel_b


def _filter_ffn_kernel(z_ref, self_ref, selb_ref, w1_ref, b1_ref, w2_ref, b2_ref, fr_ref, hf_ref, hb_ref):
    fr = fr_ref[...]
    h = jnp.sin(fr * (_dot3(z_ref[...], w1_ref[...]) + b1_ref[...]))
    h = jnp.sin(fr * (_dot3(h, w2_ref[...]) + b2_ref[...]))
    hf_ref[...] = h * self_ref[...]
    hb_ref[...] = h * selb_ref[...]


def _filter_spec_kernel(r, hf_ref, hb_ref, t_ref, w3f_ref, w3b_ref, dcf_ref, dcb_ref, fh_ref, ga_ref, gb_ref, gc_ref):
    t = t_ref[...]
    g = (_dot3(hf_ref[...], w3f_ref[...]) * jnp.exp(-t * jnp.abs(dcf_ref[...]))
         + _dot3(hb_ref[...], w3b_ref[...]) * jnp.exp(-t * jnp.abs(dcb_ref[...])))
    g = g * lax.rsqrt(jnp.sum(g * g, axis=0, keepdims=True) + EPS)
    n_sub = g.shape[0] // r
    half = n_sub // 2
    fh = fh_ref[...]
    for p in range(r):
        spec = _dot(fh, g[p * n_sub:(p + 1) * n_sub].astype(BF16))
        ga_ref[p] = spec[:half]
        gb_ref[p] = spec[half:] - spec[:half]
        gc_ref[p] = spec[half:] + spec[:half]


def _hyena_filters(seq, r, w1, b1, w2, b2, w3, freq, decay, fwd):
    n_tot = 2 * seq
    n_sub = n_tot // r
    half = n_sub // 2
    z, t, sel_f, sel_b = _filter_positions(seq, r)
    pad = (-HY_POS_DIM) % SUBLANES
    z = jnp.asarray(np.pad(z, ((0, 0), (0, pad))), F32)
    w1p = jnp.pad(w1, ((0, pad), (0, 0)))
    kin = HY_POS_DIM + pad
    hid = HY_FILTER_HIDDEN
    tr = min(n_tot, 512)
    rows = lambda width: pl.BlockSpec((tr, width), lambda i: (i, 0))
    full = lambda shape: pl.BlockSpec(shape, lambda j: (0,) * len(shape))
    hf, hb = pl.pallas_call(
        _filter_ffn_kernel,
        out_shape=[jax.ShapeDtypeStruct((n_tot, hid), F32)] * 2,
        grid=(n_tot // tr,),
        in_specs=[rows(kin), rows(hid), rows(hid), full((kin, hid)), full((1, hid)), full((hid, hid)),
                  full((1, hid)), full((1, hid))],
        out_specs=[rows(hid)] * 2,
        compiler_params=_cparams(("parallel",)),
    )(z, jnp.asarray(sel_f, F32), jnp.asarray(sel_b, F32), w1p, b1.reshape(1, hid), w2, b2.reshape(1, hid),
      freq.reshape(1, hid))
    ncb = HY_OC // LANES
    return pl.pallas_call(
        functools.partial(_filter_spec_kernel, r),
        out_shape=[jax.ShapeDtypeStruct((r, half, HY_OC), F32)] * 3,
        grid=(ncb,),
        in_specs=[
            full((n_tot, hid)), full((n_tot, hid)), full((n_tot, LANES)),
            pl.BlockSpec((hid, LANES), lambda j: (0, j)),
            pl.BlockSpec((hid, LANES), lambda j: (0, j + ncb)),
            pl.BlockSpec((1, LANES), lambda j: (0, j)),
            pl.BlockSpec((1, LANES), lambda j: (0, j + ncb)),
            full((2 * half, n_sub)),
        ],
        out_specs=[pl.BlockSpec((r, half, LANES), lambda j: (0, 0, j))] * 3,
        compiler_params=_cparams(("parallel",)),
    )(hf, hb, jnp.asarray(t, F32), w3, w3, decay.reshape(1, -1), decay.reshape(1, -1), fwd)


def _hyena_kernel(r, hy0_ref, hy1_ref, hy2_ref, sw_ref, sb_ref, skip_ref,
                  ga0_ref, gb0_ref, gc0_ref, ga1_ref, gb1_ref, gc1_ref, fwd_ref, inv_ref, twr_ref, twi_ref,
                  o_ref, z_ref, ph_ref, rhs_ref, x_ref):
    seq = hy0_ref.shape[0]
    m_len = seq // r
    half = fwd_ref.shape[0] // 2
    row = lax.broadcasted_iota(jnp.int32, (m_len, LANES), 0)
    hy_refs = (hy0_ref, hy1_ref, hy2_ref)

    def load_phases(part):
        for j in range(r):
            ph_ref[j] = hy_refs[part][pl.ds(j, m_len, stride=r), :]

    def short_conv(part, j):
        w = sw_ref[:, part * LANES:(part + 1) * LANES]
        b = sb_ref[:, part * LANES:(part + 1) * LANES]
        if j > 0:
            prev = ph_ref[j - 1]
        else:
            prev = jnp.where(row == 0, 0.0, pltpu.roll(ph_ref[r - 1], 1, axis=0))
        if j < r - 1:
            nxt = ph_ref[j + 1]
        else:
            nxt = jnp.where(row == m_len - 1, 0.0, pltpu.roll(ph_ref[0], m_len - 1, axis=0))
        return prev * w[0:1] + ph_ref[j] * w[1:2] + nxt * w[2:3] + b

    load_phases(0)
    for j in range(r):
        z_ref[j] = short_conv(0, j)

    for o, (ga_ref, gb_ref, gc_ref) in enumerate(((ga0_ref, gb0_ref, gc0_ref), (ga1_ref, gb1_ref, gc1_ref))):
        for j in range(r):
            rhs_ref[:, j * LANES:(j + 1) * LANES] = z_ref[j].astype(BF16)
        x_ref[...] = _dot(fwd_ref[...], rhs_ref[...])
        dc = [x_ref[0:1, j * LANES:(j + 1) * LANES] for j in range(r)]
        ny = [x_ref[half:half + 1, j * LANES:(j + 1) * LANES] for j in range(r)]
        y_dc, y_ny = [], []
        for j in range(r):
            a = jnp.zeros((1, LANES), F32)
            c = jnp.zeros((1, LANES), F32)
            for jp in range(r):
                p = (j - jp) % r
                a = a + ga_ref[p, 0:1, :] * dc[jp]
                t = (ga_ref[p, 0:1, :] + gb_ref[p, 0:1, :]) * ny[jp]
                c = c + t if jp <= j else c - t
            y_dc.append(a)
            y_ny.append(c)

        def mix(i, carry):
            r0 = pl.multiple_of(i * SUBLANES, SUBLANES)
            rows_re = pl.ds(r0, SUBLANES)
            rows_im = pl.ds(half + r0, SUBLANES)
            xr = [x_ref[rows_re, j * LANES:(j + 1) * LANES] for j in range(r)]
            xi = [x_ref[rows_im, j * LANES:(j + 1) * LANES] for j in range(r)]
            xs = [a + b for a, b in zip(xr, xi)]
            wr = twr_ref[rows_re, :]
            wi = twi_ref[rows_re, :]
            for j in range(r):
                acc = {}
                for jp in range(r):
                    p = (j - jp) % r
                    k1 = ga_ref[p, rows_re, :] * xs[jp]
                    k2 = gb_ref[p, rows_re, :] * xr[jp]
                    k3 = gc_ref[p, rows_re, :] * xi[jp]
                    side = jp <= j
                    ks = (k1, k2, k3)
                    acc[side] = ks if side not in acc else tuple(a + b for a, b in zip(acc[side], ks))
                pr = acc[True][0] - acc[True][2]
                pi = acc[True][0] + acc[True][1]
                if False in acc:
                    qr = acc[False][0] - acc[False][2]
                    qi = acc[False][0] + acc[False][1]
                    pr = pr + wr * qr - wi * qi
                    pi = pi + wr * qi + wi * qr
                x_ref[rows_re, j * LANES:(j + 1) * LANES] = pr
                x_ref[rows_im, j * LANES:(j + 1) * LANES] = pi
            return carry

        lax.fori_loop(0, half // SUBLANES, mix, 0)
        for j in range(r):
            x_ref[0:1, j * LANES:(j + 1) * LANES] = y_dc[j]
            x_ref[half:half + 1, j * LANES:(j + 1) * LANES] = y_ny[j]
        y = _dot(inv_ref[...], x_ref[...].astype(BF16))
        sk = skip_ref[o:o + 1, :]
        load_phases(o + 1)
        for j in range(r):
            z_ref[j] = short_conv(o + 1, j) * (y[:, j * LANES:(j + 1) * LANES] + z_ref[j] * sk)
    for j in range(r):
        o_ref[pl.ds(j, m_len, stride=r), :] = z_ref[j]


def _hyena_direct_kernel(hy_ref, sw_ref, sb_ref, skip_ref, ga_ref, gb_ref, fwd_ref, inv_ref, o_ref):
    bb, seq, _ = hy_ref.shape
    half = fwd_ref.shape[0] // 2
    w = HYENA_WIDTH
    row = lax.broadcasted_iota(jnp.int32, (seq, w), 0)
    is_dc = lax.broadcasted_iota(jnp.int32, (half, w), 0) == 0

    def short_conv(b, part):
        cols = slice(part * w, (part + 1) * w)
        x = hy_ref[b, :, cols]
        prev = jnp.where(row == 0, 0.0, pltpu.roll(x, 1, axis=0))
        nxt = jnp.where(row == seq - 1, 0.0, pltpu.roll(x, seq - 1, axis=0))
        return prev * sw_ref[0:1, cols] + x * sw_ref[1:2, cols] + nxt * sw_ref[2:3, cols] + sb_ref[:, cols]

    for b in range(bb):
        z = short_conv(b, 0)
        for o in range(HY_ORDER):
            cols = slice(o * w, (o + 1) * w)
            x = _dot(fwd_ref[...], z.astype(BF16))
            xr, xi = x[:half], x[half:]
            gr = ga_ref[0, :, cols]
            gi = gb_ref[0, :, cols] + gr
            vr = gr * xr - jnp.where(is_dc, 0.0, gi * xi)
            vi = jnp.where(is_dc, gi * xi, gr * xi + gi * xr)
            y = _dot(inv_ref[...], jnp.concatenate([vr, vi], axis=0).astype(BF16))
            z = short_conv(b, o + 1) * (y + z * skip_ref[o:o + 1, :])
        o_ref[b] = z


def _hyena_direct(hy, short_w, short_b, skip, ga, gb, fwd, inv):
    nb, seq, _ = hy.shape
    bb = _pick_tile(nb, 4)
    const = lambda a: pl.BlockSpec(a.shape, lambda i: (0,) * a.ndim)
    short_b = short_b.reshape(1, -1)
    return pl.pallas_call(
        _hyena_direct_kernel,
        out_shape=jax.ShapeDtypeStruct((nb, seq, HYENA_WIDTH), F32),
        grid=(nb // bb,),
        in_specs=[pl.BlockSpec((bb, seq, HY_IN), lambda i: (i, 0, 0)), const(short_w), const(short_b), const(skip),
                  const(ga), const(gb), const(fwd), const(inv)],
        out_specs=pl.BlockSpec((bb, seq, HYENA_WIDTH), lambda i: (i, 0, 0)),
        compiler_params=_cparams(("parallel",)),
    )(hy, short_w, short_b, skip, ga, gb, fwd, inv)


def _hyena(hy, short_w, short_b, skip, spectra, fwd, inv, tw_r, tw_i, r):
    ga, gb, gc = spectra
    if r == 1:
        return _hyena_direct(hy, short_w, short_b, skip, ga, gb, fwd, inv)
    nb, seq, _ = hy.shape
    m_len = seq // r
    n_half2 = fwd.shape[0]
    half = n_half2 // 2
    ncb = HYENA_WIDTH // LANES
    parts = HY_ORDER + 1
    once = pl.Buffered(1)
    hy_spec = lambda part: pl.BlockSpec((None, seq, LANES), lambda c, b: (b, 0, part * ncb + c))
    g_spec = lambda o: pl.BlockSpec((r, half, LANES), lambda c, b: (0, 0, o * ncb + c), pipeline_mode=once)
    const = lambda shape: pl.BlockSpec(shape, lambda c, b: (0,) * len(shape), pipeline_mode=once)
    sw = short_w.reshape(3, parts, ncb, LANES).transpose(2, 0, 1, 3).reshape(ncb, 3, parts * LANES)
    sb = short_b.reshape(1, parts, ncb, LANES).transpose(2, 0, 1, 3).reshape(ncb, 1, parts * LANES)
    return pl.pallas_call(
        functools.partial(_hyena_kernel, r),
        out_shape=jax.ShapeDtypeStruct((nb, seq, HYENA_WIDTH), F32),
        grid=(ncb, nb),
        in_specs=[
            hy_spec(0), hy_spec(1), hy_spec(2),
            pl.BlockSpec((None, 3, parts * LANES), lambda c, b: (c, 0, 0)),
            pl.BlockSpec((None, 1, parts * LANES), lambda c, b: (c, 0, 0)),
            pl.BlockSpec((HY_ORDER, LANES), lambda c, b: (0, c)),
            g_spec(0), g_spec(0), g_spec(0), g_spec(1), g_spec(1), g_spec(1),
            const((n_half2, m_len)), const((m_len, n_half2)),
            const((half, LANES)), const((half, LANES)),
        ],
        out_specs=pl.BlockSpec((None, seq, LANES), lambda c, b: (b, 0, c)),
        scratch_shapes=[
            pltpu.VMEM((r, m_len, LANES), F32),
            pltpu.VMEM((r, m_len, LANES), F32),
            pltpu.VMEM((m_len, r * LANES), BF16),
            pltpu.VMEM((n_half2, r * LANES), F32),
        ],
        compiler_params=_cparams(("parallel", "parallel")),
    )(hy, hy, hy, sw, sb, skip, ga, gb, gc, ga, gb, gc, fwd, inv, tw_r, tw_i)


def _route(logits):
    lane = lax.broadcasted_iota(jnp.int32, logits.shape, 1).astype(F32)
    big = jnp.float32(1e9)
    is_grp = (lane >= N_EXPERTS) & (lane < N_EXPERTS + N_GROUPS)
    gl = jnp.where(is_grp, logits, NEG_BIG)
    gmax = jnp.max(gl, axis=1, keepdims=True)
    gidx = jnp.min(jnp.where(gl == gmax, lane, big), axis=1, keepdims=True) - N_EXPERTS
    den = jnp.sum(jnp.where(is_grp, jnp.exp(gl - gmax), 0.0), axis=1, keepdims=True)
    pg_top = 1.0 / den
    lo = gidx * EXPERTS_PER_GROUP
    sel = jnp.where((lane >= lo) & (lane < lo + EXPERTS_PER_GROUP), logits, NEG_BIG)
    m1 = jnp.max(sel, axis=1, keepdims=True)
    i1 = jnp.min(jnp.where(sel == m1, lane, big), axis=1, keepdims=True)
    sel2 = jnp.where(lane == i1, NEG_BIG, sel)
    m2 = jnp.max(sel2, axis=1, keepdims=True)
    i2 = jnp.min(jnp.where(sel2 == m2, lane, big), axis=1, keepdims=True)
    e2 = jnp.exp(m2 - m1)
    w1 = pg_top / (1.0 + e2)
    w2 = pg_top * e2 / (1.0 + e2)
    comb = jnp.where(lane == i1, w1, 0.0) + jnp.where(lane == i2, w2, 0.0)
    return comb + jnp.where(lane == GROUP_ID_LANE, gidx, 0.0)


def _outproj_kernel(attn_ref, hyz_ref, x_ref, mod_ref, wo_ref, hg_ref, bd_ref, lng_ref, lnb_ref,
                    rwh_ref, rwl_ref, rb_ref, x1_ref, tok_ref, rec_ref):
    m = mod_ref[...]
    tm = x_ref.shape[0]
    n_sub = _proj_subtiles(tm)
    ts = tm // n_sub
    half = wo_ref.shape[0] // 2
    for t in range(n_sub):
        rows = slice(t * ts, (t + 1) * ts)
        z = hyz_ref[rows, :]
        ms = _dot((z * z).astype(BF16), bd_ref[...])
        zn = (z * lax.rsqrt(ms + EPS) * hg_ref[...]).astype(BF16)
        mix = _dot(attn_ref[rows, :], wo_ref[:half, :]) + _dot(zn, wo_ref[half:, :])
        x1 = _layer_norm(DEEPNORM_ALPHA * x_ref[rows, :] + m[2:3] * mix, lng_ref[...], lnb_ref[...])
        h2 = x1 * (1.0 + m[4:5]) + m[3:4]
        x1_ref[rows, :] = x1
        tok_ref[rows, :D_MODEL] = h2
        hh, hl = _split_bf16(h2)
        logits = _dot(hh, rwh_ref[...]) + _dot(hl, rwh_ref[...]) + _dot(hh, rwl_ref[...]) + rb_ref[...]
        record = _route(logits)
        tok_ref[rows, D_MODEL:] = record
        rec_ref[rows, :] = record


def _outproj(attn, hyz, x, mod, w_out, hy_gain, bd_hy, ln_g, ln_b, rw_hi, rw_lo, rb, latent, tm):
    nb, seq, _ = x.shape
    mod_map = (lambda b, i: (b, 0, 0)) if latent else (lambda b, i: (0, 0, 0))
    const = lambda shape: pl.BlockSpec(shape, lambda b, i: (0,) * len(shape))
    tok = lambda width: pl.BlockSpec((None, tm, width), lambda b, i: (b, i, 0))
    return pl.pallas_call(
        _outproj_kernel,
        out_shape=[
            jax.ShapeDtypeStruct((nb, seq, D_MODEL), F32),
            jax.ShapeDtypeStruct((nb, seq, D_MODEL + LANES), F32),
            jax.ShapeDtypeStruct((nb, seq, LANES), F32),
        ],
        grid=(nb, seq // tm),
        in_specs=[
            tok(Q_DIM), tok(HYENA_WIDTH), tok(D_MODEL),
            pl.BlockSpec((None, 6, D_MODEL), mod_map),
            const((D_MODEL, D_MODEL)), const((1, HYENA_WIDTH)), const((HYENA_WIDTH, HYENA_WIDTH)),
            const((1, D_MODEL)), const((1, D_MODEL)),
            const((D_MODEL, LANES)), const((D_MODEL, LANES)), const((1, LANES)),
        ],
        out_specs=[tok(D_MODEL), tok(D_MODEL + LANES), tok(LANES)],
        compiler_params=_cparams(("parallel", "parallel")),
    )(attn, hyz, x, mod, w_out, hy_gain, bd_hy, ln_g, ln_b, rw_hi, rw_lo, rb)


def _gather_rows(table, idx):
    n, d = idx.shape[0], table.shape[1]
    info = plsc.get_sparse_core_info()
    workers = info.num_cores * info.num_subcores
    per_worker = n // workers
    assert per_worker * workers == n and per_worker % GATHER_ROWS == 0
    mesh = plsc.VectorSubcoreMesh(core_axis_name="c", subcore_axis_name="s")

    @functools.partial(
        pl.kernel, mesh=mesh,
        out_type=jax.ShapeDtypeStruct((n, d), table.dtype),
        scratch_types=[pltpu.VMEM((GATHER_ROWS,), jnp.int32), pltpu.VMEM((GATHER_ROWS, d), table.dtype),
                       pltpu.SemaphoreType.DMA],
    )
    def gather(table_hbm, idx_hbm, out_hbm, idx_v, rows_v, sem):
        wid = lax.axis_index("s") * info.num_cores + lax.axis_index("c")
        base = wid * per_worker

        @pl.loop(0, per_worker // GATHER_ROWS)
        def _(j):
            off = base + j * GATHER_ROWS
            pltpu.sync_copy(idx_hbm.at[pl.ds(off, GATHER_ROWS)], idx_v)
            pltpu.async_copy(table_hbm.at[idx_v], rows_v, sem).wait()
            pltpu.sync_copy(rows_v, out_hbm.at[pl.ds(off, GATHER_ROWS)])

    return gather(table, idx)


def _sort_plan(record, tile):
    tokens = record.shape[0]
    gid = record[:, GROUP_ID_LANE].astype(jnp.int32)
    onehot = (gid[:, None] == jnp.arange(N_GROUPS, dtype=jnp.int32)[None, :]).astype(jnp.int32)
    csum = jnp.cumsum(onehot, axis=0)
    counts = csum[-1]
    rank = jnp.sum(csum * onehot, axis=1) - 1
    padded = (counts + tile - 1) // tile * tile
    ends = jnp.cumsum(padded)
    starts = ends - padded
    pos = jnp.sum(starts[None, :] * onehot, axis=1) + rank
    n_pad = tokens + N_GROUPS * tile
    src = jnp.zeros((n_pad,), jnp.int32).at[pos].set(jnp.arange(tokens, dtype=jnp.int32), unique_indices=True,
                                                      mode='promise_in_bounds')
    tile_start = jnp.arange(n_pad // tile, dtype=jnp.int32) * tile
    tile_group = jnp.minimum(jnp.sum((tile_start[:, None] >= ends[None, :]).astype(jnp.int32), axis=1), N_GROUPS - 1)
    tile_valid = (tile_start < ends[-1]).astype(jnp.int32)
    return pos, src, tile_group, tile_valid


def _moe_kernel(tg_ref, tv_ref, x_ref, wg_ref, wu_ref, wd_ref, o_ref):
    i = pl.program_id(0)

    @pl.when(tv_ref[i] > 0)
    def _():
        first = tg_ref[i] * EXPERTS_PER_GROUP
        x = x_ref[:, :D_MODEL].astype(BF16)
        rec = x_ref[:, D_MODEL:]
        lane = lax.broadcasted_iota(jnp.int32, rec.shape, 1)
        y = None
        for e in range(EXPERTS_PER_GROUP):
            c = jnp.sum(jnp.where(lane == first + e, rec, 0.0), axis=1, keepdims=True)
            h = _silu(_dot(x, wg_ref[e])) * _dot(x, wu_ref[e]) * c
            part = _dot(h.astype(BF16), wd_ref[e])
            y = part if y is None else y + part
        o_ref[...] = y

    @pl.when(tv_ref[i] == 0)
    def _():
        o_ref[...] = jnp.zeros_like(o_ref)


def _moe(sorted_tok, tile_group, tile_valid, wg, wu, wd):
    rows = sorted_tok.shape[0]
    tile = MOE_TILE
    grp = lambda shape: pl.BlockSpec(shape, lambda i, tg, tv: (tg[i], 0, 0))
    return pl.pallas_call(
        _moe_kernel,
        out_shape=jax.ShapeDtypeStruct((rows, D_MODEL), F32),
        grid_spec=pltpu.PrefetchScalarGridSpec(
            num_scalar_prefetch=2,
            grid=(rows // tile,),
            in_specs=[
                pl.BlockSpec((tile, D_MODEL + LANES), lambda i, tg, tv: (i, 0)),
                grp((EXPERTS_PER_GROUP, D_MODEL, EXPERT_FF)), grp((EXPERTS_PER_GROUP, D_MODEL, EXPERT_FF)),
                grp((EXPERTS_PER_GROUP, EXPERT_FF, D_MODEL)),
            ],
            out_specs=pl.BlockSpec((tile, D_MODEL), lambda i, tg, tv: (i, 0)),
        ),
        compiler_params=_cparams(("arbitrary",)),
    )(tile_group, tile_valid, sorted_tok, wg, wu, wd)


def _final_norm_kernel(x1_ref, y_ref, mod_ref, lng_ref, lnb_ref, o_ref):
    o_ref[...] = _layer_norm(DEEPNORM_ALPHA * x1_ref[...] + mod_ref[5:6, :] * y_ref[...], lng_ref[...], lnb_ref[...])


def _final_norm(x1, y, mod, ln_g, ln_b, latent, tm):
    nb, seq, _ = x1.shape
    mod_map = (lambda b, i: (b, 0, 0)) if latent else (lambda b, i: (0, 0, 0))
    tok = pl.BlockSpec((None, tm, D_MODEL), lambda b, i: (b, i, 0))
    const = pl.BlockSpec((1, D_MODEL), lambda b, i: (0, 0))
    return pl.pallas_call(
        _final_norm_kernel,
        out_shape=jax.ShapeDtypeStruct((nb, seq, D_MODEL), F32),
        grid=(nb, seq // tm),
        in_specs=[tok, tok, pl.BlockSpec((None, 6, D_MODEL), mod_map), const, const],
        out_specs=tok,
        compiler_params=_cparams(("parallel", "parallel")),
    )(x1, y, mod, ln_g, ln_b)


def _block_diag_mean(width, group):
    idx = np.arange(width) // group
    return jnp.asarray((idx[:, None] == idx[None, :]).astype(np.float32) / group, BF16)


def _pick_tile(n, target):
    t = min(n, target)
    while n % t:
        t //= 2
    return t


def _trunk_layer(x, mod, p, latent, ctx_k=None, ctx_v=None):
    nb, seq, _ = x.shape
    tm = _pick_tile(seq, 1024)
    q, kt, vx, hy, *kv_new = _inproj(x, mod, p['w_in'], p['qk_gain'], p['bd_qk'], latent, tm)
    if latent:
        ckt = jnp.transpose(ctx_k, (0, 2, 3, 1)).astype(BF16)
        cv = jnp.transpose(ctx_v, (0, 2, 1, 3))
        cvx = jnp.concatenate([cv, jnp.ones_like(cv)], axis=-1).astype(BF16)
    else:
        ckt = cvx = None
    attn = _attention(q, kt, vx, p['attn_gain'], ckt, cvx)

    r = _hyena_decimation(seq)
    fwd, inv, tw_r, tw_i = _dft_tables(seq, r)
    fwd = jnp.asarray(fwd, F32).astype(BF16)
    spectra = _hyena_filters(seq, r, p['hy_f_w1'], p['hy_f_b1'], p['hy_f_w2'], p['hy_f_b2'], p['hy_f_w3'],
                              p['hy_freq'], p['hy_decay'], fwd)
    hyz = _hyena(hy, p['hy_short_w'], p['hy_short_b'], p['hy_skip'], spectra,
                 fwd[:, :seq // r], jnp.asarray(inv, F32).astype(BF16), jnp.asarray(tw_r, F32),
                 jnp.asarray(tw_i, F32), r)

    x1, tok, record = _outproj(attn, hyz, x, mod, p['w_out'], p['hy_gain'], p['bd_hy'], p['ln1_g'], p['ln1_b'],
                               p['rw_hi'], p['rw_lo'], p['rb'], latent, tm)
    tok = tok.reshape(nb * seq, D_MODEL + LANES)
    pos, src, tile_group, tile_valid = _sort_plan(record.reshape(nb * seq, LANES), MOE_TILE)
    y_sorted = _moe(_gather_rows(tok, src), tile_group, tile_valid, p['wg'], p['wu'], p['wd'])
    y = _gather_rows(y_sorted, pos).reshape(nb, seq, D_MODEL)
    return _final_norm(x1, y, mod, p['ln2_g'], p['ln2_b'], latent, tm), kv_new


def _prepare(w_in, q_gain, k_gain, attn_out_gain, hy_short_w, hy_short_b, hy_f_w1, hy_f_b1, hy_f_w2, hy_f_b2,
             hy_f_w3, hy_freq, hy_decay, hy_skip, hy_out_gain, w_out, ln1_g, ln1_b, router_grp_w, router_grp_b,
             router_exp_w, router_exp_b, exp_w_gate, exp_w_up, exp_w_down, ln2_g, ln2_b, l):
    row = lambda a: a.reshape(1, -1)
    rw = jnp.concatenate([router_exp_w[l], router_grp_w[l]], axis=1)
    rw = jnp.pad(rw, ((0, 0), (0, LANES - rw.shape[1])))
    rb = jnp.concatenate([router_exp_b[l], router_grp_b[l]])
    rb = jnp.pad(rb, (0, LANES - rb.shape[0]))
    rw_hi, rw_lo = _split_bf16(rw)
    return {
        'w_in': w_in[l].astype(BF16),
        'qk_gain': row(jnp.concatenate([jnp.tile(q_gain[l], N_HEADS), jnp.tile(k_gain[l], N_KV_HEADS)])),
        'bd_qk': _block_diag_mean(QK_DIM, HEAD_DIM),
        'attn_gain': row(attn_out_gain[l]),
        'hy_short_w': hy_short_w[l], 'hy_short_b': hy_short_b[l],
        'hy_f_w1': hy_f_w1[l], 'hy_f_b1': hy_f_b1[l], 'hy_f_w2': hy_f_w2[l], 'hy_f_b2': hy_f_b2[l],
        'hy_f_w3': hy_f_w3[l], 'hy_freq': hy_freq[l], 'hy_decay': hy_decay[l], 'hy_skip': hy_skip[l],
        'hy_gain': row(hy_out_gain[l]),
        'bd_hy': _block_diag_mean(HYENA_WIDTH, HY_GROUP_DIM),
        'w_out': w_out[l].astype(BF16),
        'ln1_g': row(ln1_g[l]), 'ln1_b': row(ln1_b[l]),
        'rw_hi': rw_hi, 'rw_lo': rw_lo, 'rb': row(rb),
        'wg': exp_w_gate[l].astype(BF16), 'wu': exp_w_up[l].astype(BF16), 'wd': exp_w_down[l].astype(BF16),
        'ln2_g': row(ln2_g[l]), 'ln2_b': row(ln2_b[l]),
    }


def kernel(x_prompt, x_sample, c, cache_k, cache_v, c_ctx, w_mod, b_mod, w_in, q_gain, k_gain, attn_out_gain, hy_short_w, hy_short_b, hy_f_w1, hy_f_b1, hy_f_w2, hy_f_b2, hy_f_w3, hy_freq, hy_decay, hy_skip, hy_out_gain, w_out, ln1_g, ln1_b, router_grp_w, router_grp_b, router_exp_w, router_exp_b, exp_w_gate, exp_w_up, exp_w_down, ln2_g, ln2_b):
    depth = w_mod.shape[0]
    n_lat = c.shape[0]
    cond = jnp.concatenate([c, c_ctx[None, :]], axis=0)
    rows = -(-cond.shape[0] // SUBLANES) * SUBLANES
    cond = jnp.pad(cond, ((0, rows - cond.shape[0]), (0, 0)))
    y_prompt, y_sample = x_prompt, x_sample
    ks_new, vs_new = [], []
    for l in range(depth):
        p = _prepare(w_in, q_gain, k_gain, attn_out_gain, hy_short_w, hy_short_b, hy_f_w1, hy_f_b1, hy_f_w2,
                     hy_f_b2, hy_f_w3, hy_freq, hy_decay, hy_skip, hy_out_gain, w_out, ln1_g, ln1_b,
                     router_grp_w, router_grp_b, router_exp_w, router_exp_b, exp_w_gate, exp_w_up, exp_w_down,
                     ln2_g, ln2_b, l)
        mod = _adaln(cond, w_mod[l], b_mod[l])
        mod_lat = mod[:n_lat].reshape(n_lat, 6, D_MODEL)
        mod_ctx = mod[n_lat:n_lat + 1].reshape(1, 6, D_MODEL)
        y_prompt, (v_new, k_new) = _trunk_layer(y_prompt, mod_ctx, p, False)
        ks_new.append(k_new.reshape(k_new.shape[:2] + (N_KV_HEADS, HEAD_DIM)))
        vs_new.append(v_new.reshape(v_new.shape[:2] + (N_KV_HEADS, HEAD_DIM)))
        y_sample, _ = _trunk_layer(y_sample, mod_lat, p, True, cache_k[:, l], cache_v[:, l])
    return (y_prompt, y_sample, jnp.stack(ks_new, axis=1), jnp.stack(vs_new, axis=1))
```

```python
import functools

import numpy as np
import jax
import jax.numpy as jnp
from jax import lax
from jax.experimental import pallas as pl
from jax.experimental.pallas import tpu as pltpu
from jax.experimental.pallas import tpu_sc as plsc

F32 = jnp.float32
BF16 = jnp.bfloat16

D_MODEL = 1024
GRID_W = 64
HEAD_DIM = 64
N_HEADS = 8
N_KV_HEADS = 2
GQA_GROUP = N_HEADS // N_KV_HEADS
Q_DIM = N_HEADS * HEAD_DIM
KV_DIM = N_KV_HEADS * HEAD_DIM
QK_DIM = Q_DIM + KV_DIM
HYENA_WIDTH = 512
HY_ORDER = 2
HY_IN = (HY_ORDER + 1) * HYENA_WIDTH
HY_GROUP_DIM = 64
HY_BANDS = 16
HY_POS_DIM = 1 + 2 * HY_BANDS
HY_FILTER_HIDDEN = 64
HY_OC = HY_ORDER * HYENA_WIDTH
IN_WIDTH = Q_DIM + 2 * KV_DIM + HY_IN
ROPE_THETA = 10000.0
ROPE_FREQS = HEAD_DIM // 4
N_GROUPS = 4
EXPERTS_PER_GROUP = 8
N_EXPERTS = N_GROUPS * EXPERTS_PER_GROUP
EXPERT_FF = D_MODEL // 4
DEPTH = 1
DEEPNORM_ALPHA = (2.0 * DEPTH) ** 0.25
EPS = 1e-6

LANES = 128
SUBLANES = 8
VMEM_LIMIT = 56 * 1024 * 1024
MOE_TILE = 512
GATHER_ROWS = 32
GROUP_ID_LANE = N_EXPERTS
PROJ_SUBTILES = 2
PROJ_MIN_SUBTILE = 256
ATTN_SUBTILES = 4
ATTN_LONG_SEQ = 1024
NEG_BIG = -1e30
LOG2_E = 1.4426950408889634


def _cparams(sem):
    return pltpu.CompilerParams(dimension_semantics=sem, vmem_limit_bytes=VMEM_LIMIT)


def _proj_subtiles(tm):
    return PROJ_SUBTILES if tm // PROJ_SUBTILES >= PROJ_MIN_SUBTILE else 1


def _split_bf16(a):
    hi = a.astype(BF16)
    lo = (a - hi.astype(F32)).astype(BF16)
    return hi, lo


def _dot(a, b):
    return jnp.dot(a, b, preferred_element_type=F32)


def _dot3(a, b):
    ah, al = _split_bf16(a)
    bh, bl = _split_bf16(b)
    return _dot(ah, bh) + _dot(al, bh) + _dot(ah, bl)


def _silu(x):
    return x / (1.0 + jnp.exp(-x))


def _layer_norm(y, g, b):
    mu = jnp.mean(y, axis=-1, keepdims=True)
    yc = y - mu
    var = jnp.mean(yc * yc, axis=-1, keepdims=True)
    return yc * lax.rsqrt(var + EPS) * g + b


def _adaln_kernel(c_ref, w_ref, b_ref, o_ref):
    o_ref[...] = _dot3(_silu(c_ref[...]), w_ref[...]) + b_ref[...]


def _adaln(cond, w_mod, b_mod):
    rows = cond.shape[0]
    n = w_mod.shape[1]
    tn = 1536
    return pl.pallas_call(
        _adaln_kernel,
        out_shape=jax.ShapeDtypeStruct((rows, n), F32),
        grid=(n // tn,),
        in_specs=[
            pl.BlockSpec((rows, D_MODEL), lambda j: (0, 0)),
            pl.BlockSpec((D_MODEL, tn), lambda j: (0, j)),
            pl.BlockSpec((1, tn), lambda j: (0, j)),
        ],
        out_specs=pl.BlockSpec((rows, tn), lambda j: (0, j)),
        compiler_params=_cparams(("arbitrary",)),
    )(cond, w_mod, b_mod.reshape(1, n))


def _rope_tables(seq):
    t = np.arange(seq)
    rows = (t // GRID_W).astype(np.float64)
    cols = (t % GRID_W).astype(np.float64)
    inv_freq = ROPE_THETA ** (-np.arange(ROPE_FREQS, dtype=np.float64) / ROPE_FREQS)
    d = np.arange(LANES) % HEAD_DIM
    axis = d // (2 * ROPE_FREQS)
    f = d % ROPE_FREQS
    pos = np.where(axis[None, :] == 0, rows[:, None], cols[:, None])
    ang = pos * inv_freq[f][None, :]
    first = (d % (2 * ROPE_FREQS)) < ROPE_FREQS
    cos = np.cos(ang)
    sin = np.where(first[None, :], -np.sin(ang), np.sin(ang))
    return jnp.asarray(cos, F32), jnp.asarray(sin, F32)


def _inproj_kernel(latent, x_ref, mod_ref, w_ref, gain_ref, bd_ref, *rest):
    if latent:
        cos_ref, sin_ref, q_ref, kt_ref, vx_ref, hy_ref = rest
    else:
        q_ref, kt_ref, vx_ref, hy_ref, v_ref, knat_ref = rest
    m = mod_ref[...]
    tm = x_ref.shape[0]
    n_sub = _proj_subtiles(tm)
    ts = tm // n_sub
    for t in range(n_sub):
        rows = slice(t * ts, (t + 1) * ts)
        h = x_ref[rows, :] * (1.0 + m[1:2]) + m[0:1]
        proj = _dot(h.astype(BF16), w_ref[...])
        qk = proj[:, :QK_DIM]
        ms = _dot((qk * qk).astype(BF16), bd_ref[...])
        qk = qk * lax.rsqrt(ms + EPS) * gain_ref[...]
        if not latent:
            knat_ref[rows, :] = qk[:, Q_DIM:]
        else:
            cos = cos_ref[rows, :]
            sin = sin_ref[rows, :]
            lane = lax.broadcasted_iota(jnp.int32, cos.shape, 1)
            first = (lane % (2 * ROPE_FREQS)) < ROPE_FREQS
            chunks = []
            for c in range(QK_DIM // LANES):
                xc = qk[:, c * LANES:(c + 1) * LANES]
                below = pltpu.roll(xc, ROPE_FREQS, axis=1)
                above = pltpu.roll(xc, LANES - ROPE_FREQS, axis=1)
                chunks.append(xc * cos + jnp.where(first, above, below) * sin)
            qk = jnp.concatenate(chunks, axis=1)
        qs = (qk[:, :Q_DIM] * (HEAD_DIM ** -0.5 * LOG2_E)).astype(BF16)
        for hd in range(N_HEADS):
            q_ref[hd, rows, :] = qs[:, hd * HEAD_DIM:(hd + 1) * HEAD_DIM]
        kt = qk[:, Q_DIM:].T
        for kh in range(N_KV_HEADS):
            kt_ref[kh, :, rows] = kt[kh * HEAD_DIM:(kh + 1) * HEAD_DIM].astype(BF16)
        v = proj[:, QK_DIM:QK_DIM + KV_DIM]
        low = lax.broadcasted_iota(jnp.int32, v.shape, 1) < HEAD_DIM
        vx_ref[0, rows, :] = jnp.where(low, v, 1.0).astype(BF16)
        vx_ref[1, rows, :] = jnp.where(low, pltpu.roll(v, HEAD_DIM, axis=1), 1.0).astype(BF16)
        if not latent:
            v_ref[rows, :] = v
        hy_ref[rows, :] = proj[:, QK_DIM + KV_DIM:]


def _inproj(x, mod, w_in, qk_gain, bd_qk, latent, tm):
    nb, seq, _ = x.shape
    grid = (nb, seq // tm)
    mod_map = (lambda b, i: (b, 0, 0)) if latent else (lambda b, i: (0, 0, 0))
    in_specs = [
        pl.BlockSpec((None, tm, D_MODEL), lambda b, i: (b, i, 0)),
        pl.BlockSpec((None, 6, D_MODEL), mod_map),
        pl.BlockSpec((D_MODEL, IN_WIDTH), lambda b, i: (0, 0)),
        pl.BlockSpec((1, QK_DIM), lambda b, i: (0, 0)),
        pl.BlockSpec((QK_DIM, QK_DIM), lambda b, i: (0, 0)),
    ]
    args = [x, mod, w_in, qk_gain, bd_qk]
    out_shape = [
        jax.ShapeDtypeStruct((nb, N_HEADS, seq, HEAD_DIM), BF16),
        jax.ShapeDtypeStruct((nb, N_KV_HEADS, HEAD_DIM, seq), BF16),
        jax.ShapeDtypeStruct((nb, N_KV_HEADS, seq, KV_DIM), BF16),
        jax.ShapeDtypeStruct((nb, seq, HY_IN), F32),
    ]
    out_specs = [
        pl.BlockSpec((None, N_HEADS, tm, HEAD_DIM), lambda b, i: (b, 0, i, 0)),
        pl.BlockSpec((None, N_KV_HEADS, HEAD_DIM, tm), lambda b, i: (b, 0, 0, i)),
        pl.BlockSpec((None, N_KV_HEADS, tm, KV_DIM), lambda b, i: (b, 0, i, 0)),
        pl.BlockSpec((None, tm, HY_IN), lambda b, i: (b, i, 0)),
    ]
    if latent:
        cos, sin = _rope_tables(seq)
        in_specs += [pl.BlockSpec((tm, LANES), lambda b, i: (i, 0))] * 2
        args += [cos, sin]
    else:
        out_shape += [jax.ShapeDtypeStruct((nb, seq, KV_DIM), F32)] * 2
        out_specs += [pl.BlockSpec((None, tm, KV_DIM), lambda b, i: (b, i, 0))] * 2
    return pl.pallas_call(
        functools.partial(_inproj_kernel, latent),
        out_shape=out_shape,
        grid=grid,
        in_specs=in_specs,
        out_specs=out_specs,
        compiler_params=_cparams(("parallel", "parallel")),
    )(*args)


def _attn_kernel(n_ctx, chunk, subtiles, q_ref, kt_ref, v_ref, gain_ref, *rest):
    if n_ctx:
        ckt_ref, cv_ref, o_ref = rest
    else:
        (o_ref,) = rest
    bb, heads, tq, _ = q_ref.shape
    seq = kt_ref.shape[-1]
    g = GQA_GROUP
    ts = tq // subtiles
    for b in range(bb):
        for k in range(heads // g):
            pieces = [(kt_ref[b, k, :, c * chunk:(c + 1) * chunk], v_ref[b, k, c * chunk:(c + 1) * chunk, :])
                      for c in range(seq // chunk)]
            if n_ctx:
                pieces.append((ckt_ref[b, k], cv_ref[b, k]))
            for t in range(subtiles):
                rows = slice(t * ts, (t + 1) * ts)
                qs = q_ref[b, k * g:(k + 1) * g, rows, :].reshape(g * ts, HEAD_DIM)
                m = acc = None
                for kt_c, v_c in pieces:
                    s = _dot(qs, kt_c)
                    row_max = jnp.max(s, axis=1, keepdims=True)
                    m_new = row_max if m is None else jnp.maximum(m, row_max)
                    pv = _dot(jnp.exp2(s - m_new).astype(BF16), v_c)
                    acc = pv if m is None else jnp.exp2(m - m_new) * acc + pv
                    m = m_new
                o = acc[:, :HEAD_DIM] / acc[:, HEAD_DIM:HEAD_DIM + 1]
                o = o * lax.rsqrt(jnp.mean(o * o, axis=1, keepdims=True) + EPS)
                for i in range(g):
                    cols = slice((k * g + i) * HEAD_DIM, (k * g + i + 1) * HEAD_DIM)
                    o_ref[b, rows, cols] = (o[i * ts:(i + 1) * ts] * gain_ref[:, cols]).astype(o_ref.dtype)


def _attention(q, kt, v, gain, ckt, cv):
    nb, _, seq, _ = q.shape
    n_ctx = 0 if ckt is None else ckt.shape[-1]
    chunk = _pick_tile(seq, 2048)
    if seq >= ATTN_LONG_SEQ:
        bb, kv, tq, subtiles = 1, 1, _pick_tile(seq, 512), ATTN_SUBTILES
    else:
        bb, kv, tq, subtiles = _pick_tile(nb, 4), N_KV_HEADS, seq, 1
    width = kv * GQA_GROUP * HEAD_DIM
    in_specs = [
        pl.BlockSpec((bb, kv * GQA_GROUP, tq, HEAD_DIM), lambda b, k, i: (b, k, i, 0)),
        pl.BlockSpec((bb, kv, HEAD_DIM, seq), lambda b, k, i: (b, k, 0, 0)),
        pl.BlockSpec((bb, kv, seq, KV_DIM), lambda b, k, i: (b, k, 0, 0)),
        pl.BlockSpec((1, width), lambda b, k, i: (0, k)),
    ]
    args = [q, kt, v, gain]
    if n_ctx:
        in_specs += [
            pl.BlockSpec((bb, kv, HEAD_DIM, n_ctx), lambda b, k, i: (b, k, 0, 0)),
            pl.BlockSpec((bb, kv, n_ctx, KV_DIM), lambda b, k, i: (b, k, 0, 0)),
        ]
        args += [ckt, cv]
    return pl.pallas_call(
        functools.partial(_attn_kernel, n_ctx, chunk, subtiles),
        out_shape=jax.ShapeDtypeStruct((nb, seq, Q_DIM), BF16),
        grid=(nb // bb, N_KV_HEADS // kv, seq // tq),
        in_specs=in_specs,
        out_specs=pl.BlockSpec((bb, tq, width), lambda b, k, i: (b, i, k)),
        compiler_params=_cparams(("parallel", "parallel", "parallel")),
    )(*args)


def _hyena_decimation(seq):
    return 8 if seq >= 2048 else 1


def _dft_tables(seq, r):
    n_sub = 2 * seq // r
    half = n_sub // 2
    k = np.arange(half, dtype=np.float64)[:, None]
    m = np.arange(n_sub, dtype=np.float64)[None, :]
    ang = 2.0 * np.pi * k * m / n_sub
    fwd = np.concatenate([np.cos(ang), -np.sin(ang)], axis=0)
    fwd[half] = np.cos(np.pi * m[0])
    inv = fwd.T.copy() * (2.0 / n_sub)
    inv[:, 0] *= 0.5
    inv[:, half] *= 0.5
    inv = inv[:seq // r]
    kk = np.arange(half, dtype=np.float64)[:, None] * np.ones((1, LANES))
    tw_r = np.cos(2.0 * np.pi * kk / n_sub)
    tw_i = -np.sin(2.0 * np.pi * kk / n_sub)
    return fwd, inv, tw_r, tw_i


def _filter_positions(seq, r):
    n_tot = 2 * seq
    n = (np.arange(n_tot // r)[None, :] * r + np.arange(r)[:, None]).reshape(-1)
    j = np.where(n < seq, n, n_tot - n)
    t = j.astype(np.float64) / seq
    bands = np.arange(1, HY_BANDS + 1, dtype=np.float64)
    ang = 2.0 * np.pi * t[:, None] * bands
    z = np.concatenate([t[:, None], np.sin(ang), np.cos(ang)], axis=-1)
    ones = np.ones((1, HY_FILTER_HIDDEN))
    sel_f = (n < seq).astype(np.float64)[:, None] * ones
    sel_b = (n > seq).astype(np.float64)[:, None] * ones
    return z, t[:, None] * np.ones((1, LANES)), sel_f, sel_b


def _filter_ffn_kernel(z_ref, self_ref, selb_ref, w1_ref, b1_ref, w2_ref, b2_ref, fr_ref, hf_ref, hb_ref):
    fr = fr_ref[...]
    h = jnp.sin(fr * (_dot3(z_ref[...], w1_ref[...]) + b1_ref[...]))
    h = jnp.sin(fr * (_dot3(h, w2_ref[...]) + b2_ref[...]))
    hf_ref[...] = h * self_ref[...]
    hb_ref[...] = h * selb_ref[...]


def _filter_spec_kernel(r, hf_ref, hb_ref, t_ref, w3f_ref, w3b_ref, dcf_ref, dcb_ref, fh_ref, ga_ref, gb_ref, gc_ref):
    t = t_ref[...]
    g = (_dot3(hf_ref[...], w3f_ref[...]) * jnp.exp(-t * jnp.abs(dcf_ref[...]))
         + _dot3(hb_ref[...], w3b_ref[...]) * jnp.exp(-t * jnp.abs(dcb_ref[...])))
    g = g * lax.rsqrt(jnp.sum(g * g, axis=0, keepdims=True) + EPS)
    n_sub = g.shape[0] // r
    half = n_sub // 2
    fh = fh_ref[...]
    for p in range(r):
        spec = _dot(fh, g[p * n_sub:(p + 1) * n_sub].astype(BF16))
        ga_ref[p] = spec[:half]
        gb_ref[p] = spec[half:] - spec[:half]
        gc_ref[p] = spec[half:] + spec[:half]


def _hyena_filters(seq, r, w1, b1, w2, b2, w3, freq, decay, fwd):
    n_tot = 2 * seq
    n_sub = n_tot // r
    half = n_sub // 2
    z, t, sel_f, sel_b = _filter_positions(seq, r)
    pad = (-HY_POS_DIM) % SUBLANES
    z = jnp.asarray(np.pad(z, ((0, 0), (0, pad))), F32)
    w1p = jnp.pad(w1, ((0, pad), (0, 0)))
    kin = HY_POS_DIM + pad
    hid = HY_FILTER_HIDDEN
    tr = min(n_tot, 512)
    rows = lambda width: pl.BlockSpec((tr, width), lambda i: (i, 0))
    full = lambda shape: pl.BlockSpec(shape, lambda j: (0,) * len(shape))
    hf, hb = pl.pallas_call(
        _filter_ffn_kernel,
        out_shape=[jax.ShapeDtypeStruct((n_tot, hid), F32)] * 2,
        grid=(n_tot // tr,),
        in_specs=[rows(kin), rows(hid), rows(hid), full((kin, hid)), full((1, hid)), full((hid, hid)),
                  full((1, hid)), full((1, hid))],
        out_specs=[rows(hid)] * 2,
        compiler_params=_cparams(("parallel",)),
    )(z, jnp.asarray(sel_f, F32), jnp.asarray(sel_b, F32), w1p, b1.reshape(1, hid), w2, b2.reshape(1, hid),
      freq.reshape(1, hid))
    ncb = HY_OC // LANES
    return pl.pallas_call(
        functools.partial(_filter_spec_kernel, r),
        out_shape=[jax.ShapeDtypeStruct((r, half, HY_OC), F32)] * 3,
        grid=(ncb,),
        in_specs=[
            full((n_tot, hid)), full((n_tot, hid)), full((n_tot, LANES)),
            pl.BlockSpec((hid, LANES), lambda j: (0, j)),
            pl.BlockSpec((hid, LANES), lambda j: (0, j + ncb)),
            pl.BlockSpec((1, LANES), lambda j: (0, j)),
            pl.BlockSpec((1, LANES), lambda j: (0, j + ncb)),
            full((2 * half, n_sub)),
        ],
        out_specs=[pl.BlockSpec((r, half, LANES), lambda j: (0, 0, j))] * 3,
        compiler_params=_cparams(("parallel",)),
    )(hf, hb, jnp.asarray(t, F32), w3, w3, decay.reshape(1, -1), decay.reshape(1, -1), fwd)


def _hyena_kernel(r, hy0_ref, hy1_ref, hy2_ref, sw_ref, sb_ref, skip_ref,
                  ga0_ref, gb0_ref, gc0_ref, ga1_ref, gb1_ref, gc1_ref, fwd_ref, inv_ref, twr_ref, twi_ref,
                  o_ref, z_ref, ph_ref, rhs_ref, x_ref):
    seq = hy0_ref.shape[0]
    m_len = seq // r
    half = fwd_ref.shape[0] // 2
    row = lax.broadcasted_iota(jnp.int32, (m_len, LANES), 0)
    hy_refs = (hy0_ref, hy1_ref, hy2_ref)

    def load_phases(part):
        for j in range(r):
            ph_ref[j] = hy_refs[part][pl.ds(j, m_len, stride=r), :]

    def short_conv(part, j):
        w = sw_ref[:, part * LANES:(part + 1) * LANES]
        b = sb_ref[:, part * LANES:(part + 1) * LANES]
        if j > 0:
            prev = ph_ref[j - 1]
        else:
            prev = jnp.where(row == 0, 0.0, pltpu.roll(ph_ref[r - 1], 1, axis=0))
        if j < r - 1:
            nxt = ph_ref[j + 1]
        else:
            nxt = jnp.where(row == m_len - 1, 0.0, pltpu.roll(ph_ref[0], m_len - 1, axis=0))
        return prev * w[0:1] + ph_ref[j] * w[1:2] + nxt * w[2:3] + b

    load_phases(0)
    for j in range(r):
        z_ref[j] = short_conv(0, j)

    for o, (ga_ref, gb_ref, gc_ref) in enumerate(((ga0_ref, gb0_ref, gc0_ref), (ga1_ref, gb1_ref, gc1_ref))):
        for j in range(r):
            rhs_ref[:, j * LANES:(j + 1) * LANES] = z_ref[j].astype(BF16)
        x_ref[...] = _dot(fwd_ref[...], rhs_ref[...])
        dc = [x_ref[0:1, j * LANES:(j + 1) * LANES] for j in range(r)]
        ny = [x_ref[half:half + 1, j * LANES:(j + 1) * LANES] for j in range(r)]
        y_dc, y_ny = [], []
        for j in range(r):
            a = jnp.zeros((1, LANES), F32)
            c = jnp.zeros((1, LANES), F32)
            for jp in range(r):
                p = (j - jp) % r
                a = a + ga_ref[p, 0:1, :] * dc[jp]
                t = (ga_ref[p, 0:1, :] + gb_ref[p, 0:1, :]) * ny[jp]
                c = c + t if jp <= j else c - t
            y_dc.append(a)
            y_ny.append(c)

        def mix(i, carry):
            r0 = pl.multiple_of(i * SUBLANES, SUBLANES)
            rows_re = pl.ds(r0, SUBLANES)
            rows_im = pl.ds(half + r0, SUBLANES)
            xr = [x_ref[rows_re, j * LANES:(j + 1) * LANES] for j in range(r)]
            xi = [x_ref[rows_im, j * LANES:(j + 1) * LANES] for j in range(r)]
            xs = [a + b for a, b in zip(xr, xi)]
            wr = twr_ref[rows_re, :]
            wi = twi_ref[rows_re, :]
            for j in range(r):
                acc = {}
                for jp in range(r):
                    p = (j - jp) % r
                    k1 = ga_ref[p, rows_re, :] * xs[jp]
                    k2 = gb_ref[p, rows_re, :] * xr[jp]
                    k3 = gc_ref[p, rows_re, :] * xi[jp]
                    side = jp <= j
                    ks = (k1, k2, k3)
                    acc[side] = ks if side not in acc else tuple(a + b for a, b in zip(acc[side], ks))
                pr = acc[True][0] - acc[True][2]
                pi = acc[True][0] + acc[True][1]
                if False in acc:
                    qr = acc[False][0] - acc[False][2]
                    qi = acc[False][0] + acc[False][1]
                    pr = pr + wr * qr - wi * qi
                    pi = pi + wr * qi + wi * qr
                x_ref[rows_re, j * LANES:(j + 1) * LANES] = pr
                x_ref[rows_im, j * LANES:(j + 1) * LANES] = pi
            return carry

        lax.fori_loop(0, half // SUBLANES, mix, 0)
        for j in range(r):
            x_ref[0:1, j * LANES:(j + 1) * LANES] = y_dc[j]
            x_ref[half:half + 1, j * LANES:(j + 1) * LANES] = y_ny[j]
        y = _dot(inv_ref[...], x_ref[...].astype(BF16))
        sk = skip_ref[o:o + 1, :]
        load_phases(o + 1)
        for j in range(r):
            z_ref[j] = short_conv(o + 1, j) * (y[:, j * LANES:(j + 1) * LANES] + z_ref[j] * sk)
    for j in range(r):
        o_ref[pl.ds(j, m_len, stride=r), :] = z_ref[j]


def _hyena_direct_kernel(hy_ref, sw_ref, sb_ref, skip_ref, ga_ref, gb_ref, fwd_ref, inv_ref, o_ref):
    bb, seq, _ = hy_ref.shape
    half = fwd_ref.shape[0] // 2
    w = HYENA_WIDTH
    row = lax.broadcasted_iota(jnp.int32, (seq, w), 0)
    is_dc = lax.broadcasted_iota(jnp.int32, (half, w), 0) == 0

    def short_conv(b, part):
        cols = slice(part * w, (part + 1) * w)
        x = hy_ref[b, :, cols]
        prev = jnp.where(row == 0, 0.0, pltpu.roll(x, 1, axis=0))
        nxt = jnp.where(row == seq - 1, 0.0, pltpu.roll(x, seq - 1, axis=0))
        return prev * sw_ref[0:1, cols] + x * sw_ref[1:2, cols] + nxt * sw_ref[2:3, cols] + sb_ref[:, cols]

    for b in range(bb):
        z = short_conv(b, 0)
        for o in range(HY_ORDER):
            cols = slice(o * w, (o + 1) * w)
            x = _dot(fwd_ref[...], z.astype(BF16))
            xr, xi = x[:half], x[half:]
            gr = ga_ref[0, :, cols]
            gi = gb_ref[0, :, cols] + gr
            vr = gr * xr - jnp.where(is_dc, 0.0, gi * xi)
            vi = jnp.where(is_dc, gi * xi, gr * xi + gi * xr)
            y = _dot(inv_ref[...], jnp.concatenate([vr, vi], axis=0).astype(BF16))
            z = short_conv(b, o + 1) * (y + z * skip_ref[o:o + 1, :])
        o_ref[b] = z


def _hyena_direct(hy, short_w, short_b, skip, ga, gb, fwd, inv):
    nb, seq, _ = hy.shape
    bb = _pick_tile(nb, 4)
    const = lambda a: pl.BlockSpec(a.shape, lambda i: (0,) * a.ndim)
    short_b = short_b.reshape(1, -1)
    return pl.pallas_call(
        _hyena_direct_kernel,
        out_shape=jax.ShapeDtypeStruct((nb, seq, HYENA_WIDTH), F32),
        grid=(nb // bb,),
        in_specs=[pl.BlockSpec((bb, seq, HY_IN), lambda i: (i, 0, 0)), const(short_w), const(short_b), const(skip),
                  const(ga), const(gb), const(fwd), const(inv)],
        out_specs=pl.BlockSpec((bb, seq, HYENA_WIDTH), lambda i: (i, 0, 0)),
        compiler_params=_cparams(("parallel",)),
    )(hy, short_w, short_b, skip, ga, gb, fwd, inv)


def _hyena(hy, short_w, short_b, skip, spectra, fwd, inv, tw_r, tw_i, r):
    ga, gb, gc = spectra
    if r == 1:
        return _hyena_direct(hy, short_w, short_b, skip, ga, gb, fwd, inv)
    nb, seq, _ = hy.shape
    m_len = seq // r
    n_half2 = fwd.shape[0]
    half = n_half2 // 2
    ncb = HYENA_WIDTH // LANES
    parts = HY_ORDER + 1
    once = pl.Buffered(1)
    hy_spec = lambda part: pl.BlockSpec((None, seq, LANES), lambda c, b: (b, 0, part * ncb + c))
    g_spec = lambda o: pl.BlockSpec((r, half, LANES), lambda c, b: (0, 0, o * ncb + c), pipeline_mode=once)
    const = lambda shape: pl.BlockSpec(shape, lambda c, b: (0,) * len(shape), pipeline_mode=once)
    sw = short_w.reshape(3, parts, ncb, LANES).transpose(2, 0, 1, 3).reshape(ncb, 3, parts * LANES)
    sb = short_b.reshape(1, parts, ncb, LANES).transpose(2, 0, 1, 3).reshape(ncb, 1, parts * LANES)
    return pl.pallas_call(
        functools.partial(_hyena_kernel, r),
        out_shape=jax.ShapeDtypeStruct((nb, seq, HYENA_WIDTH), F32),
        grid=(ncb, nb),
        in_specs=[
            hy_spec(0), hy_spec(1), hy_spec(2),
            pl.BlockSpec((None, 3, parts * LANES), lambda c, b: (c, 0, 0)),
            pl.BlockSpec((None, 1, parts * LANES), lambda c, b: (c, 0, 0)),
            pl.BlockSpec((HY_ORDER, LANES), lambda c, b: (0, c)),
            g_spec(0), g_spec(0), g_spec(0), g_spec(1), g_spec(1), g_spec(1),
            const((n_half2, m_len)), const((m_len, n_half2)),
            const((half, LANES)), const((half, LANES)),
        ],
        out_specs=pl.BlockSpec((None, seq, LANES), lambda c, b: (b, 0, c)),
        scratch_shapes=[
            pltpu.VMEM((r, m_len, LANES), F32),
            pltpu.VMEM((r, m_len, LANES), F32),
            pltpu.VMEM((m_len, r * LANES), BF16),
            pltpu.VMEM((n_half2, r * LANES), F32),
        ],
        compiler_params=_cparams(("parallel", "parallel")),
    )(hy, hy, hy, sw, sb, skip, ga, gb, gc, ga, gb, gc, fwd, inv, tw_r, tw_i)


def _route(logits):
    lane = lax.broadcasted_iota(jnp.int32, logits.shape, 1).astype(F32)
    big = jnp.float32(1e9)
    is_grp = (lane >= N_EXPERTS) & (lane < N_EXPERTS + N_GROUPS)
    gl = jnp.where(is_grp, logits, NEG_BIG)
    gmax = jnp.max(gl, axis=1, keepdims=True)
    gidx = jnp.min(jnp.where(gl == gmax, lane, big), axis=1, keepdims=True) - N_EXPERTS
    den = jnp.sum(jnp.where(is_grp, jnp.exp(gl - gmax), 0.0), axis=1, keepdims=True)
    pg_top = 1.0 / den
    lo = gidx * EXPERTS_PER_GROUP
    sel = jnp.where((lane >= lo) & (lane < lo + EXPERTS_PER_GROUP), logits, NEG_BIG)
    m1 = jnp.max(sel, axis=1, keepdims=True)
    i1 = jnp.min(jnp.where(sel == m1, lane, big), axis=1, keepdims=True)
    sel2 = jnp.where(lane == i1, NEG_BIG, sel)
    m2 = jnp.max(sel2, axis=1, keepdims=True)
    i2 = jnp.min(jnp.where(sel2 == m2, lane, big), axis=1, keepdims=True)
    e2 = jnp.exp(m2 - m1)
    w1 = pg_top / (1.0 + e2)
    w2 = pg_top * e2 / (1.0 + e2)
    comb = jnp.where(lane == i1, w1, 0.0) + jnp.where(lane == i2, w2, 0.0)
    return comb + jnp.where(lane == GROUP_ID_LANE, gidx, 0.0)


def _outproj_kernel(attn_ref, hyz_ref, x_ref, mod_ref, wo_ref, hg_ref, bd_ref, lng_ref, lnb_ref,
                    rwh_ref, rwl_ref, rb_ref, x1_ref, tok_ref, rec_ref):
    m = mod_ref[...]
    tm = x_ref.shape[0]
    n_sub = _proj_subtiles(tm)
    ts = tm // n_sub
    half = wo_ref.shape[0] // 2
    for t in range(n_sub):
        rows = slice(t * ts, (t + 1) * ts)
        z = hyz_ref[rows, :]
        ms = _dot((z * z).astype(BF16), bd_ref[...])
        zn = (z * lax.rsqrt(ms + EPS) * hg_ref[...]).astype(BF16)
        mix = _dot(attn_ref[rows, :], wo_ref[:half, :]) + _dot(zn, wo_ref[half:, :])
        x1 = _layer_norm(DEEPNORM_ALPHA * x_ref[rows, :] + m[2:3] * mix, lng_ref[...], lnb_ref[...])
        h2 = x1 * (1.0 + m[4:5]) + m[3:4]
        x1_ref[rows, :] = x1
        tok_ref[rows, :D_MODEL] = h2
        hh, hl = _split_bf16(h2)
        logits = _dot(hh, rwh_ref[...]) + _dot(hl, rwh_ref[...]) + _dot(hh, rwl_ref[...]) + rb_ref[...]
        record = _route(logits)
        tok_ref[rows, D_MODEL:] = record
        rec_ref[rows, :] = record


def _outproj(attn, hyz, x, mod, w_out, hy_gain, bd_hy, ln_g, ln_b, rw_hi, rw_lo, rb, latent, tm):
    nb, seq, _ = x.shape
    mod_map = (lambda b, i: (b, 0, 0)) if latent else (lambda b, i: (0, 0, 0))
    const = lambda shape: pl.BlockSpec(shape, lambda b, i: (0,) * len(shape))
    tok = lambda width: pl.BlockSpec((None, tm, width), lambda b, i: (b, i, 0))
    return pl.pallas_call(
        _outproj_kernel,
        out_shape=[
            jax.ShapeDtypeStruct((nb, seq, D_MODEL), F32),
            jax.ShapeDtypeStruct((nb, seq, D_MODEL + LANES), F32),
            jax.ShapeDtypeStruct((nb, seq, LANES), F32),
        ],
        grid=(nb, seq // tm),
        in_specs=[
            tok(Q_DIM), tok(HYENA_WIDTH), tok(D_MODEL),
            pl.BlockSpec((None, 6, D_MODEL), mod_map),
            const((D_MODEL, D_MODEL)), const((1, HYENA_WIDTH)), const((HYENA_WIDTH, HYENA_WIDTH)),
            const((1, D_MODEL)), const((1, D_MODEL)),
            const((D_MODEL, LANES)), const((D_MODEL, LANES)), const((1, LANES)),
        ],
        out_specs=[tok(D_MODEL), tok(D_MODEL + LANES), tok(LANES)],
        compiler_params=_cparams(("parallel", "parallel")),
    )(attn, hyz, x, mod, w_out, hy_gain, bd_hy, ln_g, ln_b, rw_hi, rw_lo, rb)


def _gather_rows(table, idx):
    n, d = idx.shape[0], table.shape[1]
    info = plsc.get_sparse_core_info()
    workers = info.num_cores * info.num_subcores
    per_worker = n // workers
    assert per_worker * workers == n and per_worker % GATHER_ROWS == 0
    mesh = plsc.VectorSubcoreMesh(core_axis_name="c", subcore_axis_name="s")

    @functools.partial(
        pl.kernel, mesh=mesh,
        out_type=jax.ShapeDtypeStruct((n, d), table.dtype),
        scratch_types=[pltpu.VMEM((GATHER_ROWS,), jnp.int32), pltpu.VMEM((GATHER_ROWS, d), table.dtype),
                       pltpu.SemaphoreType.DMA],
    )
    def gather(table_hbm, idx_hbm, out_hbm, idx_v, rows_v, sem):
        wid = lax.axis_index("s") * info.num_cores + lax.axis_index("c")
        base = wid * per_worker

        @pl.loop(0, per_worker // GATHER_ROWS)
        def _(j):
            off = base + j * GATHER_ROWS
            pltpu.sync_copy(idx_hbm.at[pl.ds(off, GATHER_ROWS)], idx_v)
            pltpu.async_copy(table_hbm.at[idx_v], rows_v, sem).wait()
            pltpu.sync_copy(rows_v, out_hbm.at[pl.ds(off, GATHER_ROWS)])

    return gather(table, idx)


def _sort_plan(record, tile):
    tokens = record.shape[0]
    gid = record[:, GROUP_ID_LANE].astype(jnp.int32)
    onehot = (gid[:, None] == jnp.arange(N_GROUPS, dtype=jnp.int32)[None, :]).astype(jnp.int32)
    csum = jnp.cumsum(onehot, axis=0)
    counts = csum[-1]
    rank = jnp.sum(csum * onehot, axis=1) - 1
    padded = (counts + tile - 1) // tile * tile
    ends = jnp.cumsum(padded)
    starts = ends - padded
    pos = jnp.sum(starts[None, :] * onehot, axis=1) + rank
    n_pad = tokens + N_GROUPS * tile
    filler = jnp.arange(n_pad - tokens, dtype=jnp.int32)
    filler_gid = jnp.sum((filler[:, None] >= jnp.cumsum(padded - counts)[None, :]).astype(jnp.int32), axis=1)
    _, src = lax.sort((jnp.concatenate([gid, filler_gid]),
                       jnp.concatenate([jnp.arange(tokens, dtype=jnp.int32), jnp.zeros_like(filler)])),
                      num_keys=1, is_stable=True)
    tile_start = jnp.arange(n_pad // tile, dtype=jnp.int32) * tile
    tile_group = jnp.minimum(jnp.sum((tile_start[:, None] >= ends[None, :]).astype(jnp.int32), axis=1), N_GROUPS - 1)
    tile_valid = (tile_start < ends[-1]).astype(jnp.int32)
    return pos, src, tile_group, tile_valid


def _moe_kernel(tg_ref, tv_ref, x_ref, wg_ref, wu_ref, wd_ref, o_ref):
    i = pl.program_id(0)

    @pl.when(tv_ref[i] > 0)
    def _():
        first = tg_ref[i] * EXPERTS_PER_GROUP
        x = x_ref[:, :D_MODEL].astype(BF16)
        rec = x_ref[:, D_MODEL:]
        lane = lax.broadcasted_iota(jnp.int32, rec.shape, 1)
        y = None
        for e in range(EXPERTS_PER_GROUP):
            c = jnp.sum(jnp.where(lane == first + e, rec, 0.0), axis=1, keepdims=True)
            h = _silu(_dot(x, wg_ref[e])) * _dot(x, wu_ref[e]) * c
            part = _dot(h.astype(BF16), wd_ref[e])
            y = part if y is None else y + part
        o_ref[...] = y

    @pl.when(tv_ref[i] == 0)
    def _():
        o_ref[...] = jnp.zeros_like(o_ref)


def _moe(sorted_tok, tile_group, tile_valid, wg, wu, wd):
    rows = sorted_tok.shape[0]
    tile = MOE_TILE
    grp = lambda shape: pl.BlockSpec(shape, lambda i, tg, tv: (tg[i], 0, 0))
    return pl.pallas_call(
        _moe_kernel,
        out_shape=jax.ShapeDtypeStruct((rows, D_MODEL), F32),
        grid_spec=pltpu.PrefetchScalarGridSpec(
            num_scalar_prefetch=2,
            grid=(rows // tile,),
            in_specs=[
                pl.BlockSpec((tile, D_MODEL + LANES), lambda i, tg, tv: (i, 0)),
                grp((EXPERTS_PER_GROUP, D_MODEL, EXPERT_FF)), grp((EXPERTS_PER_GROUP, D_MODEL, EXPERT_FF)),
                grp((EXPERTS_PER_GROUP, EXPERT_FF, D_MODEL)),
            ],
            out_specs=pl.BlockSpec((tile, D_MODEL), lambda i, tg, tv: (i, 0)),
        ),
        compiler_params=_cparams(("arbitrary",)),
    )(tile_group, tile_valid, sorted_tok, wg, wu, wd)


def _final_norm_kernel(x1_ref, y_ref, mod_ref, lng_ref, lnb_ref, o_ref):
    o_ref[...] = _layer_norm(DEEPNORM_ALPHA * x1_ref[...] + mod_ref[5:6, :] * y_ref[...], lng_ref[...], lnb_ref[...])


def _final_norm(x1, y, mod, ln_g, ln_b, latent, tm):
    nb, seq, _ = x1.shape
    mod_map = (lambda b, i: (b, 0, 0)) if latent else (lambda b, i: (0, 0, 0))
    tok = pl.BlockSpec((None, tm, D_MODEL), lambda b, i: (b, i, 0))
    const = pl.BlockSpec((1, D_MODEL), lambda b, i: (0, 0))
    return pl.pallas_call(
        _final_norm_kernel,
        out_shape=jax.ShapeDtypeStruct((nb, seq, D_MODEL), F32),
        grid=(nb, seq // tm),
        in_specs=[tok, tok, pl.BlockSpec((None, 6, D_MODEL), mod_map), const, const],
        out_specs=tok,
        compiler_params=_cparams(("parallel", "parallel")),
    )(x1, y, mod, ln_g, ln_b)


def _block_diag_mean(width, group):
    idx = np.arange(width) // group
    return jnp.asarray((idx[:, None] == idx[None, :]).astype(np.float32) / group, BF16)


def _pick_tile(n, target):
    t = min(n, target)
    while n % t:
        t //= 2
    return t


def _trunk_layer(x, mod, p, latent, ctx_k=None, ctx_v=None):
    nb, seq, _ = x.shape
    tm = _pick_tile(seq, 1024)
    q, kt, vx, hy, *kv_new = _inproj(x, mod, p['w_in'], p['qk_gain'], p['bd_qk'], latent, tm)
    if latent:
        ckt = jnp.transpose(ctx_k, (0, 2, 3, 1)).astype(BF16)
        cv = jnp.transpose(ctx_v, (0, 2, 1, 3))
        cvx = jnp.concatenate([cv, jnp.ones_like(cv)], axis=-1).astype(BF16)
    else:
        ckt = cvx = None
    attn = _attention(q, kt, vx, p['attn_gain'], ckt, cvx)

    r = _hyena_decimation(seq)
    fwd, inv, tw_r, tw_i = _dft_tables(seq, r)
    fwd = jnp.asarray(fwd, F32).astype(BF16)
    spectra = _hyena_filters(seq, r, p['hy_f_w1'], p['hy_f_b1'], p['hy_f_w2'], p['hy_f_b2'], p['hy_f_w3'],
                              p['hy_freq'], p['hy_decay'], fwd)
    hyz = _hyena(hy, p['hy_short_w'], p['hy_short_b'], p['hy_skip'], spectra,
                 fwd[:, :seq // r], jnp.asarray(inv, F32).astype(BF16), jnp.asarray(tw_r, F32),
                 jnp.asarray(tw_i, F32), r)

    x1, tok, record = _outproj(attn, hyz, x, mod, p['w_out'], p['hy_gain'], p['bd_hy'], p['ln1_g'], p['ln1_b'],
                               p['rw_hi'], p['rw_lo'], p['rb'], latent, tm)
    tok = tok.reshape(nb * seq, D_MODEL + LANES)
    pos, src, tile_group, tile_valid = _sort_plan(record.reshape(nb * seq, LANES), MOE_TILE)
    y_sorted = _moe(_gather_rows(tok, src), tile_group, tile_valid, p['wg'], p['wu'], p['wd'])
    y = _gather_rows(y_sorted, pos).reshape(nb, seq, D_MODEL)
    return _final_norm(x1, y, mod, p['ln2_g'], p['ln2_b'], latent, tm), kv_new


def _prepare(w_in, q_gain, k_gain, attn_out_gain, hy_short_w, hy_short_b, hy_f_w1, hy_f_b1, hy_f_w2, hy_f_b2,
             hy_f_w3, hy_freq, hy_decay, hy_skip, hy_out_gain, w_out, ln1_g, ln1_b, router_grp_w, router_grp_b,
             router_exp_w, router_exp_b, exp_w_gate, exp_w_up, exp_w_down, ln2_g, ln2_b, l):
    row = lambda a: a.reshape(1, -1)
    rw = jnp.concatenate([router_exp_w[l], router_grp_w[l]], axis=1)
    rw = jnp.pad(rw, ((0, 0), (0, LANES - rw.shape[1])))
    rb = jnp.concatenate([router_exp_b[l], router_grp_b[l]])
    rb = jnp.pad(rb, (0, LANES - rb.shape[0]))
    rw_hi, rw_lo = _split_bf16(rw)
    return {
        'w_in': w_in[l].astype(BF16),
        'qk_gain': row(jnp.concatenate([jnp.tile(q_gain[l], N_HEADS), jnp.tile(k_gain[l], N_KV_HEADS)])),
        'bd_qk': _block_diag_mean(QK_DIM, HEAD_DIM),
        'attn_gain': row(attn_out_gain[l]),
        'hy_short_w': hy_short_w[l], 'hy_short_b': hy_short_b[l],
        'hy_f_w1': hy_f_w1[l], 'hy_f_b1': hy_f_b1[l], 'hy_f_w2': hy_f_w2[l], 'hy_f_b2': hy_f_b2[l],
        'hy_f_w3': hy_f_w3[l], 'hy_freq': hy_freq[l], 'hy_decay': hy_decay[l], 'hy_skip': hy_skip[l],
        'hy_gain': row(hy_out_gain[l]),
        'bd_hy': _block_diag_mean(HYENA_WIDTH, HY_GROUP_DIM),
        'w_out': w_out[l].astype(BF16),
        'ln1_g': row(ln1_g[l]), 'ln1_b': row(ln1_b[l]),
        'rw_hi': rw_hi, 'rw_lo': rw_lo, 'rb': row(rb),
        'wg': exp_w_gate[l].astype(BF16), 'wu': exp_w_up[l].astype(BF16), 'wd': exp_w_down[l].astype(BF16),
        'ln2_g': row(ln2_g[l]), 'ln2_b': row(ln2_b[l]),
    }


def kernel(x_prompt, x_sample, c, cache_k, cache_v, c_ctx, w_mod, b_mod, w_in, q_gain, k_gain, attn_out_gain, hy_short_w, hy_short_b, hy_f_w1, hy_f_b1, hy_f_w2, hy_f_b2, hy_f_w3, hy_freq, hy_decay, hy_skip, hy_out_gain, w_out, ln1_g, ln1_b, router_grp_w, router_grp_b, router_exp_w, router_exp_b, exp_w_gate, exp_w_up, exp_w_down, ln2_g, ln2_b):
    depth = w_mod.shape[0]
    n_lat = c.shape[0]
    cond = jnp.concatenate([c, c_ctx[None, :]], axis=0)
    rows = -(-cond.shape[0] // SUBLANES) * SUBLANES
    cond = jnp.pad(cond, ((0, rows - cond.shape[0]), (0, 0)))
    y_prompt, y_sample = x_prompt, x_sample
    ks_new, vs_new = [], []
    for l in range(depth):
        p = _prepare(w_in, q_gain, k_gain, attn_out_gain, hy_short_w, hy_short_b, hy_f_w1, hy_f_b1, hy_f_w2,
                     hy_f_b2, hy_f_w3, hy_freq, hy_decay, hy_skip, hy_out_gain, w_out, ln1_g, ln1_b,
                     router_grp_w, router_grp_b, router_exp_w, router_exp_b, exp_w_gate, exp_w_up, exp_w_down,
                     ln2_g, ln2_b, l)
        mod = _adaln(cond, w_mod[l], b_mod[l])
        mod_lat = mod[:n_lat].reshape(n_lat, 6, D_MODEL)
        mod_ctx = mod[n_lat:n_lat + 1].reshape(1, 6, D_MODEL)
        y_prompt, (v_new, k_new) = _trunk_layer(y_prompt, mod_ctx, p, False)
        ks_new.append(k_new.reshape(k_new.shape[:2] + (N_KV_HEADS, HEAD_DIM)))
        vs_new.append(v_new.reshape(v_new.shape[:2] + (N_KV_HEADS, HEAD_DIM)))
        y_sample, _ = _trunk_layer(y_sample, mod_lat, p, True, cache_k[:, l], cache_v[:, l])
    return (y_prompt, y_sample, jnp.stack(ks_new, axis=1), jnp.stack(vs_new, axis=1))
```

```python
import functools

import numpy as np
import jax
import jax.numpy as jnp
from jax import lax
from jax.experimental import pallas as pl
from jax.experimental.pallas import tpu as pltpu
from jax.experimental.pallas import tpu_sc as plsc

F32 = jnp.float32
BF16 = jnp.bfloat16

D_MODEL = 1024
GRID_W = 64
HEAD_DIM = 64
N_HEADS = 8
N_KV_HEADS = 2
GQA_GROUP = N_HEADS // N_KV_HEADS
Q_DIM = N_HEADS * HEAD_DIM
KV_DIM = N_KV_HEADS * HEAD_DIM
QK_DIM = Q_DIM + KV_DIM
HYENA_WIDTH = 512
HY_ORDER = 2
HY_IN = (HY_ORDER + 1) * HYENA_WIDTH
HY_GROUP_DIM = 64
HY_BANDS = 16
HY_POS_DIM = 1 + 2 * HY_BANDS
HY_FILTER_HIDDEN = 64
HY_OC = HY_ORDER * HYENA_WIDTH
IN_WIDTH = Q_DIM + 2 * KV_DIM + HY_IN
ROPE_THETA = 10000.0
ROPE_FREQS = HEAD_DIM // 4
N_GROUPS = 4
EXPERTS_PER_GROUP = 8
N_EXPERTS = N_GROUPS * EXPERTS_PER_GROUP
EXPERT_FF = D_MODEL // 4
TOP_K = 2
DEPTH = 1
DEEPNORM_ALPHA = (2.0 * DEPTH) ** 0.25
EPS = 1e-6

LANES = 128
SUBLANES = 8
VMEM_LIMIT = 56 * 1024 * 1024
MOE_TILE = 512
GATHER_ROWS = 32
EXPERT_ID_LANE = N_EXPERTS
PROJ_SUBTILES = 2
PROJ_MIN_SUBTILE = 256
ATTN_SUBTILES = 4
ATTN_LONG_SEQ = 1024
NEG_BIG = -1e30
LOG2_E = 1.4426950408889634


def _cparams(sem):
    return pltpu.CompilerParams(dimension_semantics=sem, vmem_limit_bytes=VMEM_LIMIT)


def _proj_subtiles(tm):
    return PROJ_SUBTILES if tm // PROJ_SUBTILES >= PROJ_MIN_SUBTILE else 1


def _split_bf16(a):
    hi = a.astype(BF16)
    lo = (a - hi.astype(F32)).astype(BF16)
    return hi, lo


def _dot(a, b):
    return jnp.dot(a, b, preferred_element_type=F32)


def _dot3(a, b):
    ah, al = _split_bf16(a)
    bh, bl = _split_bf16(b)
    return _dot(ah, bh) + _dot(al, bh) + _dot(ah, bl)


def _silu(x):
    return x / (1.0 + jnp.exp(-x))


def _layer_norm(y, g, b):
    mu = jnp.mean(y, axis=-1, keepdims=True)
    yc = y - mu
    var = jnp.mean(yc * yc, axis=-1, keepdims=True)
    return yc * lax.rsqrt(var + EPS) * g + b


def _adaln_kernel(c_ref, w_ref, b_ref, o_ref):
    o_ref[...] = _dot3(_silu(c_ref[...]), w_ref[...]) + b_ref[...]


def _adaln(cond, w_mod, b_mod):
    rows = cond.shape[0]
    n = w_mod.shape[1]
    tn = 1536
    return pl.pallas_call(
        _adaln_kernel,
        out_shape=jax.ShapeDtypeStruct((rows, n), F32),
        grid=(n // tn,),
        in_specs=[
            pl.BlockSpec((rows, D_MODEL), lambda j: (0, 0)),
            pl.BlockSpec((D_MODEL, tn), lambda j: (0, j)),
            pl.BlockSpec((1, tn), lambda j: (0, j)),
        ],
        out_specs=pl.BlockSpec((rows, tn), lambda j: (0, j)),
        compiler_params=_cparams(("arbitrary",)),
    )(cond, w_mod, b_mod.reshape(1, n))


def _rope_tables(seq):
    t = np.arange(seq)
    rows = (t // GRID_W).astype(np.float64)
    cols = (t % GRID_W).astype(np.float64)
    inv_freq = ROPE_THETA ** (-np.arange(ROPE_FREQS, dtype=np.float64) / ROPE_FREQS)
    d = np.arange(LANES) % HEAD_DIM
    axis = d // (2 * ROPE_FREQS)
    f = d % ROPE_FREQS
    pos = np.where(axis[None, :] == 0, rows[:, None], cols[:, None])
    ang = pos * inv_freq[f][None, :]
    first = (d % (2 * ROPE_FREQS)) < ROPE_FREQS
    cos = np.cos(ang)
    sin = np.where(first[None, :], -np.sin(ang), np.sin(ang))
    return jnp.asarray(cos, F32), jnp.asarray(sin, F32)


def _inproj_kernel(latent, x_ref, mod_ref, w_ref, gain_ref, bd_ref, *rest):
    if latent:
        cos_ref, sin_ref, q_ref, kt_ref, vx_ref, hy_ref = rest
    else:
        q_ref, kt_ref, vx_ref, hy_ref, v_ref, knat_ref = rest
    m = mod_ref[...]
    tm = x_ref.shape[0]
    n_sub = _proj_subtiles(tm)
    ts = tm // n_sub
    for t in range(n_sub):
        rows = slice(t * ts, (t + 1) * ts)
        h = x_ref[rows, :] * (1.0 + m[1:2]) + m[0:1]
        proj = _dot(h.astype(BF16), w_ref[...])
        qk = proj[:, :QK_DIM]
        ms = _dot((qk * qk).astype(BF16), bd_ref[...])
        qk = qk * lax.rsqrt(ms + EPS) * gain_ref[...]
        if not latent:
            knat_ref[rows, :] = qk[:, Q_DIM:]
        else:
            cos = cos_ref[rows, :]
            sin = sin_ref[rows, :]
            lane = lax.broadcasted_iota(jnp.int32, cos.shape, 1)
            first = (lane % (2 * ROPE_FREQS)) < ROPE_FREQS
            chunks = []
            for c in range(QK_DIM // LANES):
                xc = qk[:, c * LANES:(c + 1) * LANES]
                below = pltpu.roll(xc, ROPE_FREQS, axis=1)
                above = pltpu.roll(xc, LANES - ROPE_FREQS, axis=1)
                chunks.append(xc * cos + jnp.where(first, above, below) * sin)
            qk = jnp.concatenate(chunks, axis=1)
        qs = (qk[:, :Q_DIM] * (HEAD_DIM ** -0.5 * LOG2_E)).astype(BF16)
        for hd in range(N_HEADS):
            q_ref[hd, rows, :] = qs[:, hd * HEAD_DIM:(hd + 1) * HEAD_DIM]
        kt = qk[:, Q_DIM:].T
        for kh in range(N_KV_HEADS):
            kt_ref[kh, :, rows] = kt[kh * HEAD_DIM:(kh + 1) * HEAD_DIM].astype(BF16)
        v = proj[:, QK_DIM:QK_DIM + KV_DIM]
        low = lax.broadcasted_iota(jnp.int32, v.shape, 1) < HEAD_DIM
        vx_ref[0, rows, :] = jnp.where(low, v, 1.0).astype(BF16)
        vx_ref[1, rows, :] = jnp.where(low, pltpu.roll(v, HEAD_DIM, axis=1), 1.0).astype(BF16)
        if not latent:
            v_ref[rows, :] = v
        hy_ref[rows, :] = proj[:, QK_DIM + KV_DIM:]


def _inproj(x, mod, w_in, qk_gain, bd_qk, latent, tm):
    nb, seq, _ = x.shape
    grid = (nb, seq // tm)
    mod_map = (lambda b, i: (b, 0, 0)) if latent else (lambda b, i: (0, 0, 0))
    in_specs = [
        pl.BlockSpec((None, tm, D_MODEL), lambda b, i: (b, i, 0)),
        pl.BlockSpec((None, 6, D_MODEL), mod_map),
        pl.BlockSpec((D_MODEL, IN_WIDTH), lambda b, i: (0, 0)),
        pl.BlockSpec((1, QK_DIM), lambda b, i: (0, 0)),
        pl.BlockSpec((QK_DIM, QK_DIM), lambda b, i: (0, 0)),
    ]
    args = [x, mod, w_in, qk_gain, bd_qk]
    out_shape = [
        jax.ShapeDtypeStruct((nb, N_HEADS, seq, HEAD_DIM), BF16),
        jax.ShapeDtypeStruct((nb, N_KV_HEADS, HEAD_DIM, seq), BF16),
        jax.ShapeDtypeStruct((nb, N_KV_HEADS, seq, KV_DIM), BF16),
        jax.ShapeDtypeStruct((nb, seq, HY_IN), F32),
    ]
    out_specs = [
        pl.BlockSpec((None, N_HEADS, tm, HEAD_DIM), lambda b, i: (b, 0, i, 0)),
        pl.BlockSpec((None, N_KV_HEADS, HEAD_DIM, tm), lambda b, i: (b, 0, 0, i)),
        pl.BlockSpec((None, N_KV_HEADS, tm, KV_DIM), lambda b, i: (b, 0, i, 0)),
        pl.BlockSpec((None, tm, HY_IN), lambda b, i: (b, i, 0)),
    ]
    if latent:
        cos, sin = _rope_tables(seq)
        in_specs += [pl.BlockSpec((tm, LANES), lambda b, i: (i, 0))] * 2
        args += [cos, sin]
    else:
        out_shape += [jax.ShapeDtypeStruct((nb, seq, KV_DIM), F32)] * 2
        out_specs += [pl.BlockSpec((None, tm, KV_DIM), lambda b, i: (b, i, 0))] * 2
    return pl.pallas_call(
        functools.partial(_inproj_kernel, latent),
        out_shape=out_shape,
        grid=grid,
        in_specs=in_specs,
        out_specs=out_specs,
        compiler_params=_cparams(("parallel", "parallel")),
    )(*args)


def _attn_kernel(n_ctx, chunk, subtiles, q_ref, kt_ref, v_ref, gain_ref, *rest):
    if n_ctx:
        ckt_ref, cv_ref, o_ref = rest
    else:
        (o_ref,) = rest
    bb, heads, tq, _ = q_ref.shape
    seq = kt_ref.shape[-1]
    g = GQA_GROUP
    ts = tq // subtiles
    for b in range(bb):
        for k in range(heads // g):
            pieces = [(kt_ref[b, k, :, c * chunk:(c + 1) * chunk], v_ref[b, k, c * chunk:(c + 1) * chunk, :])
                      for c in range(seq // chunk)]
            if n_ctx:
                pieces.append((ckt_ref[b, k], cv_ref[b, k]))
            for t in range(subtiles):
                rows = slice(t * ts, (t + 1) * ts)
                qs = q_ref[b, k * g:(k + 1) * g, rows, :].reshape(g * ts, HEAD_DIM)
                m = acc = None
                for kt_c, v_c in pieces:
                    s = _dot(qs, kt_c)
                    row_max = jnp.max(s, axis=1, keepdims=True)
                    m_new = row_max if m is None else jnp.maximum(m, row_max)
                    pv = _dot(jnp.exp2(s - m_new).astype(BF16), v_c)
                    acc = pv if m is None else jnp.exp2(m - m_new) * acc + pv
                    m = m_new
                o = acc[:, :HEAD_DIM] / acc[:, HEAD_DIM:HEAD_DIM + 1]
                o = o * lax.rsqrt(jnp.mean(o * o, axis=1, keepdims=True) + EPS)
                for i in range(g):
                    cols = slice((k * g + i) * HEAD_DIM, (k * g + i + 1) * HEAD_DIM)
                    o_ref[b, rows, cols] = (o[i * ts:(i + 1) * ts] * gain_ref[:, cols]).astype(o_ref.dtype)


def _attention(q, kt, v, gain, ckt, cv):
    nb, _, seq, _ = q.shape
    n_ctx = 0 if ckt is None else ckt.shape[-1]
    chunk = _pick_tile(seq, 2048)
    if seq >= ATTN_LONG_SEQ:
        bb, kv, tq, subtiles = 1, 1, _pick_tile(seq, 512), ATTN_SUBTILES
    else:
        bb, kv, tq, subtiles = _pick_tile(nb, 4), N_KV_HEADS, seq, 1
    width = kv * GQA_GROUP * HEAD_DIM
    in_specs = [
        pl.BlockSpec((bb, kv * GQA_GROUP, tq, HEAD_DIM), lambda b, k, i: (b, k, i, 0)),
        pl.BlockSpec((bb, kv, HEAD_DIM, seq), lambda b, k, i: (b, k, 0, 0)),
        pl.BlockSpec((bb, kv, seq, KV_DIM), lambda b, k, i: (b, k, 0, 0)),
        pl.BlockSpec((1, width), lambda b, k, i: (0, k)),
    ]
    args = [q, kt, v, gain]
    if n_ctx:
        in_specs += [
            pl.BlockSpec((bb, kv, HEAD_DIM, n_ctx), lambda b, k, i: (b, k, 0, 0)),
            pl.BlockSpec((bb, kv, n_ctx, KV_DIM), lambda b, k, i: (b, k, 0, 0)),
        ]
        args += [ckt, cv]
    return pl.pallas_call(
        functools.partial(_attn_kernel, n_ctx, chunk, subtiles),
        out_shape=jax.ShapeDtypeStruct((nb, seq, Q_DIM), BF16),
        grid=(nb // bb, N_KV_HEADS // kv, seq // tq),
        in_specs=in_specs,
        out_specs=pl.BlockSpec((bb, tq, width), lambda b, k, i: (b, i, k)),
        compiler_params=_cparams(("parallel", "parallel", "parallel")),
    )(*args)


def _hyena_decimation(seq):
    return 8 if seq >= 2048 else 1


def _dft_tables(seq, r):
    n_sub = 2 * seq // r
    half = n_sub // 2
    k = np.arange(half, dtype=np.float64)[:, None]
    m = np.arange(n_sub, dtype=np.float64)[None, :]
    ang = 2.0 * np.pi * k * m / n_sub
    fwd = np.concatenate([np.cos(ang), -np.sin(ang)], axis=0)
    fwd[half] = np.cos(np.pi * m[0])
    inv = fwd.T.copy() * (2.0 / n_sub)
    inv[:, 0] *= 0.5
    inv[:, half] *= 0.5
    inv = inv[:seq // r]
    kk = np.arange(half, dtype=np.float64)[:, None] * np.ones((1, LANES))
    tw_r = np.cos(2.0 * np.pi * kk / n_sub)
    tw_i = -np.sin(2.0 * np.pi * kk / n_sub)
    return fwd, inv, tw_r, tw_i


def _filter_positions(seq, r):
    n_tot = 2 * seq
    n = (np.arange(n_tot // r)[None, :] * r + np.arange(r)[:, None]).reshape(-1)
    j = np.where(n < seq, n, n_tot - n)
    t = j.astype(np.float64) / seq
    bands = np.arange(1, HY_BANDS + 1, dtype=np.float64)
    ang = 2.0 * np.pi * t[:, None] * bands
    z = np.concatenate([t[:, None], np.sin(ang), np.cos(ang)], axis=-1)
    ones = np.ones((1, HY_FILTER_HIDDEN))
    sel_f = (n < seq).astype(np.float64)[:, None] * ones
    sel_b = (n > seq).astype(np.float64)[:, None] * ones
    return z, t[:, None] * np.ones((1, LANES)), sel_f, sel_b


def _filter_ffn_kernel(z_ref, self_ref, selb_ref, w1_ref, b1_ref, w2_ref, b2_ref, fr_ref, hf_ref, hb_ref):
    fr = fr_ref[...]
    h = jnp.sin(fr * (_dot3(z_ref[...], w1_ref[...]) + b1_ref[...]))
    h = jnp.sin(fr * (_dot3(h, w2_ref[...]) + b2_ref[...]))
    hf_ref[...] = h * self_ref[...]
    hb_ref[...] = h * selb_ref[...]


def _filter_spec_kernel(r, hf_ref, hb_ref, t_ref, w3f_ref, w3b_ref, dcf_ref, dcb_ref, fh_ref, ga_ref, gb_ref, gc_ref):
    t = t_ref[...]
    g = (_dot3(hf_ref[...], w3f_ref[...]) * jnp.exp(-t * jnp.abs(dcf_ref[...]))
         + _dot3(hb_ref[...], w3b_ref[...]) * jnp.exp(-t * jnp.abs(dcb_ref[...])))
    g = g * lax.rsqrt(jnp.sum(g * g, axis=0, keepdims=True) + EPS)
    n_sub = g.shape[0] // r
    half = n_sub // 2
    fh = fh_ref[...]
    for p in range(r):
        spec = _dot(fh, g[p * n_sub:(p + 1) * n_sub].astype(BF16))
        ga_ref[p] = spec[:half]
        gb_ref[p] = spec[half:] - spec[:half]
        gc_ref[p] = spec[half:] + spec[:half]


def _hyena_filters(seq, r, w1, b1, w2, b2, w3, freq, decay, fwd):
    n_tot = 2 * seq
    n_sub = n_tot // r
    half = n_sub // 2
    z, t, sel_f, sel_b = _filter_positions(seq, r)
    pad = (-HY_POS_DIM) % SUBLANES
    z = jnp.asarray(np.pad(z, ((0, 0), (0, pad))), F32)
    w1p = jnp.pad(w1, ((0, pad), (0, 0)))
    kin = HY_POS_DIM + pad
    hid = HY_FILTER_HIDDEN
    tr = min(n_tot, 512)
    rows = lambda width: pl.BlockSpec((tr, width), lambda i: (i, 0))
    full = lambda shape: pl.BlockSpec(shape, lambda j: (0,) * len(shape))
    hf, hb = pl.pallas_call(
        _filter_ffn_kernel,
        out_shape=[jax.ShapeDtypeStruct((n_tot, hid), F32)] * 2,
        grid=(n_tot // tr,),
        in_specs=[rows(kin), rows(hid), rows(hid), full((kin, hid)), full((1, hid)), full((hid, hid)),
                  full((1, hid)), full((1, hid))],
        out_specs=[rows(hid)] * 2,
        compiler_params=_cparams(("parallel",)),
    )(z, jnp.asarray(sel_f, F32), jnp.asarray(sel_b, F32), w1p, b1.reshape(1, hid), w2, b2.reshape(1, hid),
      freq.reshape(1, hid))
    ncb = HY_OC // LANES
    return pl.pallas_call(
        functools.partial(_filter_spec_kernel, r),
        out_shape=[jax.ShapeDtypeStruct((r, half, HY_OC), F32)] * 3,
        grid=(ncb,),
        in_specs=[
            full((n_tot, hid)), full((n_tot, hid)), full((n_tot, LANES)),
            pl.BlockSpec((hid, LANES), lambda j: (0, j)),
            pl.BlockSpec((hid, LANES), lambda j: (0, j + ncb)),
            pl.BlockSpec((1, LANES), lambda j: (0, j)),
            pl.BlockSpec((1, LANES), lambda j: (0, j + ncb)),
            full((2 * half, n_sub)),
        ],
        out_specs=[pl.BlockSpec((r, half, LANES), lambda j: (0, 0, j))] * 3,
        compiler_params=_cparams(("parallel",)),
    )(hf, hb, jnp.asarray(t, F32), w3, w3, decay.reshape(1, -1), decay.reshape(1, -1), fwd)


def _hyena_kernel(r, hy0_ref, hy1_ref, hy2_ref, sw_ref, sb_ref, skip_ref,
                  ga0_ref, gb0_ref, gc0_ref, ga1_ref, gb1_ref, gc1_ref, fwd_ref, inv_ref, twr_ref, twi_ref,
                  o_ref, z_ref, ph_ref, rhs_ref, x_ref):
    seq = hy0_ref.shape[0]
    m_len = seq // r
    half = fwd_ref.shape[0] // 2
    row = lax.broadcasted_iota(jnp.int32, (m_len, LANES), 0)
    hy_refs = (hy0_ref, hy1_ref, hy2_ref)

    def load_phases(part):
        for j in range(r):
            ph_ref[j] = hy_refs[part][pl.ds(j, m_len, stride=r), :]

    def short_conv(part, j):
        w = sw_ref[:, part * LANES:(part + 1) * LANES]
        b = sb_ref[:, part * LANES:(part + 1) * LANES]
        if j > 0:
            prev = ph_ref[j - 1]
        else:
            prev = jnp.where(row == 0, 0.0, pltpu.roll(ph_ref[r - 1], 1, axis=0))
        if j < r - 1:
            nxt = ph_ref[j + 1]
        else:
            nxt = jnp.where(row == m_len - 1, 0.0, pltpu.roll(ph_ref[0], m_len - 1, axis=0))
        return prev * w[0:1] + ph_ref[j] * w[1:2] + nxt * w[2:3] + b

    load_phases(0)
    for j in range(r):
        z_ref[j] = short_conv(0, j)

    for o, (ga_ref, gb_ref, gc_ref) in enumerate(((ga0_ref, gb0_ref, gc0_ref), (ga1_ref, gb1_ref, gc1_ref))):
        for j in range(r):
            rhs_ref[:, j * LANES:(j + 1) * LANES] = z_ref[j].astype(BF16)
        x_ref[...] = _dot(fwd_ref[...], rhs_ref[...])
        dc = [x_ref[0:1, j * LANES:(j + 1) * LANES] for j in range(r)]
        ny = [x_ref[half:half + 1, j * LANES:(j + 1) * LANES] for j in range(r)]
        y_dc, y_ny = [], []
        for j in range(r):
            a = jnp.zeros((1, LANES), F32)
            c = jnp.zeros((1, LANES), F32)
            for jp in range(r):
                p = (j - jp) % r
                a = a + ga_ref[p, 0:1, :] * dc[jp]
                t = (ga_ref[p, 0:1, :] + gb_ref[p, 0:1, :]) * ny[jp]
                c = c + t if jp <= j else c - t
            y_dc.append(a)
            y_ny.append(c)

        def mix(i, carry):
            r0 = pl.multiple_of(i * SUBLANES, SUBLANES)
            rows_re = pl.ds(r0, SUBLANES)
            rows_im = pl.ds(half + r0, SUBLANES)
            xr = [x_ref[rows_re, j * LANES:(j + 1) * LANES] for j in range(r)]
            xi = [x_ref[rows_im, j * LANES:(j + 1) * LANES] for j in range(r)]
            xs = [a + b for a, b in zip(xr, xi)]
            wr = twr_ref[rows_re, :]
            wi = twi_ref[rows_re, :]
            for j in range(r):
                acc = {}
                for jp in range(r):
                    p = (j - jp) % r
                    k1 = ga_ref[p, rows_re, :] * xs[jp]
                    k2 = gb_ref[p, rows_re, :] * xr[jp]
                    k3 = gc_ref[p, rows_re, :] * xi[jp]
                    side = jp <= j
                    ks = (k1, k2, k3)
                    acc[side] = ks if side not in acc else tuple(a + b for a, b in zip(acc[side], ks))
                pr = acc[True][0] - acc[True][2]
                pi = acc[True][0] + acc[True][1]
                if False in acc:
                    qr = acc[False][0] - acc[False][2]
                    qi = acc[False][0] + acc[False][1]
                    pr = pr + wr * qr - wi * qi
                    pi = pi + wr * qi + wi * qr
                x_ref[rows_re, j * LANES:(j + 1) * LANES] = pr
                x_ref[rows_im, j * LANES:(j + 1) * LANES] = pi
            return carry

        lax.fori_loop(0, half // SUBLANES, mix, 0)
        for j in range(r):
            x_ref[0:1, j * LANES:(j + 1) * LANES] = y_dc[j]
            x_ref[half:half + 1, j * LANES:(j + 1) * LANES] = y_ny[j]
        y = _dot(inv_ref[...], x_ref[...].astype(BF16))
        sk = skip_ref[o:o + 1, :]
        load_phases(o + 1)
        for j in range(r):
            z_ref[j] = short_conv(o + 1, j) * (y[:, j * LANES:(j + 1) * LANES] + z_ref[j] * sk)
    for j in range(r):
        o_ref[pl.ds(j, m_len, stride=r), :] = z_ref[j]


def _hyena_direct_kernel(hy_ref, sw_ref, sb_ref, skip_ref, ga_ref, gb_ref, fwd_ref, inv_ref, o_ref):
    bb, seq, _ = hy_ref.shape
    half = fwd_ref.shape[0] // 2
    w = HYENA_WIDTH
    row = lax.broadcasted_iota(jnp.int32, (seq, w), 0)
    is_dc = lax.broadcasted_iota(jnp.int32, (half, w), 0) == 0

    def short_conv(b, part):
        cols = slice(part * w, (part + 1) * w)
        x = hy_ref[b, :, cols]
        prev = jnp.where(row == 0, 0.0, pltpu.roll(x, 1, axis=0))
        nxt = jnp.where(row == seq - 1, 0.0, pltpu.roll(x, seq - 1, axis=0))
        return prev * sw_ref[0:1, cols] + x * sw_ref[1:2, cols] + nxt * sw_ref[2:3, cols] + sb_ref[:, cols]

    for b in range(bb):
        z = short_conv(b, 0)
        for o in range(HY_ORDER):
            cols = slice(o * w, (o + 1) * w)
            x = _dot(fwd_ref[...], z.astype(BF16))
            xr, xi = x[:half], x[half:]
            gr = ga_ref[0, :, cols]
            gi = gb_ref[0, :, cols] + gr
            vr = gr * xr - jnp.where(is_dc, 0.0, gi * xi)
            vi = jnp.where(is_dc, gi * xi, gr * xi + gi * xr)
            y = _dot(inv_ref[...], jnp.concatenate([vr, vi], axis=0).astype(BF16))
            z = short_conv(b, o + 1) * (y + z * skip_ref[o:o + 1, :])
        o_ref[b] = z


def _hyena_direct(hy, short_w, short_b, skip, ga, gb, fwd, inv):
    nb, seq, _ = hy.shape
    bb = _pick_tile(nb, 4)
    const = lambda a: pl.BlockSpec(a.shape, lambda i: (0,) * a.ndim)
    short_b = short_b.reshape(1, -1)
    return pl.pallas_call(
        _hyena_direct_kernel,
        out_shape=jax.ShapeDtypeStruct((nb, seq, HYENA_WIDTH), F32),
        grid=(nb // bb,),
        in_specs=[pl.BlockSpec((bb, seq, HY_IN), lambda i: (i, 0, 0)), const(short_w), const(short_b), const(skip),
                  const(ga), const(gb), const(fwd), const(inv)],
        out_specs=pl.BlockSpec((bb, seq, HYENA_WIDTH), lambda i: (i, 0, 0)),
        compiler_params=_cparams(("parallel",)),
    )(hy, short_w, short_b, skip, ga, gb, fwd, inv)


def _hyena(hy, short_w, short_b, skip, spectra, fwd, inv, tw_r, tw_i, r):
    ga, gb, gc = spectra
    if r == 1:
        return _hyena_direct(hy, short_w, short_b, skip, ga, gb, fwd, inv)
    nb, seq, _ = hy.shape
    m_len = seq // r
    n_half2 = fwd.shape[0]
    half = n_half2 // 2
    ncb = HYENA_WIDTH // LANES
    parts = HY_ORDER + 1
    once = pl.Buffered(1)
    hy_spec = lambda part: pl.BlockSpec((None, seq, LANES), lambda c, b: (b, 0, part * ncb + c))
    g_spec = lambda o: pl.BlockSpec((r, half, LANES), lambda c, b: (0, 0, o * ncb + c), pipeline_mode=once)
    const = lambda shape: pl.BlockSpec(shape, lambda c, b: (0,) * len(shape), pipeline_mode=once)
    sw = short_w.reshape(3, parts, ncb, LANES).transpose(2, 0, 1, 3).reshape(ncb, 3, parts * LANES)
    sb = short_b.reshape(1, parts, ncb, LANES).transpose(2, 0, 1, 3).reshape(ncb, 1, parts * LANES)
    return pl.pallas_call(
        functools.partial(_hyena_kernel, r),
        out_shape=jax.ShapeDtypeStruct((nb, seq, HYENA_WIDTH), F32),
        grid=(ncb, nb),
        in_specs=[
            hy_spec(0), hy_spec(1), hy_spec(2),
            pl.BlockSpec((None, 3, parts * LANES), lambda c, b: (c, 0, 0)),
            pl.BlockSpec((None, 1, parts * LANES), lambda c, b: (c, 0, 0)),
            pl.BlockSpec((HY_ORDER, LANES), lambda c, b: (0, c)),
            g_spec(0), g_spec(0), g_spec(0), g_spec(1), g_spec(1), g_spec(1),
            const((n_half2, m_len)), const((m_len, n_half2)),
            const((half, LANES)), const((half, LANES)),
        ],
        out_specs=pl.BlockSpec((None, seq, LANES), lambda c, b: (b, 0, c)),
        scratch_shapes=[
            pltpu.VMEM((r, m_len, LANES), F32),
            pltpu.VMEM((r, m_len, LANES), F32),
            pltpu.VMEM((m_len, r * LANES), BF16),
            pltpu.VMEM((n_half2, r * LANES), F32),
        ],
        compiler_params=_cparams(("parallel", "parallel")),
    )(hy, hy, hy, sw, sb, skip, ga, gb, gc, ga, gb, gc, fwd, inv, tw_r, tw_i)


def _route(logits):
    lane = lax.broadcasted_iota(jnp.int32, logits.shape, 1).astype(F32)
    big = jnp.float32(1e9)
    is_grp = (lane >= N_EXPERTS) & (lane < N_EXPERTS + N_GROUPS)
    gl = jnp.where(is_grp, logits, NEG_BIG)
    gmax = jnp.max(gl, axis=1, keepdims=True)
    gidx = jnp.min(jnp.where(gl == gmax, lane, big), axis=1, keepdims=True) - N_EXPERTS
    den = jnp.sum(jnp.where(is_grp, jnp.exp(gl - gmax), 0.0), axis=1, keepdims=True)
    pg_top = 1.0 / den
    lo = gidx * EXPERTS_PER_GROUP
    sel = jnp.where((lane >= lo) & (lane < lo + EXPERTS_PER_GROUP), logits, NEG_BIG)
    m1 = jnp.max(sel, axis=1, keepdims=True)
    i1 = jnp.min(jnp.where(sel == m1, lane, big), axis=1, keepdims=True)
    sel2 = jnp.where(lane == i1, NEG_BIG, sel)
    m2 = jnp.max(sel2, axis=1, keepdims=True)
    i2 = jnp.min(jnp.where(sel2 == m2, lane, big), axis=1, keepdims=True)
    e2 = jnp.exp(m2 - m1)
    w1 = pg_top / (1.0 + e2)
    w2 = pg_top * e2 / (1.0 + e2)
    comb = jnp.where(lane == i1, w1, 0.0) + jnp.where(lane == i2, w2, 0.0)
    return comb + jnp.where(lane == EXPERT_ID_LANE, i1, 0.0) + jnp.where(lane == EXPERT_ID_LANE + 1, i2, 0.0)


def _outproj_kernel(attn_ref, hyz_ref, x_ref, mod_ref, wo_ref, hg_ref, bd_ref, lng_ref, lnb_ref,
                    rwh_ref, rwl_ref, rb_ref, x1_ref, tok_ref, rec_ref):
    m = mod_ref[...]
    tm = x_ref.shape[0]
    n_sub = _proj_subtiles(tm)
    ts = tm // n_sub
    half = wo_ref.shape[0] // 2
    for t in range(n_sub):
        rows = slice(t * ts, (t + 1) * ts)
        z = hyz_ref[rows, :]
        ms = _dot((z * z).astype(BF16), bd_ref[...])
        zn = (z * lax.rsqrt(ms + EPS) * hg_ref[...]).astype(BF16)
        mix = _dot(attn_ref[rows, :], wo_ref[:half, :]) + _dot(zn, wo_ref[half:, :])
        x1 = _layer_norm(DEEPNORM_ALPHA * x_ref[rows, :] + m[2:3] * mix, lng_ref[...], lnb_ref[...])
        h2 = x1 * (1.0 + m[4:5]) + m[3:4]
        x1_ref[rows, :] = x1
        tok_ref[rows, :D_MODEL] = h2
        hh, hl = _split_bf16(h2)
        logits = _dot(hh, rwh_ref[...]) + _dot(hl, rwh_ref[...]) + _dot(hh, rwl_ref[...]) + rb_ref[...]
        record = _route(logits)
        tok_ref[rows, D_MODEL:] = record
        rec_ref[rows, :] = record


def _outproj(attn, hyz, x, mod, w_out, hy_gain, bd_hy, ln_g, ln_b, rw_hi, rw_lo, rb, latent, tm):
    nb, seq, _ = x.shape
    mod_map = (lambda b, i: (b, 0, 0)) if latent else (lambda b, i: (0, 0, 0))
    const = lambda shape: pl.BlockSpec(shape, lambda b, i: (0,) * len(shape))
    tok = lambda width: pl.BlockSpec((None, tm, width), lambda b, i: (b, i, 0))
    return pl.pallas_call(
        _outproj_kernel,
        out_shape=[
            jax.ShapeDtypeStruct((nb, seq, D_MODEL), F32),
            jax.ShapeDtypeStruct((nb, seq, D_MODEL + LANES), F32),
            jax.ShapeDtypeStruct((nb, seq, LANES), F32),
        ],
        grid=(nb, seq // tm),
        in_specs=[
            tok(Q_DIM), tok(HYENA_WIDTH), tok(D_MODEL),
            pl.BlockSpec((None, 6, D_MODEL), mod_map),
            const((D_MODEL, D_MODEL)), const((1, HYENA_WIDTH)), const((HYENA_WIDTH, HYENA_WIDTH)),
            const((1, D_MODEL)), const((1, D_MODEL)),
            const((D_MODEL, LANES)), const((D_MODEL, LANES)), const((1, LANES)),
        ],
        out_specs=[tok(D_MODEL), tok(D_MODEL + LANES), tok(LANES)],
        compiler_params=_cparams(("parallel", "parallel")),
    )(attn, hyz, x, mod, w_out, hy_gain, bd_hy, ln_g, ln_b, rw_hi, rw_lo, rb)


def _gather_rows(table, idx):
    n, d = idx.shape[0], table.shape[1]
    info = plsc.get_sparse_core_info()
    workers = info.num_cores * info.num_subcores
    per_worker = n // workers
    assert per_worker * workers == n and per_worker % GATHER_ROWS == 0
    mesh = plsc.VectorSubcoreMesh(core_axis_name="c", subcore_axis_name="s")

    @functools.partial(
        pl.kernel, mesh=mesh,
        out_type=jax.ShapeDtypeStruct((n, d), table.dtype),
        scratch_types=[pltpu.VMEM((GATHER_ROWS,), jnp.int32), pltpu.VMEM((GATHER_ROWS, d), table.dtype),
                       pltpu.SemaphoreType.DMA],
    )
    def gather(table_hbm, idx_hbm, out_hbm, idx_v, rows_v, sem):
        wid = lax.axis_index("s") * info.num_cores + lax.axis_index("c")
        base = wid * per_worker

        @pl.loop(0, per_worker // GATHER_ROWS)
        def _(j):
            off = base + j * GATHER_ROWS
            pltpu.sync_copy(idx_hbm.at[pl.ds(off, GATHER_ROWS)], idx_v)
            pltpu.async_copy(table_hbm.at[idx_v], rows_v, sem).wait()
            pltpu.sync_copy(rows_v, out_hbm.at[pl.ds(off, GATHER_ROWS)])

    return gather(table, idx)


def _sort_plan(record, tile):
    tokens = record.shape[0]
    eid = jnp.concatenate([record[:, EXPERT_ID_LANE], record[:, EXPERT_ID_LANE + 1]]).astype(jnp.int32)
    onehot = (eid[:, None] == jnp.arange(N_EXPERTS, dtype=jnp.int32)[None, :]).astype(jnp.int32)
    csum = jnp.cumsum(onehot, axis=0)
    counts = csum[-1]
    rank = jnp.sum(csum * onehot, axis=1) - 1
    padded = (counts + tile - 1) // tile * tile
    ends = jnp.cumsum(padded)
    starts = ends - padded
    pos = jnp.sum(starts[None, :] * onehot, axis=1) + rank
    n_pad = 2 * tokens + N_EXPERTS * tile
    filler = jnp.arange(n_pad - 2 * tokens, dtype=jnp.int32)
    filler_eid = jnp.sum((filler[:, None] >= jnp.cumsum(padded - counts)[None, :]).astype(jnp.int32), axis=1)
    token = jnp.arange(tokens, dtype=jnp.int32)
    _, src = lax.sort((jnp.concatenate([eid, filler_eid]), jnp.concatenate([token, token, jnp.zeros_like(filler)])),
                      num_keys=1, is_stable=True)
    tile_start = jnp.arange(n_pad // tile, dtype=jnp.int32) * tile
    tile_expert = jnp.minimum(jnp.sum((tile_start[:, None] >= ends[None, :]).astype(jnp.int32), axis=1),
                              N_EXPERTS - 1)
    tile_valid = (tile_start < ends[-1]).astype(jnp.int32)
    return pos, src, tile_expert, tile_valid


def _moe_kernel(te_ref, tv_ref, x_ref, wg_ref, wu_ref, wd_ref, o_ref):
    i = pl.program_id(0)

    @pl.when(tv_ref[i] > 0)
    def _():
        x = x_ref[:, :D_MODEL].astype(BF16)
        rec = x_ref[:, D_MODEL:]
        lane = lax.broadcasted_iota(jnp.int32, rec.shape, 1)
        c = jnp.sum(jnp.where(lane == te_ref[i], rec, 0.0), axis=1, keepdims=True)
        h = _silu(_dot(x, wg_ref[...])) * _dot(x, wu_ref[...]) * c
        o_ref[...] = _dot(h.astype(BF16), wd_ref[...])

    @pl.when(tv_ref[i] == 0)
    def _():
        o_ref[...] = jnp.zeros_like(o_ref)


def _moe(sorted_tok, tile_expert, tile_valid, wg, wu, wd):
    rows = sorted_tok.shape[0]
    tile = MOE_TILE
    per_expert = lambda shape: pl.BlockSpec((None,) + shape, lambda i, te, tv: (te[i], 0, 0))
    return pl.pallas_call(
        _moe_kernel,
        out_shape=jax.ShapeDtypeStruct((rows, D_MODEL), F32),
        grid_spec=pltpu.PrefetchScalarGridSpec(
            num_scalar_prefetch=2,
            grid=(rows // tile,),
            in_specs=[
                pl.BlockSpec((tile, D_MODEL + LANES), lambda i, te, tv: (i, 0)),
                per_expert((D_MODEL, EXPERT_FF)), per_expert((D_MODEL, EXPERT_FF)), per_expert((EXPERT_FF, D_MODEL)),
            ],
            out_specs=pl.BlockSpec((tile, D_MODEL), lambda i, te, tv: (i, 0)),
        ),
        compiler_params=_cparams(("arbitrary",)),
    )(tile_expert, tile_valid, sorted_tok, wg, wu, wd)


def _final_norm_kernel(x1_ref, y_ref, mod_ref, lng_ref, lnb_ref, o_ref):
    moe = y_ref[0] + y_ref[1]
    o_ref[...] = _layer_norm(DEEPNORM_ALPHA * x1_ref[...] + mod_ref[5:6, :] * moe, lng_ref[...], lnb_ref[...])


def _final_norm(x1, y, mod, ln_g, ln_b, latent, tm):
    nb, seq, _ = x1.shape
    mod_map = (lambda b, i: (b, 0, 0)) if latent else (lambda b, i: (0, 0, 0))
    tok = pl.BlockSpec((None, tm, D_MODEL), lambda b, i: (b, i, 0))
    const = pl.BlockSpec((1, D_MODEL), lambda b, i: (0, 0))
    return pl.pallas_call(
        _final_norm_kernel,
        out_shape=jax.ShapeDtypeStruct((nb, seq, D_MODEL), F32),
        grid=(nb, seq // tm),
        in_specs=[tok, pl.BlockSpec((TOP_K, None, tm, D_MODEL), lambda b, i: (0, b, i, 0)),
                  pl.BlockSpec((None, 6, D_MODEL), mod_map), const, const],
        out_specs=tok,
        compiler_params=_cparams(("parallel", "parallel")),
    )(x1, y, mod, ln_g, ln_b)


def _block_diag_mean(width, group):
    idx = np.arange(width) // group
    return jnp.asarray((idx[:, None] == idx[None, :]).astype(np.float32) / group, BF16)


def _pick_tile(n, target):
    t = min(n, target)
    while n % t:
        t //= 2
    return t


def _trunk_layer(x, mod, p, latent, ctx_k=None, ctx_v=None):
    nb, seq, _ = x.shape
    tm = _pick_tile(seq, 1024)
    q, kt, vx, hy, *kv_new = _inproj(x, mod, p['w_in'], p['qk_gain'], p['bd_qk'], latent, tm)
    if latent:
        ckt = jnp.transpose(ctx_k, (0, 2, 3, 1)).astype(BF16)
        cv = jnp.transpose(ctx_v, (0, 2, 1, 3))
        cvx = jnp.concatenate([cv, jnp.ones_like(cv)], axis=-1).astype(BF16)
    else:
        ckt = cvx = None
    attn = _attention(q, kt, vx, p['attn_gain'], ckt, cvx)

    r = _hyena_decimation(seq)
    fwd, inv, tw_r, tw_i = _dft_tables(seq, r)
    fwd = jnp.asarray(fwd, F32).astype(BF16)
    spectra = _hyena_filters(seq, r, p['hy_f_w1'], p['hy_f_b1'], p['hy_f_w2'], p['hy_f_b2'], p['hy_f_w3'],
                              p['hy_freq'], p['hy_decay'], fwd)
    hyz = _hyena(hy, p['hy_short_w'], p['hy_short_b'], p['hy_skip'], spectra,
                 fwd[:, :seq // r], jnp.asarray(inv, F32).astype(BF16), jnp.asarray(tw_r, F32),
                 jnp.asarray(tw_i, F32), r)

    x1, tok, record = _outproj(attn, hyz, x, mod, p['w_out'], p['hy_gain'], p['bd_hy'], p['ln1_g'], p['ln1_b'],
                               p['rw_hi'], p['rw_lo'], p['rb'], latent, tm)
    tok = tok.reshape(nb * seq, D_MODEL + LANES)
    pos, src, tile_expert, tile_valid = _sort_plan(record.reshape(nb * seq, LANES), MOE_TILE)
    y_sorted = _moe(_gather_rows(tok, src), tile_expert, tile_valid, p['wg'], p['wu'], p['wd'])
    y = _gather_rows(y_sorted, pos).reshape(TOP_K, nb, seq, D_MODEL)
    return _final_norm(x1, y, mod, p['ln2_g'], p['ln2_b'], latent, tm), kv_new


def _prepare(w_in, q_gain, k_gain, attn_out_gain, hy_short_w, hy_short_b, hy_f_w1, hy_f_b1, hy_f_w2, hy_f_b2,
             hy_f_w3, hy_freq, hy_decay, hy_skip, hy_out_gain, w_out, ln1_g, ln1_b, router_grp_w, router_grp_b,
             router_exp_w, router_exp_b, exp_w_gate, exp_w_up, exp_w_down, ln2_g, ln2_b, l):
    row = lambda a: a.reshape(1, -1)
    rw = jnp.concatenate([router_exp_w[l], router_grp_w[l]], axis=1)
    rw = jnp.pad(rw, ((0, 0), (0, LANES - rw.shape[1])))
    rb = jnp.concatenate([router_exp_b[l], router_grp_b[l]])
    rb = jnp.pad(rb, (0, LANES - rb.shape[0]))
    rw_hi, rw_lo = _split_bf16(rw)
    return {
        'w_in': w_in[l].astype(BF16),
        'qk_gain': row(jnp.concatenate([jnp.tile(q_gain[l], N_HEADS), jnp.tile(k_gain[l], N_KV_HEADS)])),
        'bd_qk': _block_diag_mean(QK_DIM, HEAD_DIM),
        'attn_gain': row(attn_out_gain[l]),
        'hy_short_w': hy_short_w[l], 'hy_short_b': hy_short_b[l],
        'hy_f_w1': hy_f_w1[l], 'hy_f_b1': hy_f_b1[l], 'hy_f_w2': hy_f_w2[l], 'hy_f_b2': hy_f_b2[l],
        'hy_f_w3': hy_f_w3[l], 'hy_freq': hy_freq[l], 'hy_decay': hy_decay[l], 'hy_skip': hy_skip[l],
        'hy_gain': row(hy_out_gain[l]),
        'bd_hy': _block_diag_mean(HYENA_WIDTH, HY_GROUP_DIM),
        'w_out': w_out[l].astype(BF16),
        'ln1_g': row(ln1_g[l]), 'ln1_b': row(ln1_b[l]),
        'rw_hi': rw_hi, 'rw_lo': rw_lo, 'rb': row(rb),
        'wg': exp_w_gate[l].astype(BF16), 'wu': exp_w_up[l].astype(BF16), 'wd': exp_w_down[l].astype(BF16),
        'ln2_g': row(ln2_g[l]), 'ln2_b': row(ln2_b[l]),
    }


def kernel(x_prompt, x_sample, c, cache_k, cache_v, c_ctx, w_mod, b_mod, w_in, q_gain, k_gain, attn_out_gain, hy_short_w, hy_short_b, hy_f_w1, hy_f_b1, hy_f_w2, hy_f_b2, hy_f_w3, hy_freq, hy_decay, hy_skip, hy_out_gain, w_out, ln1_g, ln1_b, router_grp_w, router_grp_b, router_exp_w, router_exp_b, exp_w_gate, exp_w_up, exp_w_down, ln2_g, ln2_b):
    depth = w_mod.shape[0]
    n_lat = c.shape[0]
    cond = jnp.concatenate([c, c_ctx[None, :]], axis=0)
    rows = -(-cond.shape[0] // SUBLANES) * SUBLANES
    cond = jnp.pad(cond, ((0, rows - cond.shape[0]), (0, 0)))
    y_prompt, y_sample = x_prompt, x_sample
    ks_new, vs_new = [], []
    for l in range(depth):
        p = _prepare(w_in, q_gain, k_gain, attn_out_gain, hy_short_w, hy_short_b, hy_f_w1, hy_f_b1, hy_f_w2,
                     hy_f_b2, hy_f_w3, hy_freq, hy_decay, hy_skip, hy_out_gain, w_out, ln1_g, ln1_b,
                     router_grp_w, router_grp_b, router_exp_w, router_exp_b, exp_w_gate, exp_w_up, exp_w_down,
                     ln2_g, ln2_b, l)
        mod = _adaln(cond, w_mod[l], b_mod[l])
        mod_lat = mod[:n_lat].reshape(n_lat, 6, D_MODEL)
        mod_ctx = mod[n_lat:n_lat + 1].reshape(1, 6, D_MODEL)
        y_prompt, (v_new, k_new) = _trunk_layer(y_prompt, mod_ctx, p, False)
        ks_new.append(k_new.reshape(k_new.shape[:2] + (N_KV_HEADS, HEAD_DIM)))
        vs_new.append(v_new.reshape(v_new.shape[:2] + (N_KV_HEADS, HEAD_DIM)))
        y_sample, _ = _trunk_layer(y_sample, mod_lat, p, True, cache_k[:, l], cache_v[:, l])
    return (y_prompt, y_sample, jnp.stack(ks_new, axis=1), jnp.stack(vs_new, axis=1))
```

```python
import functools

import numpy as np
import jax
import jax.numpy as jnp
from jax import lax
from jax.experimental import pallas as pl
from jax.experimental.pallas import tpu as pltpu
from jax.experimental.pallas import tpu_sc as plsc

F32 = jnp.float32
BF16 = jnp.bfloat16

D_MODEL = 1024
GRID_W = 64
HEAD_DIM = 64
N_HEADS = 8
N_KV_HEADS = 2
GQA_GROUP = N_HEADS // N_KV_HEADS
Q_DIM = N_HEADS * HEAD_DIM
KV_DIM = N_KV_HEADS * HEAD_DIM
QK_DIM = Q_DIM + KV_DIM
HYENA_WIDTH = 512
HY_ORDER = 2
HY_IN = (HY_ORDER + 1) * HYENA_WIDTH
HY_GROUP_DIM = 64
HY_BANDS = 16
HY_POS_DIM = 1 + 2 * HY_BANDS
HY_FILTER_HIDDEN = 64
HY_OC = HY_ORDER * HYENA_WIDTH
IN_WIDTH = Q_DIM + 2 * KV_DIM + HY_IN
ROPE_THETA = 10000.0
ROPE_FREQS = HEAD_DIM // 4
N_GROUPS = 4
EXPERTS_PER_GROUP = 8
N_EXPERTS = N_GROUPS * EXPERTS_PER_GROUP
EXPERT_FF = D_MODEL // 4
DEPTH = 1
DEEPNORM_ALPHA = (2.0 * DEPTH) ** 0.25
EPS = 1e-6

LANES = 128
SUBLANES = 8
VMEM_LIMIT = 56 * 1024 * 1024
MOE_TILE = 512
GATHER_ROWS = 32
GROUP_ID_LANE = N_EXPERTS
PROJ_SUBTILES = 2
PROJ_MIN_SUBTILE = 256
ATTN_SUBTILES = 4
ATTN_LONG_SEQ = 1024
NEG_BIG = -1e30
LOG2_E = 1.4426950408889634


def _cparams(sem):
    return pltpu.CompilerParams(dimension_semantics=sem, vmem_limit_bytes=VMEM_LIMIT)


def _proj_subtiles(tm):
    return PROJ_SUBTILES if tm // PROJ_SUBTILES >= PROJ_MIN_SUBTILE else 1


def _split_bf16(a):
    hi = a.astype(BF16)
    lo = (a - hi.astype(F32)).astype(BF16)
    return hi, lo


def _dot(a, b):
    return jnp.dot(a, b, preferred_element_type=F32)


def _dot3(a, b):
    ah, al = _split_bf16(a)
    bh, bl = _split_bf16(b)
    return _dot(ah, bh) + _dot(al, bh) + _dot(ah, bl)


def _silu(x):
    return x / (1.0 + jnp.exp(-x))


def _layer_norm(y, g, b):
    mu = jnp.mean(y, axis=-1, keepdims=True)
    yc = y - mu
    var = jnp.mean(yc * yc, axis=-1, keepdims=True)
    return yc * lax.rsqrt(var + EPS) * g + b


def _adaln_kernel(c_ref, w_ref, b_ref, o_ref):
    o_ref[...] = _dot3(_silu(c_ref[...]), w_ref[...]) + b_ref[...]


def _adaln(cond, w_mod, b_mod):
    rows = cond.shape[0]
    n = w_mod.shape[1]
    tn = 1536
    return pl.pallas_call(
        _adaln_kernel,
        out_shape=jax.ShapeDtypeStruct((rows, n), F32),
        grid=(n // tn,),
        in_specs=[
            pl.BlockSpec((rows, D_MODEL), lambda j: (0, 0)),
            pl.BlockSpec((D_MODEL, tn), lambda j: (0, j)),
            pl.BlockSpec((1, tn), lambda j: (0, j)),
        ],
        out_specs=pl.BlockSpec((rows, tn), lambda j: (0, j)),
        compiler_params=_cparams(("arbitrary",)),
    )(cond, w_mod, b_mod.reshape(1, n))


def _rope_tables(seq):
    t = np.arange(seq)
    rows = (t // GRID_W).astype(np.float64)
    cols = (t % GRID_W).astype(np.float64)
    inv_freq = ROPE_THETA ** (-np.arange(ROPE_FREQS, dtype=np.float64) / ROPE_FREQS)
    d = np.arange(LANES) % HEAD_DIM
    axis = d // (2 * ROPE_FREQS)
    f = d % ROPE_FREQS
    pos = np.where(axis[None, :] == 0, rows[:, None], cols[:, None])
    ang = pos * inv_freq[f][None, :]
    first = (d % (2 * ROPE_FREQS)) < ROPE_FREQS
    cos = np.cos(ang)
    sin = np.where(first[None, :], -np.sin(ang), np.sin(ang))
    return jnp.asarray(cos, F32), jnp.asarray(sin, F32)


def _inproj_kernel(latent, x_ref, mod_ref, w_ref, gain_ref, bd_ref, *rest):
    if latent:
        cos_ref, sin_ref, q_ref, kt_ref, vx_ref, hy_ref = rest
    else:
        q_ref, kt_ref, vx_ref, hy_ref, v_ref, knat_ref = rest
    m = mod_ref[...]
    tm = x_ref.shape[0]
    n_sub = _proj_subtiles(tm)
    ts = tm // n_sub
    for t in range(n_sub):
        rows = slice(t * ts, (t + 1) * ts)
        h = x_ref[rows, :] * (1.0 + m[1:2]) + m[0:1]
        proj = _dot(h.astype(BF16), w_ref[...])
        qk = proj[:, :QK_DIM]
        ms = _dot((qk * qk).astype(BF16), bd_ref[...])
        qk = qk * lax.rsqrt(ms + EPS) * gain_ref[...]
        if not latent:
            knat_ref[rows, :] = qk[:, Q_DIM:]
        else:
            cos = cos_ref[rows, :]
            sin = sin_ref[rows, :]
            lane = lax.broadcasted_iota(jnp.int32, cos.shape, 1)
            first = (lane % (2 * ROPE_FREQS)) < ROPE_FREQS
            chunks = []
            for c in range(QK_DIM // LANES):
                xc = qk[:, c * LANES:(c + 1) * LANES]
                below = pltpu.roll(xc, ROPE_FREQS, axis=1)
                above = pltpu.roll(xc, LANES - ROPE_FREQS, axis=1)
                chunks.append(xc * cos + jnp.where(first, above, below) * sin)
            qk = jnp.concatenate(chunks, axis=1)
        qs = (qk[:, :Q_DIM] * (HEAD_DIM ** -0.5 * LOG2_E)).astype(BF16)
        for hd in range(N_HEADS):
            q_ref[hd, rows, :] = qs[:, hd * HEAD_DIM:(hd + 1) * HEAD_DIM]
        kt = qk[:, Q_DIM:].T
        for kh in range(N_KV_HEADS):
            kt_ref[kh, :, rows] = kt[kh * HEAD_DIM:(kh + 1) * HEAD_DIM].astype(BF16)
        v = proj[:, QK_DIM:QK_DIM + KV_DIM]
        low = lax.broadcasted_iota(jnp.int32, v.shape, 1) < HEAD_DIM
        vx_ref[0, rows, :] = jnp.where(low, v, 1.0).astype(BF16)
        vx_ref[1, rows, :] = jnp.where(low, pltpu.roll(v, HEAD_DIM, axis=1), 1.0).astype(BF16)
        if not latent:
            v_ref[rows, :] = v
        hy_ref[rows, :] = proj[:, QK_DIM + KV_DIM:]


def _inproj(x, mod, w_in, qk_gain, bd_qk, latent, tm):
    nb, seq, _ = x.shape
    grid = (nb, seq // tm)
    mod_map = (lambda b, i: (b, 0, 0)) if latent else (lambda b, i: (0, 0, 0))
    in_specs = [
        pl.BlockSpec((None, tm, D_MODEL), lambda b, i: (b, i, 0)),
        pl.BlockSpec((None, 6, D_MODEL), mod_map),
        pl.BlockSpec((D_MODEL, IN_WIDTH), lambda b, i: (0, 0)),
        pl.BlockSpec((1, QK_DIM), lambda b, i: (0, 0)),
        pl.BlockSpec((QK_DIM, QK_DIM), lambda b, i: (0, 0)),
    ]
    args = [x, mod, w_in, qk_gain, bd_qk]
    out_shape = [
        jax.ShapeDtypeStruct((nb, N_HEADS, seq, HEAD_DIM), BF16),
        jax.ShapeDtypeStruct((nb, N_KV_HEADS, HEAD_DIM, seq), BF16),
        jax.ShapeDtypeStruct((nb, N_KV_HEADS, seq, KV_DIM), BF16),
        jax.ShapeDtypeStruct((nb, seq, HY_IN), F32),
    ]
    out_specs = [
        pl.BlockSpec((None, N_HEADS, tm, HEAD_DIM), lambda b, i: (b, 0, i, 0)),
        pl.BlockSpec((None, N_KV_HEADS, HEAD_DIM, tm), lambda b, i: (b, 0, 0, i)),
        pl.BlockSpec((None, N_KV_HEADS, tm, KV_DIM), lambda b, i: (b, 0, i, 0)),
        pl.BlockSpec((None, tm, HY_IN), lambda b, i: (b, i, 0)),
    ]
    if latent:
        cos, sin = _rope_tables(seq)
        in_specs += [pl.BlockSpec((tm, LANES), lambda b, i: (i, 0))] * 2
        args += [cos, sin]
    else:
        out_shape += [jax.ShapeDtypeStruct((nb, seq, KV_DIM), F32)] * 2
        out_specs += [pl.BlockSpec((None, tm, KV_DIM), lambda b, i: (b, i, 0))] * 2
    return pl.pallas_call(
        functools.partial(_inproj_kernel, latent),
        out_shape=out_shape,
        grid=grid,
        in_specs=in_specs,
        out_specs=out_specs,
        compiler_params=_cparams(("parallel", "parallel")),
    )(*args)


def _attn_kernel(n_ctx, chunk, subtiles, q_ref, kt_ref, v_ref, gain_ref, *rest):
    if n_ctx:
        ckt_ref, cv_ref, o_ref = rest
    else:
        (o_ref,) = rest
    bb, heads, tq, _ = q_ref.shape
    seq = kt_ref.shape[-1]
    g = GQA_GROUP
    ts = tq // subtiles
    for b in range(bb):
        for k in range(heads // g):
            pieces = [(kt_ref[b, k, :, c * chunk:(c + 1) * chunk], v_ref[b, k, c * chunk:(c + 1) * chunk, :])
                      for c in range(seq // chunk)]
            if n_ctx:
                pieces.append((ckt_ref[b, k], cv_ref[b, k]))
            for t in range(subtiles):
                rows = slice(t * ts, (t + 1) * ts)
                qs = q_ref[b, k * g:(k + 1) * g, rows, :].reshape(g * ts, HEAD_DIM)
                m = acc = None
                for kt_c, v_c in pieces:
                    s = _dot(qs, kt_c)
                    row_max = jnp.max(s, axis=1, keepdims=True)
                    m_new = row_max if m is None else jnp.maximum(m, row_max)
                    pv = _dot(jnp.exp2(s - m_new).astype(BF16), v_c)
                    acc = pv if m is None else jnp.exp2(m - m_new) * acc + pv
                    m = m_new
                o = acc[:, :HEAD_DIM] / acc[:, HEAD_DIM:HEAD_DIM + 1]
                o = o * lax.rsqrt(jnp.mean(o * o, axis=1, keepdims=True) + EPS)
                for i in range(g):
                    cols = slice((k * g + i) * HEAD_DIM, (k * g + i + 1) * HEAD_DIM)
                    o_ref[b, rows, cols] = (o[i * ts:(i + 1) * ts] * gain_ref[:, cols]).astype(o_ref.dtype)


def _attention(q, kt, v, gain, ckt, cv):
    nb, _, seq, _ = q.shape
    n_ctx = 0 if ckt is None else ckt.shape[-1]
    chunk = _pick_tile(seq, 2048)
    if seq >= ATTN_LONG_SEQ:
        bb, kv, tq, subtiles = 1, 1, _pick_tile(seq, 512), ATTN_SUBTILES
    else:
        bb, kv, tq, subtiles = _pick_tile(nb, 4), N_KV_HEADS, seq, 1
    width = kv * GQA_GROUP * HEAD_DIM
    in_specs = [
        pl.BlockSpec((bb, kv * GQA_GROUP, tq, HEAD_DIM), lambda b, k, i: (b, k, i, 0)),
        pl.BlockSpec((bb, kv, HEAD_DIM, seq), lambda b, k, i: (b, k, 0, 0)),
        pl.BlockSpec((bb, kv, seq, KV_DIM), lambda b, k, i: (b, k, 0, 0)),
        pl.BlockSpec((1, width), lambda b, k, i: (0, k)),
    ]
    args = [q, kt, v, gain]
    if n_ctx:
        in_specs += [
            pl.BlockSpec((bb, kv, HEAD_DIM, n_ctx), lambda b, k, i: (b, k, 0, 0)),
            pl.BlockSpec((bb, kv, n_ctx, KV_DIM), lambda b, k, i: (b, k, 0, 0)),
        ]
        args += [ckt, cv]
    return pl.pallas_call(
        functools.partial(_attn_kernel, n_ctx, chunk, subtiles),
        out_shape=jax.ShapeDtypeStruct((nb, seq, Q_DIM), BF16),
        grid=(nb // bb, N_KV_HEADS // kv, seq // tq),
        in_specs=in_specs,
        out_specs=pl.BlockSpec((bb, tq, width), lambda b, k, i: (b, i, k)),
        compiler_params=_cparams(("parallel", "parallel", "parallel")),
    )(*args)


def _hyena_decimation(seq):
    return 8 if seq >= 2048 else 1


def _dft_tables(seq, r):
    n_sub = 2 * seq // r
    half = n_sub // 2
    k = np.arange(half, dtype=np.float64)[:, None]
    m = np.arange(n_sub, dtype=np.float64)[None, :]
    ang = 2.0 * np.pi * k * m / n_sub
    fwd = np.concatenate([np.cos(ang), -np.sin(ang)], axis=0)
    fwd[half] = np.cos(np.pi * m[0])
    inv = fwd.T.copy() * (2.0 / n_sub)
    inv[:, 0] *= 0.5
    inv[:, half] *= 0.5
    inv = inv[:seq // r]
    kk = np.arange(half, dtype=np.float64)[:, None] * np.ones((1, LANES))
    tw_r = np.cos(2.0 * np.pi * kk / n_sub)
    tw_i = -np.sin(2.0 * np.pi * kk / n_sub)
    return fwd, inv, tw_r, tw_i


def _filter_positions(seq, r):
    n_tot = 2 * seq
    n = (np.arange(n_tot // r)[None, :] * r + np.arange(r)[:, None]).reshape(-1)
    j = np.where(n < seq, n, n_tot - n)
    t = j.astype(np.float64) / seq
    bands = np.arange(1, HY_BANDS + 1, dtype=np.float64)
    ang = 2.0 * np.pi * t[:, None] * bands
    z = np.concatenate([t[:, None], np.sin(ang), np.cos(ang)], axis=-1)
    ones = np.ones((1, HY_FILTER_HIDDEN))
    sel_f = (n < seq).astype(np.float64)[:, None] * ones
    sel_b = (n > seq).astype(np.float64)[:, None] * ones
    return z, t[:, None] * np.ones((1, LANES)), sel_f, sel_b


def _filter_ffn_kernel(z_ref, self_ref, selb_ref, w1_ref, b1_ref, w2_ref, b2_ref, fr_ref, hf_ref, hb_ref):
    fr = fr_ref[...]
    h = jnp.sin(fr * (_dot3(z_ref[...], w1_ref[...]) + b1_ref[...]))
    h = jnp.sin(fr * (_dot3(h, w2_ref[...]) + b2_ref[...]))
    hf_ref[...] = h * self_ref[...]
    hb_ref[...] = h * selb_ref[...]


def _filter_spec_kernel(r, hf_ref, hb_ref, t_ref, w3f_ref, w3b_ref, dcf_ref, dcb_ref, fh_ref, ga_ref, gb_ref, gc_ref):
    t = t_ref[...]
    g = (_dot3(hf_ref[...], w3f_ref[...]) * jnp.exp(-t * jnp.abs(dcf_ref[...]))
         + _dot3(hb_ref[...], w3b_ref[...]) * jnp.exp(-t * jnp.abs(dcb_ref[...])))
    g = g * lax.rsqrt(jnp.sum(g * g, axis=0, keepdims=True) + EPS)
    n_sub = g.shape[0] // r
    half = n_sub // 2
    fh = fh_ref[...]
    for p in range(r):
        spec = _dot(fh, g[p * n_sub:(p + 1) * n_sub].astype(BF16))
        ga_ref[p] = spec[:half]
        gb_ref[p] = spec[half:] - spec[:half]
        gc_ref[p] = spec[half:] + spec[:half]


def _hyena_filters(seq, r, w1, b1, w2, b2, w3, freq, decay, fwd):
    n_tot = 2 * seq
    n_sub = n_tot // r
    half = n_sub // 2
    z, t, sel_f, sel_b = _filter_positions(seq, r)
    pad = (-HY_POS_DIM) % SUBLANES
    z = jnp.asarray(np.pad(z, ((0, 0), (0, pad))), F32)
    w1p = jnp.pad(w1, ((0, pad), (0, 0)))
    kin = HY_POS_DIM + pad
    hid = HY_FILTER_HIDDEN
    tr = min(n_tot, 512)
    rows = lambda width: pl.BlockSpec((tr, width), lambda i: (i, 0))
    full = lambda shape: pl.BlockSpec(shape, lambda j: (0,) * len(shape))
    hf, hb = pl.pallas_call(
        _filter_ffn_kernel,
        out_shape=[jax.ShapeDtypeStruct((n_tot, hid), F32)] * 2,
        grid=(n_tot // tr,),
        in_specs=[rows(kin), rows(hid), rows(hid), full((kin, hid)), full((1, hid)), full((hid, hid)),
                  full((1, hid)), full((1, hid))],
        out_specs=[rows(hid)] * 2,
        compiler_params=_cparams(("parallel",)),
    )(z, jnp.asarray(sel_f, F32), jnp.asarray(sel_b, F32), w1p, b1.reshape(1, hid), w2, b2.reshape(1, hid),
      freq.reshape(1, hid))
    ncb = HY_OC // LANES
    return pl.pallas_call(
        functools.partial(_filter_spec_kernel, r),
        out_shape=[jax.ShapeDtypeStruct((r, half, HY_OC), F32)] * 3,
        grid=(ncb,),
        in_specs=[
            full((n_tot, hid)), full((n_tot, hid)), full((n_tot, LANES)),
            pl.BlockSpec((hid, LANES), lambda j: (0, j)),
            pl.BlockSpec((hid, LANES), lambda j: (0, j + ncb)),
            pl.BlockSpec((1, LANES), lambda j: (0, j)),
            pl.BlockSpec((1, LANES), lambda j: (0, j + ncb)),
            full((2 * half, n_sub)),
        ],
        out_specs=[pl.BlockSpec((r, half, LANES), lambda j: (0, 0, j))] * 3,
        compiler_params=_cparams(("parallel",)),
    )(hf, hb, jnp.asarray(t, F32), w3, w3, decay.reshape(1, -1), decay.reshape(1, -1), fwd)


def _hyena_kernel(r, hy0_ref, hy1_ref, hy2_ref, sw_ref, sb_ref, skip_ref,
                  ga0_ref, gb0_ref, gc0_ref, ga1_ref, gb1_ref, gc1_ref, fwd_ref, inv_ref, twr_ref, twi_ref,
                  o_ref, z_ref, ph_ref, rhs_ref, x_ref):
    seq = hy0_ref.shape[0]
    m_len = seq // r
    half = fwd_ref.shape[0] // 2
    row = lax.broadcasted_iota(jnp.int32, (m_len, LANES), 0)
    hy_refs = (hy0_ref, hy1_ref, hy2_ref)

    def load_phases(part):
        for j in range(r):
            ph_ref[j] = hy_refs[part][pl.ds(j, m_len, stride=r), :]

    def short_conv(part, j):
        w = sw_ref[:, part * LANES:(part + 1) * LANES]
        b = sb_ref[:, part * LANES:(part + 1) * LANES]
        if j > 0:
            prev = ph_ref[j - 1]
        else:
            prev = jnp.where(row == 0, 0.0, pltpu.roll(ph_ref[r - 1], 1, axis=0))
        if j < r - 1:
            nxt = ph_ref[j + 1]
        else:
            nxt = jnp.where(row == m_len - 1, 0.0, pltpu.roll(ph_ref[0], m_len - 1, axis=0))
        return prev * w[0:1] + ph_ref[j] * w[1:2] + nxt * w[2:3] + b

    load_phases(0)
    for j in range(r):
        z_ref[j] = short_conv(0, j)

    for o, (ga_ref, gb_ref, gc_ref) in enumerate(((ga0_ref, gb0_ref, gc0_ref), (ga1_ref, gb1_ref, gc1_ref))):
        for j in range(r):
            rhs_ref[:, j * LANES:(j + 1) * LANES] = z_ref[j].astype(BF16)
        x_ref[...] = _dot(fwd_ref[...], rhs_ref[...])
        dc = [x_ref[0:1, j * LANES:(j + 1) * LANES] for j in range(r)]
        ny = [x_ref[half:half + 1, j * LANES:(j + 1) * LANES] for j in range(r)]
        y_dc, y_ny = [], []
        for j in range(r):
            a = jnp.zeros((1, LANES), F32)
            c = jnp.zeros((1, LANES), F32)
            for jp in range(r):
                p = (j - jp) % r
                a = a + ga_ref[p, 0:1, :] * dc[jp]
                t = (ga_ref[p, 0:1, :] + gb_ref[p, 0:1, :]) * ny[jp]
                c = c + t if jp <= j else c - t
            y_dc.append(a)
            y_ny.append(c)

        def mix(i, carry):
            r0 = pl.multiple_of(i * SUBLANES, SUBLANES)
            rows_re = pl.ds(r0, SUBLANES)
            rows_im = pl.ds(half + r0, SUBLANES)
            xr = [x_ref[rows_re, j * LANES:(j + 1) * LANES] for j in range(r)]
            xi = [x_ref[rows_im, j * LANES:(j + 1) * LANES] for j in range(r)]
            xs = [a + b for a, b in zip(xr, xi)]
            wr = twr_ref[rows_re, :]
            wi = twi_ref[rows_re, :]
            for j in range(r):
                acc = {}
                for jp in range(r):
                    p = (j - jp) % r
                    k1 = ga_ref[p, rows_re, :] * xs[jp]
                    k2 = gb_ref[p, rows_re, :] * xr[jp]
                    k3 = gc_ref[p, rows_re, :] * xi[jp]
                    side = jp <= j
                    ks = (k1, k2, k3)
                    acc[side] = ks if side not in acc else tuple(a + b for a, b in zip(acc[side], ks))
                pr = acc[True][0] - acc[True][2]
                pi = acc[True][0] + acc[True][1]
                if False in acc:
                    qr = acc[False][0] - acc[False][2]
                    qi = acc[False][0] + acc[False][1]
                    pr = pr + wr * qr - wi * qi
                    pi = pi + wr * qi + wi * qr
                x_ref[rows_re, j * LANES:(j + 1) * LANES] = pr
                x_ref[rows_im, j * LANES:(j + 1) * LANES] = pi
            return carry

        lax.fori_loop(0, half // SUBLANES, mix, 0)
        for j in range(r):
            x_ref[0:1, j * LANES:(j + 1) * LANES] = y_dc[j]
            x_ref[half:half + 1, j * LANES:(j + 1) * LANES] = y_ny[j]
        y = _dot(inv_ref[...], x_ref[...].astype(BF16))
        sk = skip_ref[o:o + 1, :]
        load_phases(o + 1)
        for j in range(r):
            z_ref[j] = short_conv(o + 1, j) * (y[:, j * LANES:(j + 1) * LANES] + z_ref[j] * sk)
    for j in range(r):
        o_ref[pl.ds(j, m_len, stride=r), :] = z_ref[j]


def _hyena_direct_kernel(hy_ref, sw_ref, sb_ref, skip_ref, ga_ref, gb_ref, fwd_ref, inv_ref, o_ref):
    bb, seq, _ = hy_ref.shape
    half = fwd_ref.shape[0] // 2
    w = HYENA_WIDTH
    row = lax.broadcasted_iota(jnp.int32, (seq, w), 0)
    is_dc = lax.broadcasted_iota(jnp.int32, (half, w), 0) == 0

    def short_conv(b, part):
        cols = slice(part * w, (part + 1) * w)
        x = hy_ref[b, :, cols]
        prev = jnp.where(row == 0, 0.0, pltpu.roll(x, 1, axis=0))
        nxt = jnp.where(row == seq - 1, 0.0, pltpu.roll(x, seq - 1, axis=0))
        return prev * sw_ref[0:1, cols] + x * sw_ref[1:2, cols] + nxt * sw_ref[2:3, cols] + sb_ref[:, cols]

    for b in range(bb):
        z = short_conv(b, 0)
        for o in range(HY_ORDER):
            cols = slice(o * w, (o + 1) * w)
            x = _dot(fwd_ref[...], z.astype(BF16))
            xr, xi = x[:half], x[half:]
            gr = ga_ref[0, :, cols]
            gi = gb_ref[0, :, cols] + gr
            vr = gr * xr - jnp.where(is_dc, 0.0, gi * xi)
            vi = jnp.where(is_dc, gi * xi, gr * xi + gi * xr)
            y = _dot(inv_ref[...], jnp.concatenate([vr, vi], axis=0).astype(BF16))
            z = short_conv(b, o + 1) * (y + z * skip_ref[o:o + 1, :])
        o_ref[b] = z


def _hyena_direct(hy, short_w, short_b, skip, ga, gb, fwd, inv):
    nb, seq, _ = hy.shape
    bb = _pick_tile(nb, 4)
    const = lambda a: pl.BlockSpec(a.shape, lambda i: (0,) * a.ndim)
    short_b = short_b.reshape(1, -1)
    return pl.pallas_call(
        _hyena_direct_kernel,
        out_shape=jax.ShapeDtypeStruct((nb, seq, HYENA_WIDTH), F32),
        grid=(nb // bb,),
        in_specs=[pl.BlockSpec((bb, seq, HY_IN), lambda i: (i, 0, 0)), const(short_w), const(short_b), const(skip),
                  const(ga), const(gb), const(fwd), const(inv)],
        out_specs=pl.BlockSpec((bb, seq, HYENA_WIDTH), lambda i: (i, 0, 0)),
        compiler_params=_cparams(("parallel",)),
    )(hy, short_w, short_b, skip, ga, gb, fwd, inv)


def _hyena(hy, short_w, short_b, skip, spectra, fwd, inv, tw_r, tw_i, r):
    ga, gb, gc = spectra
    if r == 1:
        return _hyena_direct(hy, short_w, short_b, skip, ga, gb, fwd, inv)
    nb, seq, _ = hy.shape
    m_len = seq // r
    n_half2 = fwd.shape[0]
    half = n_half2 // 2
    ncb = HYENA_WIDTH // LANES
    parts = HY_ORDER + 1
    once = pl.Buffered(1)
    hy_spec = lambda part: pl.BlockSpec((None, seq, LANES), lambda c, b: (b, 0, part * ncb + c))
    g_spec = lambda o: pl.BlockSpec((r, half, LANES), lambda c, b: (0, 0, o * ncb + c), pipeline_mode=once)
    const = lambda shape: pl.BlockSpec(shape, lambda c, b: (0,) * len(shape), pipeline_mode=once)
    sw = short_w.reshape(3, parts, ncb, LANES).transpose(2, 0, 1, 3).reshape(ncb, 3, parts * LANES)
    sb = short_b.reshape(1, parts, ncb, LANES).transpose(2, 0, 1, 3).reshape(ncb, 1, parts * LANES)
    return pl.pallas_call(
        functools.partial(_hyena_kernel, r),
        out_shape=jax.ShapeDtypeStruct((nb, seq, HYENA_WIDTH), F32),
        grid=(ncb, nb),
        in_specs=[
            hy_spec(0), hy_spec(1), hy_spec(2),
            pl.BlockSpec((None, 3, parts * LANES), lambda c, b: (c, 0, 0)),
            pl.BlockSpec((None, 1, parts * LANES), lambda c, b: (c, 0, 0)),
            pl.BlockSpec((HY_ORDER, LANES), lambda c, b: (0, c)),
            g_spec(0), g_spec(0), g_spec(0), g_spec(1), g_spec(1), g_spec(1),
            const((n_half2, m_len)), const((m_len, n_half2)),
            const((half, LANES)), const((half, LANES)),
        ],
        out_specs=pl.BlockSpec((None, seq, LANES), lambda c, b: (b, 0, c)),
        scratch_shapes=[
            pltpu.VMEM((r, m_len, LANES), F32),
            pltpu.VMEM((r, m_len, LANES), F32),
            pltpu.VMEM((m_len, r * LANES), BF16),
            pltpu.VMEM((n_half2, r * LANES), F32),
        ],
        compiler_params=_cparams(("parallel", "parallel")),
    )(hy, hy, hy, sw, sb, skip, ga, gb, gc, ga, gb, gc, fwd, inv, tw_r, tw_i)


def _route(logits):
    lane = lax.broadcasted_iota(jnp.int32, logits.shape, 1).astype(F32)
    big = jnp.float32(1e9)
    is_grp = (lane >= N_EXPERTS) & (lane < N_EXPERTS + N_GROUPS)
    gl = jnp.where(is_grp, logits, NEG_BIG)
    gmax = jnp.max(gl, axis=1, keepdims=True)
    gidx = jnp.min(jnp.where(gl == gmax, lane, big), axis=1, keepdims=True) - N_EXPERTS
    den = jnp.sum(jnp.where(is_grp, jnp.exp(gl - gmax), 0.0), axis=1, keepdims=True)
    pg_top = 1.0 / den
    lo = gidx * EXPERTS_PER_GROUP
    sel = jnp.where((lane >= lo) & (lane < lo + EXPERTS_PER_GROUP), logits, NEG_BIG)
    m1 = jnp.max(sel, axis=1, keepdims=True)
    i1 = jnp.min(jnp.where(sel == m1, lane, big), axis=1, keepdims=True)
    sel2 = jnp.where(lane == i1, NEG_BIG, sel)
    m2 = jnp.max(sel2, axis=1, keepdims=True)
    i2 = jnp.min(jnp.where(sel2 == m2, lane, big), axis=1, keepdims=True)
    e2 = jnp.exp(m2 - m1)
    w1 = pg_top / (1.0 + e2)
    w2 = pg_top * e2 / (1.0 + e2)
    comb = jnp.where(lane == i1, w1, 0.0) + jnp.where(lane == i2, w2, 0.0)
    return comb + jnp.where(lane == GROUP_ID_LANE, gidx, 0.0)


def _outproj_kernel(attn_ref, hyz_ref, x_ref, mod_ref, wo_ref, hg_ref, bd_ref, lng_ref, lnb_ref,
                    rwh_ref, rwl_ref, rb_ref, x1_ref, tok_ref, rec_ref):
    m = mod_ref[...]
    tm = x_ref.shape[0]
    n_sub = _proj_subtiles(tm)
    ts = tm // n_sub
    half = wo_ref.shape[0] // 2
    for t in range(n_sub):
        rows = slice(t * ts, (t + 1) * ts)
        z = hyz_ref[rows, :]
        ms = _dot((z * z).astype(BF16), bd_ref[...])
        zn = (z * lax.rsqrt(ms + EPS) * hg_ref[...]).astype(BF16)
        mix = _dot(attn_ref[rows, :], wo_ref[:half, :]) + _dot(zn, wo_ref[half:, :])
        x1 = _layer_norm(DEEPNORM_ALPHA * x_ref[rows, :] + m[2:3] * mix, lng_ref[...], lnb_ref[...])
        h2 = x1 * (1.0 + m[4:5]) + m[3:4]
        x1_ref[rows, :] = x1
        tok_ref[rows, :D_MODEL] = h2
        hh, hl = _split_bf16(h2)
        logits = _dot(hh, rwh_ref[...]) + _dot(hl, rwh_ref[...]) + _dot(hh, rwl_ref[...]) + rb_ref[...]
        record = _route(logits)
        tok_ref[rows, D_MODEL:] = record
        rec_ref[rows, :] = record


def _outproj(attn, hyz, x, mod, w_out, hy_gain, bd_hy, ln_g, ln_b, rw_hi, rw_lo, rb, latent, tm):
    nb, seq, _ = x.shape
    mod_map = (lambda b, i: (b, 0, 0)) if latent else (lambda b, i: (0, 0, 0))
    const = lambda shape: pl.BlockSpec(shape, lambda b, i: (0,) * len(shape))
    tok = lambda width: pl.BlockSpec((None, tm, width), lambda b, i: (b, i, 0))
    return pl.pallas_call(
        _outproj_kernel,
        out_shape=[
            jax.ShapeDtypeStruct((nb, seq, D_MODEL), F32),
            jax.ShapeDtypeStruct((nb, seq, D_MODEL + LANES), F32),
            jax.ShapeDtypeStruct((nb, seq, LANES), F32),
        ],
        grid=(nb, seq // tm),
        in_specs=[
            tok(Q_DIM), tok(HYENA_WIDTH), tok(D_MODEL),
            pl.BlockSpec((None, 6, D_MODEL), mod_map),
            const((D_MODEL, D_MODEL)), const((1, HYENA_WIDTH)), const((HYENA_WIDTH, HYENA_WIDTH)),
            const((1, D_MODEL)), const((1, D_MODEL)),
            const((D_MODEL, LANES)), const((D_MODEL, LANES)), const((1, LANES)),
        ],
        out_specs=[tok(D_MODEL), tok(D_MODEL + LANES), tok(LANES)],
        compiler_params=_cparams(("parallel", "parallel")),
    )(attn, hyz, x, mod, w_out, hy_gain, bd_hy, ln_g, ln_b, rw_hi, rw_lo, rb)


def _gather_rows(table, idx):
    n, d = idx.shape[0], table.shape[1]
    info = plsc.get_sparse_core_info()
    workers = info.num_cores * info.num_subcores
    per_worker = n // workers
    assert per_worker * workers == n and per_worker % GATHER_ROWS == 0
    mesh = plsc.VectorSubcoreMesh(core_axis_name="c", subcore_axis_name="s")

    @functools.partial(
        pl.kernel, mesh=mesh,
        out_type=jax.ShapeDtypeStruct((n, d), table.dtype),
        scratch_types=[pltpu.VMEM((GATHER_ROWS,), jnp.int32), pltpu.VMEM((GATHER_ROWS, d), table.dtype),
                       pltpu.SemaphoreType.DMA],
    )
    def gather(table_hbm, idx_hbm, out_hbm, idx_v, rows_v, sem):
        wid = lax.axis_index("s") * info.num_cores + lax.axis_index("c")
        base = wid * per_worker

        @pl.loop(0, per_worker // GATHER_ROWS)
        def _(j):
            off = base + j * GATHER_ROWS
            pltpu.sync_copy(idx_hbm.at[pl.ds(off, GATHER_ROWS)], idx_v)
            pltpu.async_copy(table_hbm.at[idx_v], rows_v, sem).wait()
            pltpu.sync_copy(rows_v, out_hbm.at[pl.ds(off, GATHER_ROWS)])

    return gather(table, idx)


def _sort_plan(record, tile):
    tokens = record.shape[0]
    gid = record[:, GROUP_ID_LANE].astype(jnp.int32)
    onehot = (gid[:, None] == jnp.arange(N_GROUPS, dtype=jnp.int32)[None, :]).astype(jnp.int32)
    csum = jnp.cumsum(onehot, axis=0)
    counts = csum[-1]
    rank = jnp.sum(csum * onehot, axis=1) - 1
    padded = (counts + tile - 1) // tile * tile
    ends = jnp.cumsum(padded)
    starts = ends - padded
    pos = jnp.sum(starts[None, :] * onehot, axis=1) + rank
    n_pad = tokens + N_GROUPS * tile
    filler = jnp.arange(n_pad - tokens, dtype=jnp.int32)
    filler_gid = jnp.sum((filler[:, None] >= jnp.cumsum(padded - counts)[None, :]).astype(jnp.int32), axis=1)
    _, src = lax.sort((jnp.concatenate([gid, filler_gid]),
                       jnp.concatenate([jnp.arange(tokens, dtype=jnp.int32), jnp.zeros_like(filler)])),
                      num_keys=1, is_stable=True)
    tile_start = jnp.arange(n_pad // tile, dtype=jnp.int32) * tile
    tile_group = jnp.minimum(jnp.sum((tile_start[:, None] >= ends[None, :]).astype(jnp.int32), axis=1), N_GROUPS - 1)
    tile_valid = (tile_start < ends[-1]).astype(jnp.int32)
    return pos, src, tile_group, tile_valid


def _moe_kernel(tg_ref, tv_ref, x_ref, wg32_ref, wu32_ref, wd32_ref, o_ref, wg_ref, wu_ref, wd_ref):
    i = pl.program_id(0)

    @pl.when((i == 0) | (tg_ref[i] != tg_ref[jnp.maximum(i - 1, 0)]))
    def _():
        for e in range(EXPERTS_PER_GROUP):
            wg_ref[e] = wg32_ref[e].astype(BF16)
            wu_ref[e] = wu32_ref[e].astype(BF16)
            wd_ref[e] = wd32_ref[e].astype(BF16)

    @pl.when(tv_ref[i] > 0)
    def _():
        first = tg_ref[i] * EXPERTS_PER_GROUP
        x = x_ref[:, :D_MODEL].astype(BF16)
        rec = x_ref[:, D_MODEL:]
        lane = lax.broadcasted_iota(jnp.int32, rec.shape, 1)
        y = None
        for e in range(EXPERTS_PER_GROUP):
            c = jnp.sum(jnp.where(lane == first + e, rec, 0.0), axis=1, keepdims=True)
            h = _silu(_dot(x, wg_ref[e])) * _dot(x, wu_ref[e]) * c
            part = _dot(h.astype(BF16), wd_ref[e])
            y = part if y is None else y + part
        o_ref[...] = y

    @pl.when(tv_ref[i] == 0)
    def _():
        o_ref[...] = jnp.zeros_like(o_ref)


def _moe(sorted_tok, tile_group, tile_valid, wg, wu, wd):
    rows = sorted_tok.shape[0]
    tile = MOE_TILE
    grp = lambda shape: pl.BlockSpec(shape, lambda i, tg, tv: (tg[i], 0, 0), pipeline_mode=pl.Buffered(1))
    return pl.pallas_call(
        _moe_kernel,
        out_shape=jax.ShapeDtypeStruct((rows, D_MODEL), F32),
        grid_spec=pltpu.PrefetchScalarGridSpec(
            num_scalar_prefetch=2,
            grid=(rows // tile,),
            in_specs=[
                pl.BlockSpec((tile, D_MODEL + LANES), lambda i, tg, tv: (i, 0)),
                grp((EXPERTS_PER_GROUP, D_MODEL, EXPERT_FF)), grp((EXPERTS_PER_GROUP, D_MODEL, EXPERT_FF)),
                grp((EXPERTS_PER_GROUP, EXPERT_FF, D_MODEL)),
            ],
            out_specs=pl.BlockSpec((tile, D_MODEL), lambda i, tg, tv: (i, 0)),
            scratch_shapes=[
                pltpu.VMEM((EXPERTS_PER_GROUP, D_MODEL, EXPERT_FF), BF16),
                pltpu.VMEM((EXPERTS_PER_GROUP, D_MODEL, EXPERT_FF), BF16),
                pltpu.VMEM((EXPERTS_PER_GROUP, EXPERT_FF, D_MODEL), BF16),
            ],
        ),
        compiler_params=_cparams(("arbitrary",)),
    )(tile_group, tile_valid, sorted_tok, wg, wu, wd)


def _final_norm_kernel(x1_ref, y_ref, mod_ref, lng_ref, lnb_ref, o_ref):
    o_ref[...] = _layer_norm(DEEPNORM_ALPHA * x1_ref[...] + mod_ref[5:6, :] * y_ref[...], lng_ref[...], lnb_ref[...])


def _final_norm(x1, y, mod, ln_g, ln_b, latent, tm):
    nb, seq, _ = x1.shape
    mod_map = (lambda b, i: (b, 0, 0)) if latent else (lambda b, i: (0, 0, 0))
    tok = pl.BlockSpec((None, tm, D_MODEL), lambda b, i: (b, i, 0))
    const = pl.BlockSpec((1, D_MODEL), lambda b, i: (0, 0))
    return pl.pallas_call(
        _final_norm_kernel,
        out_shape=jax.ShapeDtypeStruct((nb, seq, D_MODEL), F32),
        grid=(nb, seq // tm),
        in_specs=[tok, tok, pl.BlockSpec((None, 6, D_MODEL), mod_map), const, const],
        out_specs=tok,
        compiler_params=_cparams(("parallel", "parallel")),
    )(x1, y, mod, ln_g, ln_b)


def _block_diag_mean(width, group):
    idx = np.arange(width) // group
    return jnp.asarray((idx[:, None] == idx[None, :]).astype(np.float32) / group, BF16)


def _pick_tile(n, target):
    t = min(n, target)
    while n % t:
        t //= 2
    return t


def _trunk_layer(x, mod, p, latent, ctx_k=None, ctx_v=None):
    nb, seq, _ = x.shape
    tm = _pick_tile(seq, 1024)
    q, kt, vx, hy, *kv_new = _inproj(x, mod, p['w_in'], p['qk_gain'], p['bd_qk'], latent, tm)
    if latent:
        ckt = jnp.transpose(ctx_k, (0, 2, 3, 1)).astype(BF16)
        cv = jnp.transpose(ctx_v, (0, 2, 1, 3))
        cvx = jnp.concatenate([cv, jnp.ones_like(cv)], axis=-1).astype(BF16)
    else:
        ckt = cvx = None
    attn = _attention(q, kt, vx, p['attn_gain'], ckt, cvx)

    r = _hyena_decimation(seq)
    fwd, inv, tw_r, tw_i = _dft_tables(seq, r)
    fwd = jnp.asarray(fwd, F32).astype(BF16)
    spectra = _hyena_filters(seq, r, p['hy_f_w1'], p['hy_f_b1'], p['hy_f_w2'], p['hy_f_b2'], p['hy_f_w3'],
                              p['hy_freq'], p['hy_decay'], fwd)
    hyz = _hyena(hy, p['hy_short_w'], p['hy_short_b'], p['hy_skip'], spectra,
                 fwd[:, :seq // r], jnp.asarray(inv, F32).astype(BF16), jnp.asarray(tw_r, F32),
                 jnp.asarray(tw_i, F32), r)

    x1, tok, record = _outproj(attn, hyz, x, mod, p['w_out'], p['hy_gain'], p['bd_hy'], p['ln1_g'], p['ln1_b'],
                               p['rw_hi'], p['rw_lo'], p['rb'], latent, tm)
    tok = tok.reshape(nb * seq, D_MODEL + LANES)
    pos, src, tile_group, tile_valid = _sort_plan(record.reshape(nb * seq, LANES), MOE_TILE)
    y_sorted = _moe(_gather_rows(tok, src), tile_group, tile_valid, p['wg'], p['wu'], p['wd'])
    y = _gather_rows(y_sorted, pos).reshape(nb, seq, D_MODEL)
    return _final_norm(x1, y, mod, p['ln2_g'], p['ln2_b'], latent, tm), kv_new


def _prepare(w_in, q_gain, k_gain, attn_out_gain, hy_short_w, hy_short_b, hy_f_w1, hy_f_b1, hy_f_w2, hy_f_b2,
             hy_f_w3, hy_freq, hy_decay, hy_skip, hy_out_gain, w_out, ln1_g, ln1_b, router_grp_w, router_grp_b,
             router_exp_w, router_exp_b, exp_w_gate, exp_w_up, exp_w_down, ln2_g, ln2_b, l):
    row = lambda a: a.reshape(1, -1)
    rw = jnp.concatenate([router_exp_w[l], router_grp_w[l]], axis=1)
    rw = jnp.pad(rw, ((0, 0), (0, LANES - rw.shape[1])))
    rb = jnp.concatenate([router_exp_b[l], router_grp_b[l]])
    rb = jnp.pad(rb, (0, LANES - rb.shape[0]))
    rw_hi, rw_lo = _split_bf16(rw)
    return {
        'w_in': w_in[l].astype(BF16),
        'qk_gain': row(jnp.concatenate([jnp.tile(q_gain[l], N_HEADS), jnp.tile(k_gain[l], N_KV_HEADS)])),
        'bd_qk': _block_diag_mean(QK_DIM, HEAD_DIM),
        'attn_gain': row(attn_out_gain[l]),
        'hy_short_w': hy_short_w[l], 'hy_short_b': hy_short_b[l],
        'hy_f_w1': hy_f_w1[l], 'hy_f_b1': hy_f_b1[l], 'hy_f_w2': hy_f_w2[l], 'hy_f_b2': hy_f_b2[l],
        'hy_f_w3': hy_f_w3[l], 'hy_freq': hy_freq[l], 'hy_decay': hy_decay[l], 'hy_skip': hy_skip[l],
        'hy_gain': row(hy_out_gain[l]),
        'bd_hy': _block_diag_mean(HYENA_WIDTH, HY_GROUP_DIM),
        'w_out': w_out[l].astype(BF16),
        'ln1_g': row(ln1_g[l]), 'ln1_b': row(ln1_b[l]),
        'rw_hi': rw_hi, 'rw_lo': rw_lo, 'rb': row(rb),
        'wg': exp_w_gate[l], 'wu': exp_w_up[l], 'wd': exp_w_down[l],
        'ln2_g': row(ln2_g[l]), 'ln2_b': row(ln2_b[l]),
    }


def kernel(x_prompt, x_sample, c, cache_k, cache_v, c_ctx, w_mod, b_mod, w_in, q_gain, k_gain, attn_out_gain, hy_short_w, hy_short_b, hy_f_w1, hy_f_b1, hy_f_w2, hy_f_b2, hy_f_w3, hy_freq, hy_decay, hy_skip, hy_out_gain, w_out, ln1_g, ln1_b, router_grp_w, router_grp_b, router_exp_w, router_exp_b, exp_w_gate, exp_w_up, exp_w_down, ln2_g, ln2_b):
    depth = w_mod.shape[0]
    n_lat = c.shape[0]
    cond = jnp.concatenate([c, c_ctx[None, :]], axis=0)
    rows = -(-cond.shape[0] // SUBLANES) * SUBLANES
    cond = jnp.pad(cond, ((0, rows - cond.shape[0]), (0, 0)))
    y_prompt, y_sample = x_prompt, x_sample
    ks_new, vs_new = [], []
    for l in range(depth):
        p = _prepare(w_in, q_gain, k_gain, attn_out_gain, hy_short_w, hy_short_b, hy_f_w1, hy_f_b1, hy_f_w2,
                     hy_f_b2, hy_f_w3, hy_freq, hy_decay, hy_skip, hy_out_gain, w_out, ln1_g, ln1_b,
                     router_grp_w, router_grp_b, router_exp_w, router_exp_b, exp_w_gate, exp_w_up, exp_w_down,
                     ln2_g, ln2_b, l)
        mod = _adaln(cond, w_mod[l], b_mod[l])
        mod_lat = mod[:n_lat].reshape(n_lat, 6, D_MODEL)
        mod_ctx = mod[n_lat:n_lat + 1].reshape(1, 6, D_MODEL)
        y_prompt, (v_new, k_new) = _trunk_layer(y_prompt, mod_ctx, p, False)
        ks_new.append(k_new.reshape(k_new.shape[:2] + (N_KV_HEADS, HEAD_DIM)))
        vs_new.append(v_new.reshape(v_new.shape[:2] + (N_KV_HEADS, HEAD_DIM)))
        y_sample, _ = _trunk_layer(y_sample, mod_lat, p, True, cache_k[:, l], cache_v[:, l])
    return (y_prompt, y_sample, jnp.stack(ks_new, axis=1), jnp.stack(vs_new, axis=1))
```
